```python
import math
import jax
import jax.numpy as jnp
from jax import lax
import numpy as np


D_MODEL = 1024
BATCH = 1
SEQ = 16384
DEPTH = 2

CHUNK = 64
DN_HEADS = 4
DN_DK = 128
DN_DV = 128
CONV_K = 4
SB_HEADS = 8
SB_DH = 64
SB_BLOCK = 128
N_EXPERTS = 16
N_GROUPS = 4
EXPERTS_PER_GROUP = N_EXPERTS // N_GROUPS
TOP_K = 2
D_FF_EXPERT = 512

LN_EPS = 1e-5
RMS_EPS = 1e-6
DEEPNORM_ALPHA = (2 * DEPTH) ** 0.25
DEEPNORM_BETA = (8 * DEPTH) ** -0.25

DN_WIDTH = DN_HEADS * DN_DK
DN_VWIDTH = DN_HEADS * DN_DV
SB_WIDTH = SB_HEADS * SB_DH
IN_WIDTHS = (DN_WIDTH, DN_WIDTH, DN_VWIDTH, DN_VWIDTH, DN_HEADS, DN_HEADS,
             SB_WIDTH, SB_WIDTH, SB_WIDTH, D_MODEL, D_MODEL)
D_IN = sum(IN_WIDTHS)

kernel_name = 'hybrid_deltanet_stickbreaking_grouped_moe'


def layer_norm(x, g, b):
    xf = x.astype(jnp.float32)
    mu = jnp.mean(xf, axis=-1, keepdims=True)
    var = jnp.mean(jnp.square(xf - mu), axis=-1, keepdims=True)
    return ((xf - mu) * lax.rsqrt(var + LN_EPS) * g + b).astype(x.dtype)


def l2norm(x):
    return x * lax.rsqrt(jnp.sum(x * x, axis=-1, keepdims=True) + RMS_EPS)


def causal_depthwise_conv(x, w):
    s = x.shape[1]
    xp = jnp.pad(x, ((0, 0), (CONV_K - 1, 0), (0, 0)))
    return sum(xp[:, i:i + s] * w[i] for i in range(CONV_K))


def gated_delta_rule(q, k, v, beta, g):
    b_, s_, h_, dk = q.shape
    dv = v.shape[-1]
    n = s_ // CHUNK

    def chunks(t):
        return jnp.swapaxes(t.reshape(b_, n, CHUNK, h_, *t.shape[3:]), 2, 3)

    q, k, v, beta, g = chunks(q), chunks(k), chunks(v), chunks(beta), chunks(g)
    g = jnp.cumsum(g, axis=-1)
    tri = jnp.tril(jnp.ones((CHUNK, CHUNK), bool))
    strict = jnp.tril(jnp.ones((CHUNK, CHUNK), bool), -1)
    gdiff = g[..., :, None] - g[..., None, :]
    decay = jnp.where(tri, jnp.exp(jnp.where(tri, gdiff, 0.0)), 0.0)
    kb = k * beta[..., None]
    lmat = jnp.where(strict, jnp.einsum('bnhcd,bnhed->bnhce', kb, k) * decay, 0.0)
    eye = jnp.eye(CHUNK, dtype=jnp.float32)
    tinv = lax.linalg.triangular_solve(eye + lmat, jnp.broadcast_to(eye, lmat.shape),
                                       left_side=True, lower=True, unit_diagonal=True)
    w = jnp.einsum('bnhce,bnhed->bnhcd', tinv, kb * jnp.exp(g)[..., None])
    u = jnp.einsum('bnhce,bnhed->bnhcd', tinv, v * beta[..., None])
    qk = jnp.where(tri, jnp.einsum('bnhcd,bnhed->bnhce', q, k) * decay, 0.0)
    qg = q * jnp.exp(g)[..., None]
    g_last = g[..., -1]
    kdec = k * jnp.exp(g_last[..., None] - g)[..., None]

    def step(state, inp):
        w_i, u_i, qk_i, qg_i, kdec_i, gl_i = inp
        v_new = u_i - jnp.einsum('bhcd,bhde->bhce', w_i, state)
        o = jnp.einsum('bhcd,bhde->bhce', qg_i, state) + jnp.einsum('bhce,bhed->bhcd', qk_i, v_new)
        state = state * jnp.exp(gl_i)[..., None, None] + jnp.einsum('bhcd,bhce->bhde', kdec_i, v_new)
        return state, o

    xs = tuple(jnp.moveaxis(t, 1, 0) for t in (w, u, qk, qg, kdec, g_last))
    s0 = jnp.zeros((b_, h_, dk, dv), jnp.float32)
    _, o = lax.scan(step, s0, xs)
    return o.transpose(1, 0, 3, 2, 4).reshape(b_, s_, h_, dv)


def deltanet_branch(q, k, v, z, b, a, conv_w, a_log, dt_bias, norm_w):
    bsz, s_, _ = q.shape
    f32 = jnp.float32
    qkv = jax.nn.silu(causal_depthwise_conv(jnp.concatenate([q, k, v], axis=-1), conv_w))
    q, k, v = jnp.split(qkv.astype(f32), [DN_WIDTH, 2 * DN_WIDTH], axis=-1)
    q = l2norm(q.reshape(bsz, s_, DN_HEADS, DN_DK)) * (DN_DK ** -0.5)
    k = l2norm(k.reshape(bsz, s_, DN_HEADS, DN_DK))
    v = v.reshape(bsz, s_, DN_HEADS, DN_DV)
    beta = jax.nn.sigmoid(b.astype(f32))
    g = -jnp.exp(a_log.astype(f32)) * jax.nn.softplus(a.astype(f32) + dt_bias.astype(f32))
    o = gated_delta_rule(q, k, v, beta, g)
    o = o * lax.rsqrt(jnp.mean(o * o, axis=-1, keepdims=True) + RMS_EPS) * norm_w.astype(f32)
    o = o * jax.nn.silu(z.astype(f32).reshape(bsz, s_, DN_HEADS, DN_DV))
    return o.reshape(bsz, s_, DN_VWIDTH).astype(z.dtype)


def stick_breaking_attention(q, k, v):
    bsz, s_, _ = q.shape
    f32 = jnp.float32
    nb = s_ // SB_BLOCK
    qb = q.astype(f32).reshape(bsz, nb, SB_BLOCK, SB_HEADS, SB_DH).transpose(1, 0, 3, 2, 4) * (SB_DH ** -0.5)
    kh = k.astype(f32).reshape(bsz, s_, SB_HEADS, SB_DH).transpose(0, 2, 1, 3)
    vh = v.astype(f32).reshape(bsz, s_, SB_HEADS, SB_DH).transpose(0, 2, 1, 3)
    key_pos = jnp.arange(s_, dtype=jnp.int32)

    def block(args):
        qblk, q0 = args
        zs = jnp.einsum('bhqd,bhkd->bhqk', qblk, kh)
        strict = key_pos[None, :] < (q0 + jnp.arange(SB_BLOCK, dtype=jnp.int32))[:, None]
        log_keep = jnp.where(strict, jax.nn.log_sigmoid(-zs), 0.0)
        between = lax.cumsum(log_keep, axis=3, reverse=True) - log_keep
        att = jnp.where(strict, jnp.exp(jax.nn.log_sigmoid(zs) + between), 0.0)
        return jnp.einsum('bhqk,bhkd->bhqd', att, vh)

    starts = jnp.arange(nb, dtype=jnp.int32) * SB_BLOCK
    o = lax.map(block, (qb, starts))
    return o.transpose(1, 0, 3, 2, 4).reshape(bsz, s_, SB_WIDTH).astype(q.dtype)


def shared_grouped_router(x, w_router, router_bias):
    f32 = jnp.float32
    aff = jax.nn.sigmoid(jnp.einsum('bsd,de->bse', x, w_router).astype(f32))
    sel = aff + router_bias.astype(f32)
    bsz, s_, _ = aff.shape
    group_score = jnp.sum(lax.top_k(sel.reshape(bsz, s_, N_GROUPS, EXPERTS_PER_GROUP), TOP_K)[0], axis=-1)
    group = jnp.argmax(group_score, axis=-1)
    expert_group = jnp.arange(N_EXPERTS, dtype=jnp.int32) // EXPERTS_PER_GROUP
    masked = jnp.where(expert_group[None, None, :] == group[..., None], sel, -jnp.inf)
    _, idx = lax.top_k(masked, TOP_K)
    wsel = jnp.take_along_axis(aff, idx, axis=-1)
    wsel = wsel / jnp.sum(wsel, axis=-1, keepdims=True)
    return jnp.einsum('bsk,bske->bse', wsel, jax.nn.one_hot(idx, N_EXPERTS, dtype=f32))


def moe_ffn(x, combine, w_gate, w_up, w_down):
    h = jax.nn.silu(jnp.einsum('bsd,edf->bsef', x, w_gate)) * jnp.einsum('bsd,edf->bsef', x, w_up)
    h = h * combine[..., None].astype(h.dtype)
    return jnp.einsum('bsef,efd->bsd', h, w_down)


def setup_inputs(seed: int = 0) -> dict:
    key = jax.random.key(seed)
    ks = jax.random.split(key, 20)
    f32 = jnp.float32

    def nrm(k, shape, scale):
        return jax.random.normal(k, shape, f32) * scale

    conv_ch = 2 * DN_WIDTH + DN_VWIDTH
    dt = jnp.exp(jax.random.uniform(ks[4], (DEPTH, DN_HEADS), f32, math.log(1e-3), math.log(1e-1)))
    return {
        'x': nrm(ks[0], (BATCH, SEQ, D_MODEL), 1.0),
        'w_in': nrm(ks[1], (DEPTH, D_MODEL, D_IN), D_MODEL ** -0.5),
        'conv_w': nrm(ks[2], (DEPTH, CONV_K, conv_ch), CONV_K ** -0.5),
        'dn_a_log': jnp.log(jax.random.uniform(ks[3], (DEPTH, DN_HEADS), f32, 1.0, 16.0)),
        'dn_dt_bias': dt + jnp.log(-jnp.expm1(-dt)),
        'dn_norm_w': 1.0 + nrm(ks[5], (DEPTH, DN_DV), 0.02),
        'p_a': nrm(ks[6], (DEPTH, DN_VWIDTH, D_MODEL), DN_VWIDTH ** -0.5),
        'p_b': nrm(ks[7], (DEPTH, SB_WIDTH, D_MODEL), SB_WIDTH ** -0.5),
        'w_out': nrm(ks[8], (DEPTH, D_MODEL, D_MODEL), DEEPNORM_BETA * D_MODEL ** -0.5),
        'ln1_g': 1.0 + nrm(ks[9], (DEPTH, D_MODEL), 0.02),
        'ln1_b': nrm(ks[10], (DEPTH, D_MODEL), 0.02),
        'w_router': nrm(ks[11], (D_MODEL, N_EXPERTS), D_MODEL ** -0.5),
        'router_bias': nrm(ks[12], (N_EXPERTS,), 0.01),
        'w_gate': nrm(ks[13], (DEPTH, N_EXPERTS, D_MODEL, D_FF_EXPERT), D_MODEL ** -0.5),
        'w_up': nrm(ks[14], (DEPTH, N_EXPERTS, D_MODEL, D_FF_EXPERT), D_MODEL ** -0.5),
        'w_down': nrm(ks[15], (DEPTH, N_EXPERTS, D_FF_EXPERT, D_MODEL), DEEPNORM_BETA * D_FF_EXPERT ** -0.5),
        'ln2_g': 1.0 + nrm(ks[16], (DEPTH, D_MODEL), 0.02),
        'ln2_b': nrm(ks[17], (DEPTH, D_MODEL), 0.02),
    }


def reference(x, w_in, conv_w, dn_a_log, dn_dt_bias, dn_norm_w, p_a, p_b, w_out,
              ln1_g, ln1_b, w_router, router_bias, w_gate, w_up, w_down, ln2_g, ln2_b):
    split_points = np.cumsum(IN_WIDTHS)[:-1].tolist()
    for l in range(DEPTH):
        proj = jnp.einsum('bsd,de->bse', x, w_in[l])
        (dq, dk_, dv_, dz, db, da, sq, sk, sv, gate_a, gate_b) = jnp.split(proj, split_points, axis=-1)
        o_a = deltanet_branch(dq, dk_, dv_, dz, db, da, conv_w[l], dn_a_log[l], dn_dt_bias[l], dn_norm_w[l])
        o_b = stick_breaking_attention(sq, sk, sv)
        merged = (jax.nn.sigmoid(gate_a) * jnp.einsum('bsc,cd->bsd', o_a, p_a[l])
                  + jax.nn.sigmoid(gate_b) * jnp.einsum('bsc,cd->bsd', o_b, p_b[l]))
        mix = jnp.einsum('bsd,de->bse', merged, w_out[l])
        x = layer_norm(DEEPNORM_ALPHA * x + mix, ln1_g[l], ln1_b[l])
        combine = shared_grouped_router(x, w_router, router_bias)
        ffn = moe_ffn(x, combine, w_gate[l], w_up[l], w_down[l])
        x = layer_norm(DEEPNORM_ALPHA * x + ffn, ln2_g[l], ln2_b[l])
    return x
```

```python
import jax
import jax.numpy as jnp
from jax import lax
from jax.experimental import pallas as pl
from jax.experimental.pallas import tpu as pltpu

f32 = jnp.float32
bf16 = jnp.bfloat16
HIGHEST = lax.Precision.HIGHEST

D_MODEL = 1024
DEPTH = 2
CHUNK = 64
DN_HEADS = 4
DN_DK = 128
DN_DV = 128
CONV_K = 4
SB_HEADS = 8
SB_DH = 64
SB_BLOCK = 128
N_EXPERTS = 16
N_GROUPS = 4
EXPERTS_PER_GROUP = N_EXPERTS // N_GROUPS
D_FF_EXPERT = 512
LN_EPS = 1e-5
RMS_EPS = 1e-6
DEEPNORM_ALPHA = (2 * DEPTH) ** 0.25

DN_WIDTH = DN_HEADS * DN_DK
SB_WIDTH = SB_HEADS * SB_DH
CONV_CH = 3 * DN_WIDTH
LANES = 128
DN_SLAB = 4 * DN_WIDTH + LANES
BA_COL = 4 * DN_WIDTH
SB_SLAB = 3 * SB_WIDTH
GATE_SLAB = 2 * D_MODEL

SB_LOG_ZERO = -88.0
SB_STATIC_BLOCKS = 3
SB_MASK_PENALTY = -1e30

VMEM_LIMIT = 48 * 1024 * 1024


def _sigmoid(x):
    return 1.0 / (1.0 + jnp.exp(-x))


def _softplus(x):
    return jnp.maximum(x, 0.0) + jnp.log1p(jnp.exp(-jnp.abs(x)))


def _nt_dot(a, b):
    return lax.dot_general(a, b, (((1,), (1,)), ((), ())), preferred_element_type=f32)


def _tn_dot(a, b):
    return lax.dot_general(a, b, (((0,), (0,)), ((), ())), preferred_element_type=f32)


def _dot_exact(a, b):
    return jnp.dot(a, b, preferred_element_type=f32, precision=HIGHEST)


def _layer_norm(y, g, b):
    mu = jnp.mean(y, axis=-1, keepdims=True)
    d = y - mu
    var = jnp.mean(d * d, axis=-1, keepdims=True)
    return d * lax.rsqrt(var + LN_EPS) * g + b


def _proj_kernel(x_ref, wdn_ref, wsb_ref, wg_ref, odn_ref, osb_ref, og_ref):
    x = x_ref[...]
    odn_ref[...] = jnp.dot(x, wdn_ref[...], preferred_element_type=f32)
    osb_ref[...] = jnp.dot(x, wsb_ref[...], preferred_element_type=f32).astype(bf16)
    og_ref[...] = jnp.dot(x, wg_ref[...], preferred_element_type=f32)


def _proj(xb, w_dn, w_sb, w_g, tm=256):
    s = xb.shape[0]
    row = lambda i: (i, 0)
    fixed = lambda i: (0, 0)
    return pl.pallas_call(
        _proj_kernel,
        grid=(s // tm,),
        in_specs=[
            pl.BlockSpec((tm, D_MODEL), row),
            pl.BlockSpec((D_MODEL, DN_SLAB), fixed),
            pl.BlockSpec((D_MODEL, SB_SLAB), fixed),
            pl.BlockSpec((D_MODEL, GATE_SLAB), fixed),
        ],
        out_specs=[
            pl.BlockSpec((tm, DN_SLAB), row),
            pl.BlockSpec((tm, SB_SLAB), row),
            pl.BlockSpec((tm, GATE_SLAB), row),
        ],
        out_shape=[
            jax.ShapeDtypeStruct((s, DN_SLAB), f32),
            jax.ShapeDtypeStruct((s, SB_SLAB), bf16),
            jax.ShapeDtypeStruct((s, GATE_SLAB), f32),
        ],
        compiler_params=pltpu.CompilerParams(
            dimension_semantics=("arbitrary",), vmem_limit_bytes=VMEM_LIMIT),
        name="proj",
    )(xb, w_dn, w_sb, w_g)


DN_ROWS = 128
TAIL = 8


def _dn_kernel(p_ref, cw_ref, alog_ref, dtb_ref, nw_ref, o_ref, state_ref, tail_ref, xe_ref):
    step = pl.program_id(0)
    rows = p_ref.shape[0]

    @pl.when(step == 0)
    def _init():
        state_ref[...] = jnp.zeros_like(state_ref)
        tail_ref[...] = jnp.zeros_like(tail_ref)

    xe_ref[0:TAIL, :] = tail_ref[...]
    xe_ref[TAIL:TAIL + rows, :] = p_ref[:, 0:CONV_CH]
    tail_ref[...] = p_ref[rows - TAIL:rows, 0:CONV_CH]
    base = TAIL - (CONV_K - 1)
    y = xe_ref[base:base + rows, :] * cw_ref[0:1, :]
    for j in range(1, CONV_K):
        y = y + xe_ref[base + j:base + j + rows, :] * cw_ref[j:j + 1, :]
    y = y * _sigmoid(y)

    ba = p_ref[:, BA_COL:BA_COL + LANES]
    beta_all = _sigmoid(ba)
    g_all = -jnp.exp(alog_ref[...]) * _softplus(ba + dtb_ref[...])
    nw = nw_ref[...]

    r64 = lax.broadcasted_iota(jnp.int32, (CHUNK, CHUNK), 0)
    c64 = lax.broadcasted_iota(jnp.int32, (CHUNK, CHUNK), 1)
    tri = r64 >= c64
    strict = r64 > c64
    tril_f = jnp.where(tri, 1.0, 0.0).astype(f32)
    ones_f = jnp.ones((CHUNK, CHUNK), f32)

    for h in range(DN_HEADS):
        hs = slice(h * DN_DK, (h + 1) * DN_DK)
        q = y[:, h * DN_DK:(h + 1) * DN_DK]
        k = y[:, DN_WIDTH + h * DN_DK:DN_WIDTH + (h + 1) * DN_DK]
        v = y[:, 2 * DN_WIDTH + h * DN_DV:2 * DN_WIDTH + (h + 1) * DN_DV]
        q = q * lax.rsqrt(jnp.sum(q * q, axis=-1, keepdims=True) + RMS_EPS) * (DN_DK ** -0.5)
        k = k * lax.rsqrt(jnp.sum(k * k, axis=-1, keepdims=True) + RMS_EPS)
        beta = beta_all[:, h:h + 1]
        g = g_all[:, DN_HEADS + h:DN_HEADS + h + 1]
        state = state_ref[h]
        for c in range(rows // CHUNK):
            cs = slice(c * CHUNK, (c + 1) * CHUNK)
            qc, kc, vc, bc = q[cs], k[cs], v[cs], beta[cs]
            gb = jnp.broadcast_to(g[cs], (CHUNK, LANES))
            gcum = _dot_exact(tril_f, gb)
            gc_row = _dot_exact(ones_f, jnp.where(r64 <= c64, gb[:, :CHUNK], 0.0))
            gdiff = gcum[:, :CHUNK] - gc_row
            decay = jnp.where(tri, jnp.exp(jnp.where(tri, gdiff, 0.0)), 0.0)
            eg = jnp.exp(gcum)
            glast = gcum[CHUNK - 1:CHUNK, :]
            kbeta = kc * bc
            kb16 = kc.astype(bf16)
            lmat = jnp.where(strict, _nt_dot(kbeta.astype(bf16), kb16) * decay, 0.0)
            rhs = jnp.concatenate([kbeta * eg, vc * bc], axis=1)
            pows = [lmat]
            for _ in range(5):
                pows.append(_dot_exact(pows[-1], pows[-1]))
            for pw in pows[:0:-1]:
                rhs = rhs + _dot_exact(pw, rhs)
            rhs = rhs - _dot_exact(lmat, rhs)
            w = rhs[:, :DN_DK]
            u = rhs[:, DN_DK:]
            qk = jnp.where(tri, _nt_dot(qc.astype(bf16), kb16) * decay, 0.0)
            qg = qc * eg
            kdec = kc * jnp.exp(glast - gcum)
            ws = jnp.dot(jnp.concatenate([w, qg], axis=0).astype(bf16), state.astype(bf16),
                         preferred_element_type=f32)
            v_new = u - ws[:CHUNK]
            o = ws[CHUNK:] + jnp.dot(qk.astype(bf16), v_new.astype(bf16), preferred_element_type=f32)
            state = state * jnp.exp(glast) + _tn_dot(kdec.astype(bf16), v_new.astype(bf16))
            o = o * lax.rsqrt(jnp.mean(o * o, axis=-1, keepdims=True) + RMS_EPS) * nw
            z = p_ref[cs, 3 * DN_WIDTH + h * DN_DV:3 * DN_WIDTH + (h + 1) * DN_DV]
            o_ref[cs, hs] = o * (z * _sigmoid(z))
        state_ref[h] = state


def _deltanet(p_dn, conv_w8, alog_row, dtb_row, nw_row):
    s = p_dn.shape[0]
    fixed = lambda i: (0, 0)
    return pl.pallas_call(
        _dn_kernel,
        grid=(s // DN_ROWS,),
        in_specs=[
            pl.BlockSpec((DN_ROWS, DN_SLAB), lambda i: (i, 0)),
            pl.BlockSpec((8, CONV_CH), fixed),
            pl.BlockSpec((1, LANES), fixed),
            pl.BlockSpec((1, LANES), fixed),
            pl.BlockSpec((1, DN_DV), fixed),
        ],
        out_specs=pl.BlockSpec((DN_ROWS, DN_HEADS * DN_DV), lambda i: (i, 0)),
        out_shape=jax.ShapeDtypeStruct((s, DN_HEADS * DN_DV), f32),
        scratch_shapes=[
            pltpu.VMEM((DN_HEADS, DN_DK, DN_DV), f32),
            pltpu.VMEM((TAIL, CONV_CH), f32),
            pltpu.VMEM((TAIL + DN_ROWS, CONV_CH), f32),
        ],
        compiler_params=pltpu.CompilerParams(
            dimension_semantics=("arbitrary",), vmem_limit_bytes=VMEM_LIMIT),
        name="deltanet",
    )(p_dn, conv_w8, alog_row, dtb_row, nw_row)


N_PAIRS = SB_HEADS // 2


def _sb_kernel(q_ref, kd_ref, k1_ref, k2_ref, vd_ref, v1_ref, v2_ref, kv_hbm, o_ref,
               kbuf, vbuf, sem):
    qb = pl.program_id(0)
    blk = SB_BLOCK
    row = lax.broadcasted_iota(jnp.int32, (blk, blk), 0)
    lane = lax.broadcasted_iota(jnp.int32, (blk, blk), 1)
    diag_mask = row > lane
    even = lane < SB_DH
    suffix = jnp.where(row > lane, 1.0, 0.0).astype(bf16)
    suffix2 = jnp.concatenate([suffix, suffix], axis=0)

    def pair_block(q_pair, kblk, vblk, carries, penalty, is_diag):
        q_heads = (jnp.where(even, q_pair, jnp.zeros_like(q_pair)),
                   jnp.where(even, jnp.zeros_like(q_pair), q_pair))
        atts, new_carries = [], []
        for qh, carry in zip(q_heads, carries):
            z = _nt_dot(qh, kblk)
            sp = _softplus(z)
            lk = -sp
            if is_diag:
                lk = jnp.where(diag_mask, lk, 0.0)
            hi = lk.astype(bf16)
            lo = (lk - hi.astype(f32)).astype(bf16)
            within = jnp.dot(jnp.concatenate([hi, lo], axis=1), suffix2, preferred_element_type=f32)
            att = jnp.exp((z - sp) + within + (carry + penalty))
            if is_diag:
                att = jnp.where(diag_mask, att, 0.0)
            atts.append(att.astype(bf16))
            new_carries.append(carry + jnp.sum(lk, axis=1, keepdims=True))
        v2 = jnp.concatenate([jnp.where(even, vblk, jnp.zeros_like(vblk)),
                              jnp.where(even, jnp.zeros_like(vblk), vblk)], axis=0)
        out = jnp.dot(jnp.concatenate(atts, axis=1), v2, preferred_element_type=f32)
        return out, new_carries

    zero_col = jnp.zeros((blk, 1), f32)
    pen1 = jnp.where(qb >= 1, 0.0, SB_MASK_PENALTY).astype(f32)
    pen2 = jnp.where(qb >= 2, 0.0, SB_MASK_PENALTY).astype(f32)
    accs, carries = [], []
    for p in range(N_PAIRS):
        ps = slice(p * LANES, (p + 1) * LANES)
        q_pair = q_ref[:, ps]
        acc, cr = pair_block(q_pair, kd_ref[:, ps], vd_ref[:, ps], [zero_col, zero_col], 0.0, True)
        a1, cr = pair_block(q_pair, k1_ref[:, ps], v1_ref[:, ps], cr, pen1, False)
        a2, cr = pair_block(q_pair, k2_ref[:, ps], v2_ref[:, ps], cr, pen2, False)
        accs.append(acc + a1 + a2)
        carries.extend(cr)

    def live(cr):
        m = jnp.max(cr[0])
        for c in cr[1:]:
            m = jnp.maximum(m, jnp.max(c))
        return m >= SB_LOG_ZERO

    def cond(st):
        kb, more = st[0], st[1]
        return jnp.logical_and(kb >= 0, more)

    def body(st):
        kb = st[0]
        acc_l = list(st[2:2 + N_PAIRS])
        car_l = list(st[2 + N_PAIRS:])
        start = pl.multiple_of(kb * blk, blk)
        cpk = pltpu.make_async_copy(kv_hbm.at[pl.ds(start, blk), pl.ds(SB_WIDTH, SB_WIDTH)], kbuf, sem.at[0])
        cpv = pltpu.make_async_copy(kv_hbm.at[pl.ds(start, blk), pl.ds(2 * SB_WIDTH, SB_WIDTH)], vbuf, sem.at[1])
        cpk.start()
        cpv.start()
        cpk.wait()
        cpv.wait()
        new_car = []
        for p in range(N_PAIRS):
            ps = slice(p * LANES, (p + 1) * LANES)
            out, cr = pair_block(q_ref[:, ps], kbuf[:, ps], vbuf[:, ps],
                                 car_l[2 * p:2 * p + 2], 0.0, False)
            acc_l[p] = acc_l[p] + out
            new_car.extend(cr)
        return (kb - 1, live(new_car), *acc_l, *new_car)

    st = lax.while_loop(cond, body, (qb - SB_STATIC_BLOCKS, live(carries), *accs, *carries))
    for p in range(N_PAIRS):
        o_ref[:, p * LANES:(p + 1) * LANES] = st[2 + p]


def _sb_attention(p_sb):
    s = p_sb.shape[0]
    blk = SB_BLOCK

    def spec(col, back):
        return pl.BlockSpec((blk, SB_WIDTH), lambda i: (jnp.maximum(i - back, 0), col))

    return pl.pallas_call(
        _sb_kernel,
        grid=(s // blk,),
        in_specs=[spec(0, 0),
                  spec(1, 0), spec(1, 1), spec(1, 2),
                  spec(2, 0), spec(2, 1), spec(2, 2),
                  pl.BlockSpec(memory_space=pl.ANY)],
        out_specs=pl.BlockSpec((blk, SB_WIDTH), lambda i: (i, 0)),
        out_shape=jax.ShapeDtypeStruct((s, SB_WIDTH), f32),
        scratch_shapes=[
            pltpu.VMEM((blk, SB_WIDTH), bf16),
            pltpu.VMEM((blk, SB_WIDTH), bf16),
            pltpu.SemaphoreType.DMA((2,)),
        ],
        compiler_params=pltpu.CompilerParams(
            dimension_semantics=("arbitrary",), vmem_limit_bytes=VMEM_LIMIT),
        name="sb_attention",
    )(p_sb, p_sb, p_sb, p_sb, p_sb, p_sb, p_sb, p_sb)


def _merge_kernel(x_ref, oa_ref, ob_ref, g_ref, pa_ref, pb_ref, wo_ref, lg_ref, lb_ref, wr_ref, rb_ref,
                  x1_ref, x1b_ref, comb_ref):
    a = jnp.dot(oa_ref[...].astype(bf16), pa_ref[...], preferred_element_type=f32)
    b = jnp.dot(ob_ref[...].astype(bf16), pb_ref[...], preferred_element_type=f32)
    merged = _sigmoid(g_ref[:, :D_MODEL]) * a + _sigmoid(g_ref[:, D_MODEL:]) * b
    mix = jnp.dot(merged.astype(bf16), wo_ref[...], preferred_element_type=f32)
    x1 = _layer_norm(DEEPNORM_ALPHA * x_ref[...] + mix, lg_ref[...], lb_ref[...])
    x1_ref[...] = x1
    x1b_ref[...] = x1.astype(bf16)

    aff = _sigmoid(_dot_exact(x1, wr_ref[...]))
    tm = aff.shape[0]
    lane = lax.broadcasted_iota(jnp.int32, (tm, LANES), 1)
    grp = lax.shift_right_logical(lane, 2)
    neg = -jnp.inf
    sel = aff + rb_ref[...]
    best = None
    for gidx in range(N_GROUPS):
        sg = jnp.where(grp == gidx, sel, neg)
        m1 = jnp.max(sg, axis=1, keepdims=True)
        i1 = jnp.min(jnp.where(sg == m1, lane, LANES), axis=1, keepdims=True)
        sg2 = jnp.where(lane == i1, neg, sg)
        m2 = jnp.max(sg2, axis=1, keepdims=True)
        i2 = jnp.min(jnp.where(sg2 == m2, lane, LANES), axis=1, keepdims=True)
        score = m1 + m2
        if best is None:
            best = (score, i1, i2)
        else:
            better = score > best[0]
            best = (jnp.where(better, score, best[0]),
                    jnp.where(better, i1, best[1]),
                    jnp.where(better, i2, best[2]))
    _, i1, i2 = best
    hit1 = lane == i1
    hit2 = lane == i2
    w1 = jnp.sum(jnp.where(hit1, aff, 0.0), axis=1, keepdims=True)
    w2 = jnp.sum(jnp.where(hit2, aff, 0.0), axis=1, keepdims=True)
    denom = w1 + w2
    comb_ref[...] = jnp.where(hit1, w1 / denom, 0.0) + jnp.where(hit2, w2 / denom, 0.0)


def _merge(x, o_a, o_b, gates, p_a, p_b, w_out, ln_g, ln_b, w_router, r_bias, tm=256):
    s = x.shape[0]
    row = lambda i: (i, 0)
    fixed = lambda i: (0, 0)
    return pl.pallas_call(
        _merge_kernel,
        grid=(s // tm,),
        in_specs=[
            pl.BlockSpec((tm, D_MODEL), row),
            pl.BlockSpec((tm, DN_HEADS * DN_DV), row),
            pl.BlockSpec((tm, SB_WIDTH), row),
            pl.BlockSpec((tm, GATE_SLAB), row),
            pl.BlockSpec((DN_HEADS * DN_DV, D_MODEL), fixed),
            pl.BlockSpec((SB_WIDTH, D_MODEL), fixed),
            pl.BlockSpec((D_MODEL, D_MODEL), fixed),
            pl.BlockSpec((1, D_MODEL), fixed),
            pl.BlockSpec((1, D_MODEL), fixed),
            pl.BlockSpec((D_MODEL, LANES), fixed),
            pl.BlockSpec((1, LANES), fixed),
        ],
        out_specs=[
            pl.BlockSpec((tm, D_MODEL), row),
            pl.BlockSpec((tm, D_MODEL), row),
            pl.BlockSpec((tm, LANES), row),
        ],
        out_shape=[
            jax.ShapeDtypeStruct((s, D_MODEL), f32),
            jax.ShapeDtypeStruct((s, D_MODEL), bf16),
            jax.ShapeDtypeStruct((s, LANES), f32),
        ],
        compiler_params=pltpu.CompilerParams(
            dimension_semantics=("arbitrary",), vmem_limit_bytes=VMEM_LIMIT),
        name="merge_router",
    )(x, o_a, o_b, gates, p_a, p_b, w_out, ln_g, ln_b, w_router, r_bias)


def _moe_kernel(x1_ref, x1b_ref, comb_ref, wg_ref, wu_ref, wd_ref, lg_ref, lb_ref,
                out_ref, outb_ref, acc_ref):
    e = pl.program_id(1)

    @pl.when(e == 0)
    def _zero():
        acc_ref[...] = jnp.zeros_like(acc_ref)

    xb = x1b_ref[...]
    gate = jnp.dot(xb, wg_ref[0], preferred_element_type=f32)
    up = jnp.dot(xb, wu_ref[0], preferred_element_type=f32)
    comb = comb_ref[...]
    lane = lax.broadcasted_iota(jnp.int32, comb.shape, 1)
    ce = jnp.sum(jnp.where(lane == e, comb, 0.0), axis=1, keepdims=True)
    hid = (gate * _sigmoid(gate)) * up * ce
    acc_ref[...] += jnp.dot(hid.astype(bf16), wd_ref[0], preferred_element_type=f32)

    @pl.when(e == N_EXPERTS - 1)
    def _finish():
        y = _layer_norm(DEEPNORM_ALPHA * x1_ref[...] + acc_ref[...], lg_ref[...], lb_ref[...])
        out_ref[...] = y
        outb_ref[...] = y.astype(bf16)


def _moe(x1, x1b, comb, w_gate, w_up, w_down, ln_g, ln_b, tm=512):
    s = x1.shape[0]
    row = lambda i, e: (i, 0)
    fixed = lambda i, e: (0, 0)
    expert = lambda i, e: (e, 0, 0)
    return pl.pallas_call(
        _moe_kernel,
        grid=(s // tm, N_EXPERTS),
        in_specs=[
            pl.BlockSpec((tm, D_MODEL), row),
            pl.BlockSpec((tm, D_MODEL), row),
            pl.BlockSpec((tm, LANES), row),
            pl.BlockSpec((1, D_MODEL, D_FF_EXPERT), expert),
            pl.BlockSpec((1, D_MODEL, D_FF_EXPERT), expert),
            pl.BlockSpec((1, D_FF_EXPERT, D_MODEL), expert),
            pl.BlockSpec((1, D_MODEL), fixed),
            pl.BlockSpec((1, D_MODEL), fixed),
        ],
        out_specs=[
            pl.BlockSpec((tm, D_MODEL), row),
            pl.BlockSpec((tm, D_MODEL), row),
        ],
        out_shape=[
            jax.ShapeDtypeStruct((s, D_MODEL), f32),
            jax.ShapeDtypeStruct((s, D_MODEL), bf16),
        ],
        scratch_shapes=[pltpu.VMEM((tm, D_MODEL), f32)],
        compiler_params=pltpu.CompilerParams(
            dimension_semantics=("arbitrary", "arbitrary"), vmem_limit_bytes=VMEM_LIMIT),
        name="moe_ffn",
    )(x1, x1b, comb, w_gate, w_up, w_down, ln_g, ln_b)


def _pad_lanes(a, width=LANES):
    return jnp.pad(a, ((0, 0), (0, width - a.shape[1])))


def kernel(x, w_in, conv_w, dn_a_log, dn_dt_bias, dn_norm_w, p_a, p_b, w_out, ln1_g, ln1_b, w_router, router_bias, w_gate, w_up, w_down, ln2_g, ln2_b):
    bsz, s, _ = x.shape
    assert bsz == 1 and s % 512 == 0
    xf = x[0]
    xb = xf.astype(bf16)

    c_ba = 4 * DN_WIDTH
    c_sb = c_ba + 2 * DN_HEADS
    c_gate = c_sb + SB_SLAB
    wr_pad = _pad_lanes(w_router)
    rb_pad = jnp.pad(router_bias[None, :], ((0, 0), (0, LANES - N_EXPERTS)), constant_values=-jnp.inf)
    head_pad = ((0, 0), (DN_HEADS, LANES - 2 * DN_HEADS))

    for l in range(DEPTH):
        w = w_in[l]
        w_dn = jnp.concatenate([w[:, :c_ba], _pad_lanes(w[:, c_ba:c_sb])], axis=1).astype(bf16)
        sb_scale = jnp.concatenate([jnp.full((SB_WIDTH,), SB_DH ** -0.5, f32), jnp.ones((2 * SB_WIDTH,), f32)])
        w_sb = (w[:, c_sb:c_gate] * sb_scale).astype(bf16)
        w_g = w[:, c_gate:].astype(bf16)

        p_dn, p_sb, gates = _proj(xb, w_dn, w_sb, w_g)
        o_a = _deltanet(p_dn,
                        jnp.pad(conv_w[l], ((0, 8 - CONV_K), (0, 0))),
                        jnp.pad(dn_a_log[l][None, :], head_pad),
                        jnp.pad(dn_dt_bias[l][None, :], head_pad),
                        dn_norm_w[l][None, :])
        o_b = _sb_attention(p_sb)
        x1, x1b, comb = _merge(xf, o_a, o_b, gates,
                               p_a[l].astype(bf16), p_b[l].astype(bf16), w_out[l].astype(bf16),
                               ln1_g[l][None, :], ln1_b[l][None, :], wr_pad, rb_pad)
        xf, xb = _moe(x1, x1b, comb,
                      w_gate[l].astype(bf16), w_up[l].astype(bf16), w_down[l].astype(bf16),
                      ln2_g[l][None, :], ln2_b[l][None, :])
    return xf[None]
```

```python
import jax
import jax.numpy as jnp
from jax import lax
from jax.experimental import pallas as pl
from jax.experimental.pallas import tpu as pltpu

f32 = jnp.float32
bf16 = jnp.bfloat16
HIGHEST = lax.Precision.HIGHEST

D_MODEL = 1024
DEPTH = 2
CHUNK = 64
DN_HEADS = 4
DN_DK = 128
DN_DV = 128
CONV_K = 4
SB_HEADS = 8
SB_DH = 64
SB_BLOCK = 128
N_EXPERTS = 16
N_GROUPS = 4
EXPERTS_PER_GROUP = N_EXPERTS // N_GROUPS
D_FF_EXPERT = 512
LN_EPS = 1e-5
RMS_EPS = 1e-6
DEEPNORM_ALPHA = (2 * DEPTH) ** 0.25

DN_WIDTH = DN_HEADS * DN_DK
SB_WIDTH = SB_HEADS * SB_DH
CONV_CH = 3 * DN_WIDTH
LANES = 128
DN_SLAB = 4 * DN_WIDTH + LANES
BA_COL = 4 * DN_WIDTH
SB_SLAB = 3 * SB_WIDTH
GATE_SLAB = 2 * D_MODEL

SB_LOG_ZERO = -88.0
SB_STATIC_BLOCKS = 3
SB_MASK_PENALTY = -1e30

VMEM_LIMIT = 48 * 1024 * 1024


def _sigmoid(x):
    return 1.0 / (1.0 + jnp.exp(-x))


def _softplus(x):
    return jnp.maximum(x, 0.0) + jnp.log1p(jnp.exp(-jnp.abs(x)))


def _nt_dot(a, b):
    return lax.dot_general(a, b, (((1,), (1,)), ((), ())), preferred_element_type=f32)


def _tn_dot(a, b):
    return lax.dot_general(a, b, (((0,), (0,)), ((), ())), preferred_element_type=f32)


def _dot_exact(a, b):
    return jnp.dot(a, b, preferred_element_type=f32, precision=HIGHEST)


def _layer_norm(y, g, b):
    mu = jnp.mean(y, axis=-1, keepdims=True)
    d = y - mu
    var = jnp.mean(d * d, axis=-1, keepdims=True)
    return d * lax.rsqrt(var + LN_EPS) * g + b


def _proj_kernel(x_ref, wdn_ref, wsb_ref, wg_ref, odn_ref, osb_ref, og_ref):
    x = x_ref[...]
    odn_ref[...] = jnp.dot(x, wdn_ref[...], preferred_element_type=f32)
    osb_ref[...] = jnp.dot(x, wsb_ref[...], preferred_element_type=f32).astype(bf16)
    og_ref[...] = jnp.dot(x, wg_ref[...], preferred_element_type=f32)


def _proj(xb, w_dn, w_sb, w_g, tm=256):
    s = xb.shape[0]
    row = lambda i: (i, 0)
    fixed = lambda i: (0, 0)
    return pl.pallas_call(
        _proj_kernel,
        grid=(s // tm,),
        in_specs=[
            pl.BlockSpec((tm, D_MODEL), row),
            pl.BlockSpec((D_MODEL, DN_SLAB), fixed),
            pl.BlockSpec((D_MODEL, SB_SLAB), fixed),
            pl.BlockSpec((D_MODEL, GATE_SLAB), fixed),
        ],
        out_specs=[
            pl.BlockSpec((tm, DN_SLAB), row),
            pl.BlockSpec((tm, SB_SLAB), row),
            pl.BlockSpec((tm, GATE_SLAB), row),
        ],
        out_shape=[
            jax.ShapeDtypeStruct((s, DN_SLAB), f32),
            jax.ShapeDtypeStruct((s, SB_SLAB), bf16),
            jax.ShapeDtypeStruct((s, GATE_SLAB), f32),
        ],
        compiler_params=pltpu.CompilerParams(
            dimension_semantics=("arbitrary",), vmem_limit_bytes=VMEM_LIMIT),
        name="proj",
    )(xb, w_dn, w_sb, w_g)


DN_ROWS = 256
PAIR = 2 * CHUNK
TAIL = 8


def _split2(x):
    hi = x.astype(bf16)
    return hi, (x - hi.astype(f32)).astype(bf16)


def _split3(x):
    hi = x.astype(bf16)
    r = x - hi.astype(f32)
    mid = r.astype(bf16)
    return hi, mid, (r - mid.astype(f32)).astype(bf16)


def _dn_kernel(p_ref, cw_ref, alog_ref, dtb_ref, nw_ref, o_ref, state_ref, tail_ref, xe_ref):
    step = pl.program_id(0)
    rows = p_ref.shape[0]
    n_pb = rows // PAIR

    @pl.when(step == 0)
    def _init():
        state_ref[...] = jnp.zeros_like(state_ref)
        tail_ref[...] = jnp.zeros_like(tail_ref)

    xe_ref[0:TAIL, :] = tail_ref[...]
    xe_ref[TAIL:TAIL + rows, :] = p_ref[:, 0:CONV_CH]
    tail_ref[...] = p_ref[rows - TAIL:rows, 0:CONV_CH]

    row128 = lax.broadcasted_iota(jnp.int32, (PAIR, LANES), 0)
    lane128 = lax.broadcasted_iota(jnp.int32, (PAIR, LANES), 1)
    same_chunk = (row128 >= CHUNK) == (lane128 >= CHUNK)
    tril_bd = jnp.where(jnp.logical_and(row128 >= lane128, same_chunk), 1.0, 0.0).astype(bf16)
    triu_bd = jnp.where(jnp.logical_and(row128 <= lane128, same_chunk), 1.0, 0.0).astype(bf16)
    first_rows = row128 < CHUNK
    row64 = lax.broadcasted_iota(jnp.int32, (CHUNK, LANES), 0)
    lane64 = lax.broadcasted_iota(jnp.int32, (CHUNK, LANES), 1)
    left = lane64 < CHUNK
    col_in_chunk = jnp.bitwise_and(lane64, CHUNK - 1)
    tri_p = row64 >= col_in_chunk
    strict_p = row64 > col_in_chunk
    eye_p = jnp.where(row64 == col_in_chunk, 1.0, 0.0).astype(f32)

    def block_diag(z):
        zero = jnp.zeros_like(z)
        return jnp.concatenate([jnp.where(left, z, zero), jnp.where(left, zero, z)], axis=0)

    def pair_matmul(y_hi, y_lo, zbd_hi, zbd_lo):
        return (jnp.dot(jnp.concatenate([y_hi, y_lo], axis=1), jnp.concatenate([zbd_hi, zbd_hi], axis=0),
                        preferred_element_type=f32)
                + jnp.dot(y_hi, zbd_lo, preferred_element_type=f32))

    gcol_all, grow_all, beta_all = [], [], []
    for pb in range(n_pb):
        ba = p_ref[pb * PAIR:(pb + 1) * PAIR, BA_COL:BA_COL + LANES]
        beta_all.append(_sigmoid(ba))
        g = -jnp.exp(alog_ref[...]) * _softplus(ba + dtb_ref[...])
        parts = _split3(g)
        gcol_all.append(sum(jnp.dot(tril_bd, pt, preferred_element_type=f32) for pt in parts))
        grow_all.append(sum(_tn_dot(pt, triu_bd) for pt in parts))

    units = [(h, pb) for pb in range(n_pb) for h in range(DN_HEADS)]

    base = TAIL - (CONV_K - 1)
    qs, ks, vs = [], [], []
    for h, pb in units:
        outs = []
        for grp in range(3):
            col = grp * DN_WIDTH + h * DN_DK
            r0 = base + pb * PAIR
            acc = xe_ref[r0:r0 + PAIR, col:col + DN_DK] * cw_ref[0:1, col:col + DN_DK]
            for j in range(1, CONV_K):
                acc = acc + xe_ref[r0 + j:r0 + j + PAIR, col:col + DN_DK] * cw_ref[j:j + 1, col:col + DN_DK]
            outs.append(acc * _sigmoid(acc))
        q, k, v = outs
        qs.append(q * lax.rsqrt(jnp.sum(q * q, axis=-1, keepdims=True) + RMS_EPS) * (DN_DK ** -0.5))
        ks.append(k * lax.rsqrt(jnp.sum(k * k, axis=-1, keepdims=True) + RMS_EPS))
        vs.append(v)

    kbetas, qgs, kdecs, rstacks, decays, egl = [], [], [], [], [], []
    for (h, pb), q, k, v in zip(units, qs, ks, vs):
        gc = jnp.broadcast_to(gcol_all[pb][:, DN_HEADS + h:DN_HEADS + h + 1], (PAIR, LANES))
        beta = jnp.broadcast_to(beta_all[pb][:, h:h + 1], (PAIR, LANES))
        eg = jnp.exp(gc)
        glast = jnp.where(first_rows, gc[CHUNK - 1:CHUNK, :], gc[PAIR - 1:PAIR, :])
        kbeta = k * beta
        kbetas.append(kbeta)
        qgs.append(q * eg)
        kdecs.append((k * jnp.exp(glast - gc)).astype(bf16))
        rstacks.append(jnp.concatenate([kbeta * eg, v * beta], axis=1).astype(bf16))
        gdiff = jnp.where(left, gc[:CHUNK], gc[CHUNK:]) - grow_all[pb][DN_HEADS + h:DN_HEADS + h + 1, :]
        decays.append(jnp.where(tri_p, jnp.exp(jnp.where(tri_p, gdiff, 0.0)), 0.0))
        egl.append((jnp.exp(gc[CHUNK - 1:CHUNK, :]), jnp.exp(gc[PAIR - 1:PAIR, :])))

    lps, qkms = [], []
    for kbeta, q, k, decay in zip(kbetas, qs, ks, decays):
        kk = _nt_dot(jnp.concatenate([kbeta, q], axis=0).astype(bf16), k.astype(bf16))
        lps.append(jnp.where(strict_p, jnp.where(left, kk[0:CHUNK], kk[CHUNK:PAIR]) * decay, 0.0))
        qkms.append(jnp.where(tri_p, jnp.where(left, kk[PAIR:PAIR + CHUNK], kk[PAIR + CHUNK:]) * decay, 0.0)
                    .astype(bf16))

    pw = [_split2(lp) for lp in lps]
    pw_bd = [(block_diag(hi), block_diag(lo)) for hi, lo in pw]
    ts = [eye_p - lp for lp in lps]
    for _ in range(5):
        sq = [pair_matmul(hi, lo, bhi, blo) for (hi, lo), (bhi, blo) in zip(pw, pw_bd)]
        pw = [_split2(x) for x in sq]
        pw_bd = [(block_diag(hi), block_diag(lo)) for hi, lo in pw]
        t_split = [_split2(t) for t in ts]
        ts = [t + pair_matmul(thi, tlo, bhi, blo) for t, (thi, tlo), (bhi, blo) in zip(ts, t_split, pw_bd)]

    zero_p = jnp.zeros((CHUNK, LANES), bf16)
    mq, bo = [], []
    for t, rstack, kdec, qkm, qg in zip(ts, rstacks, kdecs, qkms, qgs):
        t16 = t.astype(bf16)
        halves = (jnp.where(left, t16, zero_p), jnp.where(left, zero_p, t16))
        qk_halves = (jnp.where(left, qkm, zero_p), jnp.where(left, zero_p, qkm))
        wus = [jnp.dot(th, rstack, preferred_element_type=f32).astype(bf16) for th in halves]
        wu_stack = jnp.concatenate(wus, axis=0)
        mq_u, bo_u = [], []
        for c in range(2):
            cs = slice(c * CHUNK, (c + 1) * CHUNK)
            kw = _tn_dot(kdec[cs], wus[c])
            qw = jnp.dot(qk_halves[c], wu_stack, preferred_element_type=f32)
            mq_u.append(jnp.concatenate([kw[:, :DN_DK], qg[cs] - qw[:, :DN_DK]], axis=0).astype(bf16))
            bo_u.append((kw[:, DN_DK:], qw[:, DN_DK:]))
        mq.append(mq_u)
        bo.append(bo_u)

    nw = nw_ref[...]
    states = [state_ref[h] for h in range(DN_HEADS)]
    for pb in range(n_pb):
        for c in range(2):
            for h in range(DN_HEADS):
                ui = pb * DN_HEADS + h
                ms = jnp.dot(mq[ui][c], states[h].astype(bf16), preferred_element_type=f32)
                b_c, o_c = bo[ui][c]
                o = ms[DN_DK:] + o_c
                states[h] = states[h] * egl[ui][c] - ms[:DN_DK] + b_c
                o = o * lax.rsqrt(jnp.mean(o * o, axis=-1, keepdims=True) + RMS_EPS) * nw
                r0 = pb * PAIR + c * CHUNK
                z = p_ref[r0:r0 + CHUNK, 3 * DN_WIDTH + h * DN_DV:3 * DN_WIDTH + (h + 1) * DN_DV]
                o_ref[r0:r0 + CHUNK, h * DN_DV:(h + 1) * DN_DV] = o * (z * _sigmoid(z))
    for h in range(DN_HEADS):
        state_ref[h] = states[h]


def _deltanet(p_dn, conv_w8, alog_row, dtb_row, nw_row):
    s = p_dn.shape[0]
    fixed = lambda i: (0, 0)
    return pl.pallas_call(
        _dn_kernel,
        grid=(s // DN_ROWS,),
        in_specs=[
            pl.BlockSpec((DN_ROWS, DN_SLAB), lambda i: (i, 0)),
            pl.BlockSpec((8, CONV_CH), fixed),
            pl.BlockSpec((1, LANES), fixed),
            pl.BlockSpec((1, LANES), fixed),
            pl.BlockSpec((1, DN_DV), fixed),
        ],
        out_specs=pl.BlockSpec((DN_ROWS, DN_HEADS * DN_DV), lambda i: (i, 0)),
        out_shape=jax.ShapeDtypeStruct((s, DN_HEADS * DN_DV), f32),
        scratch_shapes=[
            pltpu.VMEM((DN_HEADS, DN_DK, DN_DV), f32),
            pltpu.VMEM((TAIL, CONV_CH), f32),
            pltpu.VMEM((TAIL + DN_ROWS, CONV_CH), f32),
        ],
        compiler_params=pltpu.CompilerParams(
            dimension_semantics=("arbitrary",), vmem_limit_bytes=VMEM_LIMIT),
        name="deltanet",
    )(p_dn, conv_w8, alog_row, dtb_row, nw_row)


N_PAIRS = SB_HEADS // 2


def _sb_kernel(q_ref, kd_ref, k1_ref, k2_ref, vd_ref, v1_ref, v2_ref, kv_hbm, o_ref,
               kbuf, vbuf, sem):
    qb = pl.program_id(0)
    blk = SB_BLOCK
    row = lax.broadcasted_iota(jnp.int32, (blk, blk), 0)
    lane = lax.broadcasted_iota(jnp.int32, (blk, blk), 1)
    diag_mask = row > lane
    even = lane < SB_DH
    suffix = jnp.where(row > lane, 1.0, 0.0).astype(bf16)
    suffix2 = jnp.concatenate([suffix, suffix], axis=0)

    def pair_block(q_pair, kblk, vblk, carries, penalty, is_diag):
        q_heads = (jnp.where(even, q_pair, jnp.zeros_like(q_pair)),
                   jnp.where(even, jnp.zeros_like(q_pair), q_pair))
        atts, new_carries = [], []
        for qh, carry in zip(q_heads, carries):
            z = _nt_dot(qh, kblk)
            sp = _softplus(z)
            lk = -sp
            if is_diag:
                lk = jnp.where(diag_mask, lk, 0.0)
            hi = lk.astype(bf16)
            lo = (lk - hi.astype(f32)).astype(bf16)
            within = jnp.dot(jnp.concatenate([hi, lo], axis=1), suffix2, preferred_element_type=f32)
            att = jnp.exp((z - sp) + within + (carry + penalty))
            if is_diag:
                att = jnp.where(diag_mask, att, 0.0)
            atts.append(att.astype(bf16))
            new_carries.append(carry + jnp.sum(lk, axis=1, keepdims=True))
        v2 = jnp.concatenate([jnp.where(even, vblk, jnp.zeros_like(vblk)),
                              jnp.where(even, jnp.zeros_like(vblk), vblk)], axis=0)
        out = jnp.dot(jnp.concatenate(atts, axis=1), v2, preferred_element_type=f32)
        return out, new_carries

    zero_col = jnp.zeros((blk, 1), f32)
    pen1 = jnp.where(qb >= 1, 0.0, SB_MASK_PENALTY).astype(f32)
    pen2 = jnp.where(qb >= 2, 0.0, SB_MASK_PENALTY).astype(f32)
    accs, carries = [], []
    for p in range(N_PAIRS):
        ps = slice(p * LANES, (p + 1) * LANES)
        q_pair = q_ref[:, ps]
        acc, cr = pair_block(q_pair, kd_ref[:, ps], vd_ref[:, ps], [zero_col, zero_col], 0.0, True)
        a1, cr = pair_block(q_pair, k1_ref[:, ps], v1_ref[:, ps], cr, pen1, False)
        a2, cr = pair_block(q_pair, k2_ref[:, ps], v2_ref[:, ps], cr, pen2, False)
        accs.append(acc + a1 + a2)
        carries.extend(cr)

    def live(cr):
        m = jnp.max(cr[0])
        for c in cr[1:]:
            m = jnp.maximum(m, jnp.max(c))
        return m >= SB_LOG_ZERO

    def cond(st):
        kb, more = st[0], st[1]
        return jnp.logical_and(kb >= 0, more)

    def body(st):
        kb = st[0]
        acc_l = list(st[2:2 + N_PAIRS])
        car_l = list(st[2 + N_PAIRS:])
        start = pl.multiple_of(kb * blk, blk)
        cpk = pltpu.make_async_copy(kv_hbm.at[pl.ds(start, blk), pl.ds(SB_WIDTH, SB_WIDTH)], kbuf, sem.at[0])
        cpv = pltpu.make_async_copy(kv_hbm.at[pl.ds(start, blk), pl.ds(2 * SB_WIDTH, SB_WIDTH)], vbuf, sem.at[1])
        cpk.start()
        cpv.start()
        cpk.wait()
        cpv.wait()
        new_car = []
        for p in range(N_PAIRS):
            ps = slice(p * LANES, (p + 1) * LANES)
            out, cr = pair_block(q_ref[:, ps], kbuf[:, ps], vbuf[:, ps],
                                 car_l[2 * p:2 * p + 2], 0.0, False)
            acc_l[p] = acc_l[p] + out
            new_car.extend(cr)
        return (kb - 1, live(new_car), *acc_l, *new_car)

    st = lax.while_loop(cond, body, (qb - SB_STATIC_BLOCKS, live(carries), *accs, *carries))
    for p in range(N_PAIRS):
        o_ref[:, p * LANES:(p + 1) * LANES] = st[2 + p]


def _sb_attention(p_sb):
    s = p_sb.shape[0]
    blk = SB_BLOCK

    def spec(col, back):
        return pl.BlockSpec((blk, SB_WIDTH), lambda i: (jnp.maximum(i - back, 0), col))

    return pl.pallas_call(
        _sb_kernel,
        grid=(s // blk,),
        in_specs=[spec(0, 0),
                  spec(1, 0), spec(1, 1), spec(1, 2),
                  spec(2, 0), spec(2, 1), spec(2, 2),
                  pl.BlockSpec(memory_space=pl.ANY)],
        out_specs=pl.BlockSpec((blk, SB_WIDTH), lambda i: (i, 0)),
        out_shape=jax.ShapeDtypeStruct((s, SB_WIDTH), f32),
        scratch_shapes=[
            pltpu.VMEM((blk, SB_WIDTH), bf16),
            pltpu.VMEM((blk, SB_WIDTH), bf16),
            pltpu.SemaphoreType.DMA((2,)),
        ],
        compiler_params=pltpu.CompilerParams(
            dimension_semantics=("arbitrary",), vmem_limit_bytes=VMEM_LIMIT),
        name="sb_attention",
    )(p_sb, p_sb, p_sb, p_sb, p_sb, p_sb, p_sb, p_sb)


def _merge_kernel(x_ref, oa_ref, ob_ref, g_ref, pa_ref, pb_ref, wo_ref, lg_ref, lb_ref, wr_ref, rb_ref,
                  x1_ref, x1b_ref, comb_ref):
    a = jnp.dot(oa_ref[...].astype(bf16), pa_ref[...], preferred_element_type=f32)
    b = jnp.dot(ob_ref[...].astype(bf16), pb_ref[...], preferred_element_type=f32)
    merged = _sigmoid(g_ref[:, :D_MODEL]) * a + _sigmoid(g_ref[:, D_MODEL:]) * b
    mix = jnp.dot(merged.astype(bf16), wo_ref[...], preferred_element_type=f32)
    x1 = _layer_norm(DEEPNORM_ALPHA * x_ref[...] + mix, lg_ref[...], lb_ref[...])
    x1_ref[...] = x1
    x1b_ref[...] = x1.astype(bf16)

    aff = _sigmoid(_dot_exact(x1, wr_ref[...]))
    tm = aff.shape[0]
    lane = lax.broadcasted_iota(jnp.int32, (tm, LANES), 1)
    grp = lax.shift_right_logical(lane, 2)
    neg = -jnp.inf
    sel = aff + rb_ref[...]
    best = None
    for gidx in range(N_GROUPS):
        sg = jnp.where(grp == gidx, sel, neg)
        m1 = jnp.max(sg, axis=1, keepdims=True)
        i1 = jnp.min(jnp.where(sg == m1, lane, LANES), axis=1, keepdims=True)
        sg2 = jnp.where(lane == i1, neg, sg)
        m2 = jnp.max(sg2, axis=1, keepdims=True)
        i2 = jnp.min(jnp.where(sg2 == m2, lane, LANES), axis=1, keepdims=True)
        score = m1 + m2
        if best is None:
            best = (score, i1, i2)
        else:
            better = score > best[0]
            best = (jnp.where(better, score, best[0]),
                    jnp.where(better, i1, best[1]),
                    jnp.where(better, i2, best[2]))
    _, i1, i2 = best
    hit1 = lane == i1
    hit2 = lane == i2
    w1 = jnp.sum(jnp.where(hit1, aff, 0.0), axis=1, keepdims=True)
    w2 = jnp.sum(jnp.where(hit2, aff, 0.0), axis=1, keepdims=True)
    denom = w1 + w2
    comb_ref[...] = jnp.where(hit1, w1 / denom, 0.0) + jnp.where(hit2, w2 / denom, 0.0)


def _merge(x, o_a, o_b, gates, p_a, p_b, w_out, ln_g, ln_b, w_router, r_bias, tm=256):
    s = x.shape[0]
    row = lambda i: (i, 0)
    fixed = lambda i: (0, 0)
    return pl.pallas_call(
        _merge_kernel,
        grid=(s // tm,),
        in_specs=[
            pl.BlockSpec((tm, D_MODEL), row),
            pl.BlockSpec((tm, DN_HEADS * DN_DV), row),
            pl.BlockSpec((tm, SB_WIDTH), row),
            pl.BlockSpec((tm, GATE_SLAB), row),
            pl.BlockSpec((DN_HEADS * DN_DV, D_MODEL), fixed),
            pl.BlockSpec((SB_WIDTH, D_MODEL), fixed),
            pl.BlockSpec((D_MODEL, D_MODEL), fixed),
            pl.BlockSpec((1, D_MODEL), fixed),
            pl.BlockSpec((1, D_MODEL), fixed),
            pl.BlockSpec((D_MODEL, LANES), fixed),
            pl.BlockSpec((1, LANES), fixed),
        ],
        out_specs=[
            pl.BlockSpec((tm, D_MODEL), row),
            pl.BlockSpec((tm, D_MODEL), row),
            pl.BlockSpec((tm, LANES), row),
        ],
        out_shape=[
            jax.ShapeDtypeStruct((s, D_MODEL), f32),
            jax.ShapeDtypeStruct((s, D_MODEL), bf16),
            jax.ShapeDtypeStruct((s, LANES), f32),
        ],
        compiler_params=pltpu.CompilerParams(
            dimension_semantics=("arbitrary",), vmem_limit_bytes=VMEM_LIMIT),
        name="merge_router",
    )(x, o_a, o_b, gates, p_a, p_b, w_out, ln_g, ln_b, w_router, r_bias)


def _moe_kernel(x1_ref, x1b_ref, comb_ref, wg_ref, wu_ref, wd_ref, lg_ref, lb_ref,
                out_ref, outb_ref, acc_ref):
    e = pl.program_id(1)

    @pl.when(e == 0)
    def _zero():
        acc_ref[...] = jnp.zeros_like(acc_ref)

    xb = x1b_ref[...]
    gate = jnp.dot(xb, wg_ref[0], preferred_element_type=f32)
    up = jnp.dot(xb, wu_ref[0], preferred_element_type=f32)
    comb = comb_ref[...]
    lane = lax.broadcasted_iota(jnp.int32, comb.shape, 1)
    ce = jnp.sum(jnp.where(lane == e, comb, 0.0), axis=1, keepdims=True)
    hid = (gate * _sigmoid(gate)) * up * ce
    acc_ref[...] += jnp.dot(hid.astype(bf16), wd_ref[0], preferred_element_type=f32)

    @pl.when(e == N_EXPERTS - 1)
    def _finish():
        y = _layer_norm(DEEPNORM_ALPHA * x1_ref[...] + acc_ref[...], lg_ref[...], lb_ref[...])
        out_ref[...] = y
        outb_ref[...] = y.astype(bf16)


def _moe(x1, x1b, comb, w_gate, w_up, w_down, ln_g, ln_b, tm=512):
    s = x1.shape[0]
    row = lambda i, e: (i, 0)
    fixed = lambda i, e: (0, 0)
    expert = lambda i, e: (e, 0, 0)
    return pl.pallas_call(
        _moe_kernel,
        grid=(s // tm, N_EXPERTS),
        in_specs=[
            pl.BlockSpec((tm, D_MODEL), row),
            pl.BlockSpec((tm, D_MODEL), row),
            pl.BlockSpec((tm, LANES), row),
            pl.BlockSpec((1, D_MODEL, D_FF_EXPERT), expert),
            pl.BlockSpec((1, D_MODEL, D_FF_EXPERT), expert),
            pl.BlockSpec((1, D_FF_EXPERT, D_MODEL), expert),
            pl.BlockSpec((1, D_MODEL), fixed),
            pl.BlockSpec((1, D_MODEL), fixed),
        ],
        out_specs=[
            pl.BlockSpec((tm, D_MODEL), row),
            pl.BlockSpec((tm, D_MODEL), row),
        ],
        out_shape=[
            jax.ShapeDtypeStruct((s, D_MODEL), f32),
            jax.ShapeDtypeStruct((s, D_MODEL), bf16),
        ],
        scratch_shapes=[pltpu.VMEM((tm, D_MODEL), f32)],
        compiler_params=pltpu.CompilerParams(
            dimension_semantics=("arbitrary", "arbitrary"), vmem_limit_bytes=VMEM_LIMIT),
        name="moe_ffn",
    )(x1, x1b, comb, w_gate, w_up, w_down, ln_g, ln_b)


def _pad_lanes(a, width=LANES):
    return jnp.pad(a, ((0, 0), (0, width - a.shape[1])))


def kernel(x, w_in, conv_w, dn_a_log, dn_dt_bias, dn_norm_w, p_a, p_b, w_out, ln1_g, ln1_b, w_router, router_bias, w_gate, w_up, w_down, ln2_g, ln2_b):
    bsz, s, _ = x.shape
    assert bsz == 1 and s % 512 == 0
    xf = x[0]
    xb = xf.astype(bf16)

    c_ba = 4 * DN_WIDTH
    c_sb = c_ba + 2 * DN_HEADS
    c_gate = c_sb + SB_SLAB
    wr_pad = _pad_lanes(w_router)
    rb_pad = jnp.pad(router_bias[None, :], ((0, 0), (0, LANES - N_EXPERTS)), constant_values=-jnp.inf)
    head_pad = ((0, 0), (DN_HEADS, LANES - 2 * DN_HEADS))

    for l in range(DEPTH):
        w = w_in[l]
        w_dn = jnp.concatenate([w[:, :c_ba], _pad_lanes(w[:, c_ba:c_sb])], axis=1).astype(bf16)
        sb_scale = jnp.concatenate([jnp.full((SB_WIDTH,), SB_DH ** -0.5, f32), jnp.ones((2 * SB_WIDTH,), f32)])
        w_sb = (w[:, c_sb:c_gate] * sb_scale).astype(bf16)
        w_g = w[:, c_gate:].astype(bf16)

        p_dn, p_sb, gates = _proj(xb, w_dn, w_sb, w_g)
        o_a = _deltanet(p_dn,
                        jnp.pad(conv_w[l], ((0, 8 - CONV_K), (0, 0))),
                        jnp.pad(dn_a_log[l][None, :], head_pad),
                        jnp.pad(dn_dt_bias[l][None, :], head_pad),
                        dn_norm_w[l][None, :])
        o_b = _sb_attention(p_sb)
        x1, x1b, comb = _merge(xf, o_a, o_b, gates,
                               p_a[l].astype(bf16), p_b[l].astype(bf16), w_out[l].astype(bf16),
                               ln1_g[l][None, :], ln1_b[l][None, :], wr_pad, rb_pad)
        xf, xb = _moe(x1, x1b, comb,
                      w_gate[l].astype(bf16), w_up[l].astype(bf16), w_down[l].astype(bf16),
                      ln2_g[l][None, :], ln2_b[l][None, :])
    return xf[None]
```

```python
import jax
import jax.numpy as jnp
from jax import lax
from jax.experimental import pallas as pl
from jax.experimental.pallas import tpu as pltpu

f32 = jnp.float32
bf16 = jnp.bfloat16
HIGHEST = lax.Precision.HIGHEST

D_MODEL = 1024
DEPTH = 2
CHUNK = 64
DN_HEADS = 4
DN_DK = 128
DN_DV = 128
CONV_K = 4
SB_HEADS = 8
SB_DH = 64
SB_BLOCK = 128
N_EXPERTS = 16
N_GROUPS = 4
EXPERTS_PER_GROUP = N_EXPERTS // N_GROUPS
D_FF_EXPERT = 512
LN_EPS = 1e-5
RMS_EPS = 1e-6
DEEPNORM_ALPHA = (2 * DEPTH) ** 0.25

DN_WIDTH = DN_HEADS * DN_DK
SB_WIDTH = SB_HEADS * SB_DH
CONV_CH = 3 * DN_WIDTH
LANES = 128
DN_SLAB = 4 * DN_WIDTH + LANES
BA_COL = 4 * DN_WIDTH
SB_SLAB = 3 * SB_WIDTH
GATE_SLAB = 2 * D_MODEL

SB_LOG_ZERO = -88.0
SB_STATIC_BLOCKS = 3
SB_MASK_PENALTY = -1e30

VMEM_LIMIT = 48 * 1024 * 1024


def _sigmoid(x):
    return 1.0 / (1.0 + jnp.exp(-x))


def _softplus(x):
    return jnp.maximum(x, 0.0) + jnp.log1p(jnp.exp(-jnp.abs(x)))


def _nt_dot(a, b):
    return lax.dot_general(a, b, (((1,), (1,)), ((), ())), preferred_element_type=f32)


def _tn_dot(a, b):
    return lax.dot_general(a, b, (((0,), (0,)), ((), ())), preferred_element_type=f32)


def _dot_exact(a, b):
    return jnp.dot(a, b, preferred_element_type=f32, precision=HIGHEST)


def _layer_norm(y, g, b):
    mu = jnp.mean(y, axis=-1, keepdims=True)
    d = y - mu
    var = jnp.mean(d * d, axis=-1, keepdims=True)
    return d * lax.rsqrt(var + LN_EPS) * g + b


def _proj_kernel(x_ref, wdn_ref, wsb_ref, wg_ref, odn_ref, osb_ref, og_ref):
    x = x_ref[...]
    odn_ref[...] = jnp.dot(x, wdn_ref[...], preferred_element_type=f32)
    osb_ref[...] = jnp.dot(x, wsb_ref[...], preferred_element_type=f32).astype(bf16)
    og_ref[...] = jnp.dot(x, wg_ref[...], preferred_element_type=f32)


def _proj(xb, w_dn, w_sb, w_g, tm=256):
    s = xb.shape[0]
    row = lambda i: (i, 0)
    fixed = lambda i: (0, 0)
    return pl.pallas_call(
        _proj_kernel,
        grid=(s // tm,),
        in_specs=[
            pl.BlockSpec((tm, D_MODEL), row),
            pl.BlockSpec((D_MODEL, DN_SLAB), fixed),
            pl.BlockSpec((D_MODEL, SB_SLAB), fixed),
            pl.BlockSpec((D_MODEL, GATE_SLAB), fixed),
        ],
        out_specs=[
            pl.BlockSpec((tm, DN_SLAB), row),
            pl.BlockSpec((tm, SB_SLAB), row),
            pl.BlockSpec((tm, GATE_SLAB), row),
        ],
        out_shape=[
            jax.ShapeDtypeStruct((s, DN_SLAB), f32),
            jax.ShapeDtypeStruct((s, SB_SLAB), bf16),
            jax.ShapeDtypeStruct((s, GATE_SLAB), f32),
        ],
        compiler_params=pltpu.CompilerParams(
            dimension_semantics=("arbitrary",), vmem_limit_bytes=VMEM_LIMIT),
        name="proj",
    )(xb, w_dn, w_sb, w_g)


DN_ROWS = 256
PAIR = 2 * CHUNK
TAIL = 8


def _split2(x):
    hi = x.astype(bf16)
    return hi, (x - hi.astype(f32)).astype(bf16)


def _split3(x):
    hi = x.astype(bf16)
    r = x - hi.astype(f32)
    mid = r.astype(bf16)
    return hi, mid, (r - mid.astype(f32)).astype(bf16)


def _dn_kernel(p_ref, cw_ref, alog_ref, dtb_ref, nw_ref, o_ref, state_ref, tail_ref, xe_ref):
    step = pl.program_id(0)
    rows = p_ref.shape[0]
    n_pb = rows // PAIR

    @pl.when(step == 0)
    def _init():
        state_ref[...] = jnp.zeros_like(state_ref)
        tail_ref[...] = jnp.zeros_like(tail_ref)

    xe_ref[0:TAIL, :] = tail_ref[...]
    xe_ref[TAIL:TAIL + rows, :] = p_ref[:, 0:CONV_CH]
    tail_ref[...] = p_ref[rows - TAIL:rows, 0:CONV_CH]

    row128 = lax.broadcasted_iota(jnp.int32, (PAIR, LANES), 0)
    lane128 = lax.broadcasted_iota(jnp.int32, (PAIR, LANES), 1)
    same_chunk = (row128 >= CHUNK) == (lane128 >= CHUNK)
    tril_bd = jnp.where(jnp.logical_and(row128 >= lane128, same_chunk), 1.0, 0.0).astype(bf16)
    triu_bd = jnp.where(jnp.logical_and(row128 <= lane128, same_chunk), 1.0, 0.0).astype(bf16)
    first_rows = row128 < CHUNK
    row64 = lax.broadcasted_iota(jnp.int32, (CHUNK, LANES), 0)
    lane64 = lax.broadcasted_iota(jnp.int32, (CHUNK, LANES), 1)
    left = lane64 < CHUNK
    col_in_chunk = jnp.bitwise_and(lane64, CHUNK - 1)
    tri_p = row64 >= col_in_chunk
    strict_p = row64 > col_in_chunk
    eye_p = jnp.where(row64 == col_in_chunk, 1.0, 0.0).astype(f32)

    def block_diag(z):
        zero = jnp.zeros_like(z)
        return jnp.concatenate([jnp.where(left, z, zero), jnp.where(left, zero, z)], axis=0)

    def pair_matmul(y_hi, y_lo, zbd_hi, zbd_lo):
        return (jnp.dot(jnp.concatenate([y_hi, y_lo], axis=1), jnp.concatenate([zbd_hi, zbd_hi], axis=0),
                        preferred_element_type=f32)
                + jnp.dot(y_hi, zbd_lo, preferred_element_type=f32))

    gcol_all, grow_all, beta_all = [], [], []
    for pb in range(n_pb):
        ba = p_ref[pb * PAIR:(pb + 1) * PAIR, BA_COL:BA_COL + LANES]
        beta_all.append(_sigmoid(ba))
        g = -jnp.exp(alog_ref[...]) * _softplus(ba + dtb_ref[...])
        parts = _split3(g)
        gcol_all.append(sum(jnp.dot(tril_bd, pt, preferred_element_type=f32) for pt in parts))
        grow_all.append(sum(_tn_dot(pt, triu_bd) for pt in parts))

    units = [(h, pb) for pb in range(n_pb) for h in range(DN_HEADS)]

    base = TAIL - (CONV_K - 1)
    qs, ks, vs = [], [], []
    for h, pb in units:
        outs = []
        for grp in range(3):
            col = grp * DN_WIDTH + h * DN_DK
            r0 = base + pb * PAIR
            acc = xe_ref[r0:r0 + PAIR, col:col + DN_DK] * cw_ref[0:1, col:col + DN_DK]
            for j in range(1, CONV_K):
                acc = acc + xe_ref[r0 + j:r0 + j + PAIR, col:col + DN_DK] * cw_ref[j:j + 1, col:col + DN_DK]
            outs.append(acc * _sigmoid(acc))
        q, k, v = outs
        qs.append(q * lax.rsqrt(jnp.sum(q * q, axis=-1, keepdims=True) + RMS_EPS) * (DN_DK ** -0.5))
        ks.append(k * lax.rsqrt(jnp.sum(k * k, axis=-1, keepdims=True) + RMS_EPS))
        vs.append(v)

    kbetas, qgs, kdecs, rstacks, decays, egl = [], [], [], [], [], []
    for (h, pb), q, k, v in zip(units, qs, ks, vs):
        gc = jnp.broadcast_to(gcol_all[pb][:, DN_HEADS + h:DN_HEADS + h + 1], (PAIR, LANES))
        beta = jnp.broadcast_to(beta_all[pb][:, h:h + 1], (PAIR, LANES))
        eg = jnp.exp(gc)
        glast = jnp.where(first_rows, gc[CHUNK - 1:CHUNK, :], gc[PAIR - 1:PAIR, :])
        kbeta = k * beta
        kbetas.append(kbeta)
        qgs.append(q * eg)
        kdecs.append((k * jnp.exp(glast - gc)).astype(bf16))
        rstacks.append(jnp.concatenate([kbeta * eg, v * beta], axis=1).astype(bf16))
        gdiff = jnp.where(left, gc[:CHUNK], gc[CHUNK:]) - grow_all[pb][DN_HEADS + h:DN_HEADS + h + 1, :]
        decays.append(jnp.where(tri_p, jnp.exp(jnp.where(tri_p, gdiff, 0.0)), 0.0))
        egl.append((jnp.exp(gc[CHUNK - 1:CHUNK, :]), jnp.exp(gc[PAIR - 1:PAIR, :])))

    lps, qkms = [], []
    for kbeta, q, k, decay in zip(kbetas, qs, ks, decays):
        kk = _nt_dot(jnp.concatenate([kbeta, q], axis=0).astype(bf16), k.astype(bf16))
        lps.append(jnp.where(strict_p, jnp.where(left, kk[0:CHUNK], kk[CHUNK:PAIR]) * decay, 0.0))
        qkms.append(jnp.where(tri_p, jnp.where(left, kk[PAIR:PAIR + CHUNK], kk[PAIR + CHUNK:]) * decay, 0.0)
                    .astype(bf16))

    pw = [_split2(lp) for lp in lps]
    pw_bd = [(block_diag(hi), block_diag(lo)) for hi, lo in pw]
    ts = [eye_p - lp for lp in lps]
    for _ in range(5):
        sq = [pair_matmul(hi, lo, bhi, blo) for (hi, lo), (bhi, blo) in zip(pw, pw_bd)]
        pw = [_split2(x) for x in sq]
        pw_bd = [(block_diag(hi), block_diag(lo)) for hi, lo in pw]
        t_split = [_split2(t) for t in ts]
        ts = [t + pair_matmul(thi, tlo, bhi, blo) for t, (thi, tlo), (bhi, blo) in zip(ts, t_split, pw_bd)]

    zero_p = jnp.zeros((CHUNK, LANES), bf16)
    mq, bo = [], []
    for t, rstack, kdec, qkm, qg in zip(ts, rstacks, kdecs, qkms, qgs):
        t16 = t.astype(bf16)
        halves = (jnp.where(left, t16, zero_p), jnp.where(left, zero_p, t16))
        qk_halves = (jnp.where(left, qkm, zero_p), jnp.where(left, zero_p, qkm))
        wus = [jnp.dot(th, rstack, preferred_element_type=f32).astype(bf16) for th in halves]
        wu_stack = jnp.concatenate(wus, axis=0)
        mq_u, bo_u = [], []
        for c in range(2):
            cs = slice(c * CHUNK, (c + 1) * CHUNK)
            kw = _tn_dot(kdec[cs], wus[c])
            qw = jnp.dot(qk_halves[c], wu_stack, preferred_element_type=f32)
            mq_u.append(jnp.concatenate([kw[:, :DN_DK], qg[cs] - qw[:, :DN_DK]], axis=0).astype(bf16))
            bo_u.append((kw[:, DN_DK:], qw[:, DN_DK:]))
        mq.append(mq_u)
        bo.append(bo_u)

    nw = nw_ref[...]
    states = [state_ref[h] for h in range(DN_HEADS)]
    for pb in range(n_pb):
        for c in range(2):
            for h in range(DN_HEADS):
                ui = pb * DN_HEADS + h
                ms = jnp.dot(mq[ui][c], states[h].astype(bf16), preferred_element_type=f32)
                b_c, o_c = bo[ui][c]
                o = ms[DN_DK:] + o_c
                states[h] = states[h] * egl[ui][c] - ms[:DN_DK] + b_c
                o = o * lax.rsqrt(jnp.mean(o * o, axis=-1, keepdims=True) + RMS_EPS) * nw
                r0 = pb * PAIR + c * CHUNK
                z = p_ref[r0:r0 + CHUNK, 3 * DN_WIDTH + h * DN_DV:3 * DN_WIDTH + (h + 1) * DN_DV]
                o_ref[r0:r0 + CHUNK, h * DN_DV:(h + 1) * DN_DV] = o * (z * _sigmoid(z))
    for h in range(DN_HEADS):
        state_ref[h] = states[h]


def _deltanet(p_dn, conv_w8, alog_row, dtb_row, nw_row):
    s = p_dn.shape[0]
    fixed = lambda i: (0, 0)
    return pl.pallas_call(
        _dn_kernel,
        grid=(s // DN_ROWS,),
        in_specs=[
            pl.BlockSpec((DN_ROWS, DN_SLAB), lambda i: (i, 0)),
            pl.BlockSpec((8, CONV_CH), fixed),
            pl.BlockSpec((1, LANES), fixed),
            pl.BlockSpec((1, LANES), fixed),
            pl.BlockSpec((1, DN_DV), fixed),
        ],
        out_specs=pl.BlockSpec((DN_ROWS, DN_HEADS * DN_DV), lambda i: (i, 0)),
        out_shape=jax.ShapeDtypeStruct((s, DN_HEADS * DN_DV), f32),
        scratch_shapes=[
            pltpu.VMEM((DN_HEADS, DN_DK, DN_DV), f32),
            pltpu.VMEM((TAIL, CONV_CH), f32),
            pltpu.VMEM((TAIL + DN_ROWS, CONV_CH), f32),
        ],
        compiler_params=pltpu.CompilerParams(
            dimension_semantics=("arbitrary",), vmem_limit_bytes=VMEM_LIMIT),
        name="deltanet",
    )(p_dn, conv_w8, alog_row, dtb_row, nw_row)


N_PAIRS = SB_HEADS // 2


def _sb_kernel(q_ref, kd_ref, k1_ref, k2_ref, vd_ref, v1_ref, v2_ref, kv_hbm, o_ref,
               kbuf, vbuf, sem):
    qb = pl.program_id(0)
    blk = SB_BLOCK
    row = lax.broadcasted_iota(jnp.int32, (blk, blk), 0)
    lane = lax.broadcasted_iota(jnp.int32, (blk, blk), 1)
    diag_mask = row > lane
    even = lane < SB_DH
    suffix = jnp.where(row > lane, 1.0, 0.0).astype(bf16)
    suffix2 = jnp.concatenate([suffix, suffix], axis=0)

    def pair_block(q_pair, kblk, vblk, carries, penalty, is_diag):
        q_heads = (jnp.where(even, q_pair, jnp.zeros_like(q_pair)),
                   jnp.where(even, jnp.zeros_like(q_pair), q_pair))
        atts, new_carries = [], []
        for qh, carry in zip(q_heads, carries):
            z = _nt_dot(qh, kblk)
            sp = _softplus(z)
            lk = -sp
            if is_diag:
                lk = jnp.where(diag_mask, lk, 0.0)
            hi = lk.astype(bf16)
            lo = (lk - hi.astype(f32)).astype(bf16)
            within = jnp.dot(jnp.concatenate([hi, lo], axis=1), suffix2, preferred_element_type=f32)
            att = jnp.exp((z - sp) + within + (carry + penalty))
            if is_diag:
                att = jnp.where(diag_mask, att, 0.0)
            atts.append(att.astype(bf16))
            new_carries.append(carry + jnp.sum(lk, axis=1, keepdims=True))
        v2 = jnp.concatenate([jnp.where(even, vblk, jnp.zeros_like(vblk)),
                              jnp.where(even, jnp.zeros_like(vblk), vblk)], axis=0)
        out = jnp.dot(jnp.concatenate(atts, axis=1), v2, preferred_element_type=f32)
        return out, new_carries

    zero_col = jnp.zeros((blk, 1), f32)
    pen1 = jnp.where(qb >= 1, 0.0, SB_MASK_PENALTY).astype(f32)
    pen2 = jnp.where(qb >= 2, 0.0, SB_MASK_PENALTY).astype(f32)
    accs, carries = [], []
    for p in range(N_PAIRS):
        ps = slice(p * LANES, (p + 1) * LANES)
        q_pair = q_ref[:, ps]
        acc, cr = pair_block(q_pair, kd_ref[:, ps], vd_ref[:, ps], [zero_col, zero_col], 0.0, True)
        a1, cr = pair_block(q_pair, k1_ref[:, ps], v1_ref[:, ps], cr, pen1, False)
        a2, cr = pair_block(q_pair, k2_ref[:, ps], v2_ref[:, ps], cr, pen2, False)
        accs.append(acc + a1 + a2)
        carries.extend(cr)

    def live(cr):
        m = jnp.max(cr[0])
        for c in cr[1:]:
            m = jnp.maximum(m, jnp.max(c))
        return m >= SB_LOG_ZERO

    def cond(st):
        kb, more = st[0], st[1]
        return jnp.logical_and(kb >= 0, more)

    def body(st):
        kb = st[0]
        acc_l = list(st[2:2 + N_PAIRS])
        car_l = list(st[2 + N_PAIRS:])
        start = pl.multiple_of(kb * blk, blk)
        cpk = pltpu.make_async_copy(kv_hbm.at[pl.ds(start, blk), pl.ds(SB_WIDTH, SB_WIDTH)], kbuf, sem.at[0])
        cpv = pltpu.make_async_copy(kv_hbm.at[pl.ds(start, blk), pl.ds(2 * SB_WIDTH, SB_WIDTH)], vbuf, sem.at[1])
        cpk.start()
        cpv.start()
        cpk.wait()
        cpv.wait()
        new_car = []
        for p in range(N_PAIRS):
            ps = slice(p * LANES, (p + 1) * LANES)
            out, cr = pair_block(q_ref[:, ps], kbuf[:, ps], vbuf[:, ps],
                                 car_l[2 * p:2 * p + 2], 0.0, False)
            acc_l[p] = acc_l[p] + out
            new_car.extend(cr)
        return (kb - 1, live(new_car), *acc_l, *new_car)

    st = lax.while_loop(cond, body, (qb - SB_STATIC_BLOCKS, live(carries), *accs, *carries))
    for p in range(N_PAIRS):
        o_ref[:, p * LANES:(p + 1) * LANES] = st[2 + p]


def _sb_attention(p_sb):
    s = p_sb.shape[0]
    blk = SB_BLOCK

    def spec(col, back):
        return pl.BlockSpec((blk, SB_WIDTH), lambda i: (jnp.maximum(i - back, 0), col))

    return pl.pallas_call(
        _sb_kernel,
        grid=(s // blk,),
        in_specs=[spec(0, 0),
                  spec(1, 0), spec(1, 1), spec(1, 2),
                  spec(2, 0), spec(2, 1), spec(2, 2),
                  pl.BlockSpec(memory_space=pl.ANY)],
        out_specs=pl.BlockSpec((blk, SB_WIDTH), lambda i: (i, 0)),
        out_shape=jax.ShapeDtypeStruct((s, SB_WIDTH), f32),
        scratch_shapes=[
            pltpu.VMEM((blk, SB_WIDTH), bf16),
            pltpu.VMEM((blk, SB_WIDTH), bf16),
            pltpu.SemaphoreType.DMA((2,)),
        ],
        compiler_params=pltpu.CompilerParams(
            dimension_semantics=("arbitrary",), vmem_limit_bytes=VMEM_LIMIT),
        name="sb_attention",
    )(p_sb, p_sb, p_sb, p_sb, p_sb, p_sb, p_sb, p_sb)


MOE_TILE = 256
PAIRS_PER_GROUP = 6
N_CLASSES = N_GROUPS * PAIRS_PER_GROUP
X1E_W = D_MODEL + LANES
MERGE_ROWS = 512
MERGE_PARTS = 2


def _route(aff, sel, lane):
    grp = lax.shift_right_logical(lane, 2)
    neg = -jnp.inf
    best = None
    for gidx in range(N_GROUPS):
        sg = jnp.where(grp == gidx, sel, neg)
        m1 = jnp.max(sg, axis=1, keepdims=True)
        i1 = jnp.min(jnp.where(sg == m1, lane, LANES), axis=1, keepdims=True)
        sg2 = jnp.where(lane == i1, neg, sg)
        m2 = jnp.max(sg2, axis=1, keepdims=True)
        i2 = jnp.min(jnp.where(sg2 == m2, lane, LANES), axis=1, keepdims=True)
        score = m1 + m2
        if best is None:
            best = (score, i1, i2)
        else:
            better = score > best[0]
            best = (jnp.where(better, score, best[0]),
                    jnp.where(better, i1, best[1]),
                    jnp.where(better, i2, best[2]))
    _, i1, i2 = best
    w1 = jnp.sum(jnp.where(lane == i1, aff, 0.0), axis=1, keepdims=True)
    w2 = jnp.sum(jnp.where(lane == i2, aff, 0.0), axis=1, keepdims=True)
    denom = w1 + w2
    w1, w2 = w1 / denom, w2 / denom
    first_low = i1 < i2
    e_lo = jnp.minimum(i1, i2)
    e_hi = jnp.maximum(i1, i2)
    a = jnp.bitwise_and(e_lo, EXPERTS_PER_GROUP - 1)
    b = jnp.bitwise_and(e_hi, EXPERTS_PER_GROUP - 1)
    pair = jnp.where(a == 0, 0, jnp.where(a == 1, 3, 5)) + (b - a - 1)
    cls = lax.shift_right_logical(e_lo, 2) * PAIRS_PER_GROUP + pair
    return cls, jnp.where(first_low, w1, w2), jnp.where(first_low, w2, w1)


def _merge_kernel(x_ref, oa_ref, ob_ref, g_ref, pa_ref, pb_ref, wo_ref, lg_ref, lb_ref, wr_ref, rb_ref,
                  x1e_ref, meta_ref, cnt_ref, run_ref):
    step = pl.program_id(0)

    @pl.when(step == 0)
    def _init():
        run_ref[...] = jnp.zeros_like(run_ref)

    rows = x_ref.shape[0] // MERGE_PARTS
    parts = [slice(i * rows, (i + 1) * rows) for i in range(MERGE_PARTS)]
    a = [jnp.dot(oa_ref[p, :].astype(bf16), pa_ref[...], preferred_element_type=f32) for p in parts]
    b = [jnp.dot(ob_ref[p, :].astype(bf16), pb_ref[...], preferred_element_type=f32) for p in parts]
    merged = [(_sigmoid(g_ref[p, :D_MODEL]) * ai + _sigmoid(g_ref[p, D_MODEL:]) * bi).astype(bf16)
              for p, ai, bi in zip(parts, a, b)]
    mix = [jnp.dot(m, wo_ref[...], preferred_element_type=f32) for m in merged]
    x1 = [_layer_norm(DEEPNORM_ALPHA * x_ref[p, :] + mi, lg_ref[...], lb_ref[...]) for p, mi in zip(parts, mix)]
    x1b = [xi.astype(bf16) for xi in x1]
    for p, xi in zip(parts, x1):
        x1e_ref[p, :D_MODEL] = xi

    x_lo = [(xi - xbi.astype(f32)).astype(bf16) for xi, xbi in zip(x1, x1b)]
    t = [jnp.dot(xbi, wr_ref[...], preferred_element_type=f32) for xbi in x1b]
    logits = [ti[:, :LANES] + ti[:, LANES:] + jnp.dot(xl, wr_ref[:, :LANES], preferred_element_type=f32)
              for ti, xl in zip(t, x_lo)]
    lane = lax.broadcasted_iota(jnp.int32, (rows, LANES), 1)
    aff = [_sigmoid(lg) for lg in logits]
    routed = [_route(af, af + rb_ref[...], lane) for af in aff]

    rr = lax.broadcasted_iota(jnp.int32, (rows, rows), 0)
    cc = lax.broadcasted_iota(jnp.int32, (rows, rows), 1)
    tril = jnp.where(rr >= cc, 1.0, 0.0).astype(bf16)
    for p, (cls, w_lo, w_hi) in zip(parts, routed):
        onehot = lane == cls
        prefix = jnp.dot(tril, jnp.where(onehot, 1.0, 0.0).astype(bf16), preferred_element_type=f32)
        run = run_ref[...]
        rank = jnp.sum(jnp.where(onehot, prefix + run, 0.0), axis=1, keepdims=True) - 1.0
        run_ref[...] = run + prefix[rows - 1:rows, :]
        x1e_ref[p, D_MODEL:] = jnp.where(lane == 0, w_lo, jnp.where(lane == 1, w_hi, 0.0))
        meta_ref[p, :] = jnp.where(lane == 0, cls, jnp.where(lane == 1, rank.astype(jnp.int32), 0))
    cnt_ref[...] = jnp.broadcast_to(run_ref[...], cnt_ref.shape).astype(jnp.int32)


def _merge(x, o_a, o_b, gates, p_a, p_b, w_out, ln_g, ln_b, wr_cat, r_bias):
    s = x.shape[0]
    tm = MERGE_ROWS
    row = lambda i: (i, 0)
    fixed = lambda i: (0, 0)
    return pl.pallas_call(
        _merge_kernel,
        grid=(s // tm,),
        in_specs=[
            pl.BlockSpec((tm, D_MODEL), row),
            pl.BlockSpec((tm, DN_HEADS * DN_DV), row),
            pl.BlockSpec((tm, SB_WIDTH), row),
            pl.BlockSpec((tm, GATE_SLAB), row),
            pl.BlockSpec((DN_HEADS * DN_DV, D_MODEL), fixed),
            pl.BlockSpec((SB_WIDTH, D_MODEL), fixed),
            pl.BlockSpec((D_MODEL, D_MODEL), fixed),
            pl.BlockSpec((1, D_MODEL), fixed),
            pl.BlockSpec((1, D_MODEL), fixed),
            pl.BlockSpec((D_MODEL, 2 * LANES), fixed),
            pl.BlockSpec((1, LANES), fixed),
        ],
        out_specs=[
            pl.BlockSpec((tm, X1E_W), row),
            pl.BlockSpec((tm, LANES), row),
            pl.BlockSpec((8, LANES), fixed),
        ],
        out_shape=[
            jax.ShapeDtypeStruct((s, X1E_W), f32),
            jax.ShapeDtypeStruct((s, LANES), jnp.int32),
            jax.ShapeDtypeStruct((8, LANES), jnp.int32),
        ],
        scratch_shapes=[pltpu.VMEM((1, LANES), f32)],
        compiler_params=pltpu.CompilerParams(
            dimension_semantics=("arbitrary",), vmem_limit_bytes=VMEM_LIMIT),
        name="merge_router",
    )(x, o_a, o_b, gates, p_a, p_b, w_out, ln_g, ln_b, wr_cat, r_bias)


def _class_experts():
    lo, hi = [], []
    for g in range(N_GROUPS):
        for a in range(EXPERTS_PER_GROUP):
            for b in range(a + 1, EXPERTS_PER_GROUP):
                lo.append(g * EXPERTS_PER_GROUP + a)
                hi.append(g * EXPERTS_PER_GROUP + b)
    return jnp.array(lo, jnp.int32), jnp.array(hi, jnp.int32)


def _n_tiles(s):
    return -(-(s + N_CLASSES * (MOE_TILE - 1)) // MOE_TILE)


def _route_tables(meta, cnt, s):
    counts = cnt[0, :N_CLASSES]
    padded = (counts + (MOE_TILE - 1)) // MOE_TILE * MOE_TILE
    ends = jnp.cumsum(padded)
    offs = ends - padded
    dest = offs[meta[:, 0]] + meta[:, 1]
    n_active = (ends[-1] // MOE_TILE).astype(jnp.int32)[None]
    tile_row = jnp.minimum(jnp.arange(_n_tiles(s), dtype=jnp.int32) * MOE_TILE, ends[-1] - 1)
    tile_cls = jnp.minimum(jnp.searchsorted(ends, tile_row, side='right'), N_CLASSES - 1)
    e_lo, e_hi = _class_experts()
    tail = jnp.arange(s // MOE_TILE, _n_tiles(s), dtype=jnp.int32)
    pad_start = jnp.concatenate([ends - MOE_TILE, tail * MOE_TILE]).astype(jnp.int32)
    pad_valid = jnp.concatenate([padded > 0, tail >= n_active[0]]).astype(jnp.int32)
    return (dest.astype(jnp.int32), n_active, e_lo[tile_cls], e_hi[tile_cls], pad_start, pad_valid)


def _permute_kernel(dest_ref, pstart_ref, pvalid_ref, x_ref, xs_hbm, zero_ref, sem):
    step = pl.program_id(0)
    tp = x_ref.shape[0]

    @pl.when(step == 0)
    def _fill():
        zero_ref[...] = jnp.zeros_like(zero_ref)

        def fill_copy(c):
            start = pl.multiple_of(pstart_ref[c], MOE_TILE)
            return pltpu.make_async_copy(zero_ref, xs_hbm.at[pl.ds(start, MOE_TILE)], sem.at[1])

        for c in range(pstart_ref.shape[0]):
            @pl.when(pvalid_ref[c] != 0)
            def _start():
                fill_copy(c).start()
        for c in range(pstart_ref.shape[0]):
            @pl.when(pvalid_ref[c] != 0)
            def _wait():
                fill_copy(c).wait()

    base = step * tp

    def issue(r, carry):
        d = dest_ref[base + r]
        pltpu.make_async_copy(x_ref.at[pl.ds(r, 1)], xs_hbm.at[pl.ds(d, 1)], sem.at[0]).start()
        return carry

    lax.fori_loop(0, tp, issue, 0, unroll=8)
    pltpu.make_async_copy(x_ref, xs_hbm.at[pl.ds(0, tp)], sem.at[0]).wait()


def _permute(x1e, dest, pad_start, pad_valid, tp=512):
    s = x1e.shape[0]
    return pl.pallas_call(
        _permute_kernel,
        grid_spec=pltpu.PrefetchScalarGridSpec(
            num_scalar_prefetch=3,
            grid=(s // tp,),
            in_specs=[pl.BlockSpec((tp, X1E_W), lambda i, d, ps, pv: (i, 0))],
            out_specs=pl.BlockSpec(memory_space=pl.ANY),
            scratch_shapes=[pltpu.VMEM((MOE_TILE, X1E_W), f32), pltpu.SemaphoreType.DMA((2,))],
        ),
        out_shape=jax.ShapeDtypeStruct((_n_tiles(s) * MOE_TILE, X1E_W), f32),
        compiler_params=pltpu.CompilerParams(
            dimension_semantics=("arbitrary",), vmem_limit_bytes=VMEM_LIMIT),
        name="moe_permute",
    )(dest, pad_start, pad_valid, x1e)


def _moe_kernel(nact_ref, elo_ref, ehi_ref, xs_ref, wg0_ref, wu0_ref, wd0_ref, wg1_ref, wu1_ref, wd1_ref,
                ys_ref):
    @pl.when(pl.program_id(0) < nact_ref[0])
    def _tile():
        xb = xs_ref[:, :D_MODEL].astype(bf16)
        acc = None
        for col, (wg_ref, wu_ref, wd_ref) in enumerate(((wg0_ref, wu0_ref, wd0_ref),
                                                        (wg1_ref, wu1_ref, wd1_ref))):
            gate = jnp.dot(xb, wg_ref[0], preferred_element_type=f32)
            up = jnp.dot(xb, wu_ref[0], preferred_element_type=f32)
            hid = (gate * _sigmoid(gate)) * up * xs_ref[:, D_MODEL + col:D_MODEL + col + 1]
            part = jnp.dot(hid.astype(bf16), wd_ref[0], preferred_element_type=f32)
            acc = part if acc is None else acc + part
        ys_ref[...] = acc

    @pl.when(pl.program_id(0) >= nact_ref[0])
    def _unused_tile():
        ys_ref[...] = jnp.zeros_like(ys_ref)


def _moe(xs, n_active, tile_lo, tile_hi, w_gate, w_up, w_down):
    n_tiles = xs.shape[0] // MOE_TILE
    tile = lambda j, na, lo, hi: (jnp.minimum(j, na[0] - 1), 0)
    low = lambda j, na, lo, hi: (lo[j], 0, 0)
    high = lambda j, na, lo, hi: (hi[j], 0, 0)
    up_shape = (1, D_MODEL, D_FF_EXPERT)
    down_shape = (1, D_FF_EXPERT, D_MODEL)
    return pl.pallas_call(
        _moe_kernel,
        grid_spec=pltpu.PrefetchScalarGridSpec(
            num_scalar_prefetch=3,
            grid=(n_tiles,),
            in_specs=[
                pl.BlockSpec((MOE_TILE, X1E_W), tile),
                pl.BlockSpec(up_shape, low), pl.BlockSpec(up_shape, low), pl.BlockSpec(down_shape, low),
                pl.BlockSpec(up_shape, high), pl.BlockSpec(up_shape, high), pl.BlockSpec(down_shape, high),
            ],
            out_specs=pl.BlockSpec((MOE_TILE, D_MODEL), lambda j, na, lo, hi: (j, 0)),
        ),
        out_shape=jax.ShapeDtypeStruct((n_tiles * MOE_TILE, D_MODEL), f32),
        compiler_params=pltpu.CompilerParams(
            dimension_semantics=("arbitrary",), vmem_limit_bytes=VMEM_LIMIT),
        name="moe_ffn",
    )(n_active, tile_lo, tile_hi, xs, w_gate, w_up, w_down, w_gate, w_up, w_down)


def _unpermute_kernel(dest_ref, x1_ref, ys_hbm, lg_ref, lb_ref, out_ref, outb_ref, ybuf, sem):
    i = pl.program_id(0)
    n = pl.num_programs(0)
    tu = x1_ref.shape[0]

    def gather(tile, slot):
        base = tile * tu

        def issue(r, carry):
            d = dest_ref[base + r]
            pltpu.make_async_copy(ys_hbm.at[pl.ds(d, 1)], ybuf.at[slot, pl.ds(r, 1)], sem.at[slot]).start()
            return carry

        lax.fori_loop(0, tu, issue, 0, unroll=8)

    slot = lax.rem(i, 2)

    @pl.when(i == 0)
    def _first():
        gather(0, 0)

    @pl.when(i + 1 < n)
    def _next():
        gather(i + 1, 1 - slot)

    pltpu.make_async_copy(ys_hbm.at[pl.ds(0, tu)], ybuf.at[slot], sem.at[slot]).wait()
    y = _layer_norm(DEEPNORM_ALPHA * x1_ref[...] + ybuf[slot], lg_ref[...], lb_ref[...])
    out_ref[...] = y
    outb_ref[...] = y.astype(bf16)


def _unpermute(x1e, ys, dest, ln_g, ln_b, tu=256):
    s = x1e.shape[0]
    row = lambda i, d: (i, 0)
    fixed = lambda i, d: (0, 0)
    return pl.pallas_call(
        _unpermute_kernel,
        grid_spec=pltpu.PrefetchScalarGridSpec(
            num_scalar_prefetch=1,
            grid=(s // tu,),
            in_specs=[
                pl.BlockSpec((tu, D_MODEL), row),
                pl.BlockSpec(memory_space=pl.ANY),
                pl.BlockSpec((1, D_MODEL), fixed),
                pl.BlockSpec((1, D_MODEL), fixed),
            ],
            out_specs=[pl.BlockSpec((tu, D_MODEL), row), pl.BlockSpec((tu, D_MODEL), row)],
            scratch_shapes=[pltpu.VMEM((2, tu, D_MODEL), f32), pltpu.SemaphoreType.DMA((2,))],
        ),
        out_shape=[jax.ShapeDtypeStruct((s, D_MODEL), f32), jax.ShapeDtypeStruct((s, D_MODEL), bf16)],
        compiler_params=pltpu.CompilerParams(
            dimension_semantics=("arbitrary",), vmem_limit_bytes=VMEM_LIMIT),
        name="moe_unpermute_ln2",
    )(dest, x1e, ys, ln_g, ln_b)


def _pad_lanes(a, width=LANES):
    return jnp.pad(a, ((0, 0), (0, width - a.shape[1])))


def kernel(x, w_in, conv_w, dn_a_log, dn_dt_bias, dn_norm_w, p_a, p_b, w_out, ln1_g, ln1_b, w_router, router_bias, w_gate, w_up, w_down, ln2_g, ln2_b):
    bsz, s, _ = x.shape
    assert bsz == 1 and s % 512 == 0
    xf = x[0]
    xb = xf.astype(bf16)

    c_ba = 4 * DN_WIDTH
    c_sb = c_ba + 2 * DN_HEADS
    c_gate = c_sb + SB_SLAB
    wr_pad = _pad_lanes(w_router)
    wr_hi = wr_pad.astype(bf16)
    wr_cat = jnp.concatenate([wr_hi, (wr_pad - wr_hi.astype(f32)).astype(bf16)], axis=1)
    rb_pad = jnp.pad(router_bias[None, :], ((0, 0), (0, LANES - N_EXPERTS)), constant_values=-jnp.inf)
    head_pad = ((0, 0), (DN_HEADS, LANES - 2 * DN_HEADS))

    for l in range(DEPTH):
        w = w_in[l]
        w_dn = jnp.concatenate([w[:, :c_ba], _pad_lanes(w[:, c_ba:c_sb])], axis=1).astype(bf16)
        sb_scale = jnp.concatenate([jnp.full((SB_WIDTH,), SB_DH ** -0.5, f32), jnp.ones((2 * SB_WIDTH,), f32)])
        w_sb = (w[:, c_sb:c_gate] * sb_scale).astype(bf16)
        w_g = w[:, c_gate:].astype(bf16)

        p_dn, p_sb, gates = _proj(xb, w_dn, w_sb, w_g)
        o_a = _deltanet(p_dn,
                        jnp.pad(conv_w[l], ((0, 8 - CONV_K), (0, 0))),
                        jnp.pad(dn_a_log[l][None, :], head_pad),
                        jnp.pad(dn_dt_bias[l][None, :], head_pad),
                        dn_norm_w[l][None, :])
        o_b = _sb_attention(p_sb)
        x1e, meta, cnt = _merge(xf, o_a, o_b, gates,
                                p_a[l].astype(bf16), p_b[l].astype(bf16), w_out[l].astype(bf16),
                                ln1_g[l][None, :], ln1_b[l][None, :], wr_cat, rb_pad)
        dest, n_active, tile_lo, tile_hi, pad_start, pad_valid = _route_tables(meta, cnt, s)
        xs = _permute(x1e, dest, pad_start, pad_valid)
        ys = _moe(xs, n_active, tile_lo, tile_hi,
                  w_gate[l].astype(bf16), w_up[l].astype(bf16), w_down[l].astype(bf16))
        xf, xb = _unpermute(x1e, ys, dest, ln2_g[l][None, :], ln2_b[l][None, :])
    return xf[None]
```

```python
import jax
import jax.numpy as jnp
from jax import lax
from jax.experimental import pallas as pl
from jax.experimental.pallas import tpu as pltpu

f32 = jnp.float32
bf16 = jnp.bfloat16
HIGHEST = lax.Precision.HIGHEST

D_MODEL = 1024
DEPTH = 2
CHUNK = 64
DN_HEADS = 4
DN_DK = 128
DN_DV = 128
CONV_K = 4
SB_HEADS = 8
SB_DH = 64
SB_BLOCK = 128
N_EXPERTS = 16
N_GROUPS = 4
EXPERTS_PER_GROUP = N_EXPERTS // N_GROUPS
D_FF_EXPERT = 512
LN_EPS = 1e-5
RMS_EPS = 1e-6
DEEPNORM_ALPHA = (2 * DEPTH) ** 0.25

DN_WIDTH = DN_HEADS * DN_DK
SB_WIDTH = SB_HEADS * SB_DH
CONV_CH = 3 * DN_WIDTH
LANES = 128
DN_SLAB = 4 * DN_WIDTH + LANES
BA_COL = 4 * DN_WIDTH
SB_SLAB = 3 * SB_WIDTH
GATE_SLAB = 2 * D_MODEL

SB_LOG_ZERO = -88.0
SB_STATIC_BLOCKS = 2
SB_MASK_PENALTY = -1e30

VMEM_LIMIT = 48 * 1024 * 1024


def _sigmoid(x):
    return 1.0 / (1.0 + jnp.exp(-x))


def _softplus(x):
    return jnp.maximum(x, 0.0) + jnp.log1p(jnp.exp(-jnp.abs(x)))


def _nt_dot(a, b):
    return lax.dot_general(a, b, (((1,), (1,)), ((), ())), preferred_element_type=f32)


def _tn_dot(a, b):
    return lax.dot_general(a, b, (((0,), (0,)), ((), ())), preferred_element_type=f32)


def _dot_exact(a, b):
    return jnp.dot(a, b, preferred_element_type=f32, precision=HIGHEST)


def _layer_norm(y, g, b):
    mu = jnp.mean(y, axis=-1, keepdims=True)
    d = y - mu
    var = jnp.mean(d * d, axis=-1, keepdims=True)
    return d * lax.rsqrt(var + LN_EPS) * g + b


def _proj_kernel(x_ref, wdn_ref, wsb_ref, wg_ref, odn_ref, osb_ref, og_ref):
    x = x_ref[...].astype(bf16)
    odn_ref[...] = jnp.dot(x, wdn_ref[...], preferred_element_type=f32)
    osb_ref[...] = jnp.dot(x, wsb_ref[...], preferred_element_type=f32).astype(bf16)
    og_ref[...] = jnp.dot(x, wg_ref[...], preferred_element_type=f32)


def _proj(x, w_dn, w_sb, w_g, tm=256):
    s = x.shape[0]
    row = lambda i: (i, 0)
    fixed = lambda i: (0, 0)
    return pl.pallas_call(
        _proj_kernel,
        grid=(s // tm,),
        in_specs=[
            pl.BlockSpec((tm, D_MODEL), row),
            pl.BlockSpec((D_MODEL, DN_SLAB), fixed),
            pl.BlockSpec((D_MODEL, SB_SLAB), fixed),
            pl.BlockSpec((D_MODEL, GATE_SLAB), fixed),
        ],
        out_specs=[
            pl.BlockSpec((tm, DN_SLAB), row),
            pl.BlockSpec((tm, SB_SLAB), row),
            pl.BlockSpec((tm, GATE_SLAB), row),
        ],
        out_shape=[
            jax.ShapeDtypeStruct((s, DN_SLAB), f32),
            jax.ShapeDtypeStruct((s, SB_SLAB), bf16),
            jax.ShapeDtypeStruct((s, GATE_SLAB), f32),
        ],
        compiler_params=pltpu.CompilerParams(
            dimension_semantics=("arbitrary",), vmem_limit_bytes=VMEM_LIMIT),
        name="proj",
    )(x, w_dn, w_sb, w_g)


DN_ROWS = 256
PAIR = 2 * CHUNK
TAIL = 8


def _split2(x):
    hi = x.astype(bf16)
    return hi, (x - hi.astype(f32)).astype(bf16)


def _split3(x):
    hi = x.astype(bf16)
    r = x - hi.astype(f32)
    mid = r.astype(bf16)
    return hi, mid, (r - mid.astype(f32)).astype(bf16)


def _dn_kernel(p_ref, cw_ref, alog_ref, dtb_ref, nw_ref, o_ref, state_ref, tail_ref, xe_ref):
    step = pl.program_id(0)
    rows = p_ref.shape[0]
    n_pb = rows // PAIR

    @pl.when(step == 0)
    def _init():
        state_ref[...] = jnp.zeros_like(state_ref)
        tail_ref[...] = jnp.zeros_like(tail_ref)

    xe_ref[0:TAIL, :] = tail_ref[...]
    xe_ref[TAIL:TAIL + rows, :] = p_ref[:, 0:CONV_CH]
    tail_ref[...] = p_ref[rows - TAIL:rows, 0:CONV_CH]

    row128 = lax.broadcasted_iota(jnp.int32, (PAIR, LANES), 0)
    lane128 = lax.broadcasted_iota(jnp.int32, (PAIR, LANES), 1)
    same_chunk = (row128 >= CHUNK) == (lane128 >= CHUNK)
    tril_bd = jnp.where(jnp.logical_and(row128 >= lane128, same_chunk), 1.0, 0.0).astype(bf16)
    triu_bd = jnp.where(jnp.logical_and(row128 <= lane128, same_chunk), 1.0, 0.0).astype(bf16)
    first_rows = row128 < CHUNK
    row64 = lax.broadcasted_iota(jnp.int32, (CHUNK, LANES), 0)
    lane64 = lax.broadcasted_iota(jnp.int32, (CHUNK, LANES), 1)
    left = lane64 < CHUNK
    col_in_chunk = jnp.bitwise_and(lane64, CHUNK - 1)
    tri_p = row64 >= col_in_chunk
    strict_p = row64 > col_in_chunk
    eye_p = jnp.where(row64 == col_in_chunk, 1.0, 0.0).astype(f32)

    def block_diag(z):
        zero = jnp.zeros_like(z)
        return jnp.concatenate([jnp.where(left, z, zero), jnp.where(left, zero, z)], axis=0)

    def pair_matmul(y_hi, y_lo, zbd_hi, zbd_lo):
        return (jnp.dot(jnp.concatenate([y_hi, y_lo], axis=1), jnp.concatenate([zbd_hi, zbd_hi], axis=0),
                        preferred_element_type=f32)
                + jnp.dot(y_hi, zbd_lo, preferred_element_type=f32))

    gcol_all, grow_all, beta_all = [], [], []
    for pb in range(n_pb):
        ba = p_ref[pb * PAIR:(pb + 1) * PAIR, BA_COL:BA_COL + LANES]
        beta_all.append(_sigmoid(ba))
        g = -jnp.exp(alog_ref[...]) * _softplus(ba + dtb_ref[...])
        parts = _split3(g)
        gcol_all.append(sum(jnp.dot(tril_bd, pt, preferred_element_type=f32) for pt in parts))
        grow_all.append(sum(_tn_dot(pt, triu_bd) for pt in parts))

    units = [(h, pb) for pb in range(n_pb) for h in range(DN_HEADS)]

    base = TAIL - (CONV_K - 1)
    qs, ks, vs = [], [], []
    for h, pb in units:
        outs = []
        for grp in range(3):
            col = grp * DN_WIDTH + h * DN_DK
            r0 = base + pb * PAIR
            acc = xe_ref[r0:r0 + PAIR, col:col + DN_DK] * cw_ref[0:1, col:col + DN_DK]
            for j in range(1, CONV_K):
                acc = acc + xe_ref[r0 + j:r0 + j + PAIR, col:col + DN_DK] * cw_ref[j:j + 1, col:col + DN_DK]
            outs.append(acc * _sigmoid(acc))
        q, k, v = outs
        qs.append(q * lax.rsqrt(jnp.sum(q * q, axis=-1, keepdims=True) + RMS_EPS) * (DN_DK ** -0.5))
        ks.append(k * lax.rsqrt(jnp.sum(k * k, axis=-1, keepdims=True) + RMS_EPS))
        vs.append(v)

    kbetas, qgs, kdecs, rstacks, decays, egl = [], [], [], [], [], []
    for (h, pb), q, k, v in zip(units, qs, ks, vs):
        gc = jnp.broadcast_to(gcol_all[pb][:, DN_HEADS + h:DN_HEADS + h + 1], (PAIR, LANES))
        beta = jnp.broadcast_to(beta_all[pb][:, h:h + 1], (PAIR, LANES))
        eg = jnp.exp(gc)
        glast = jnp.where(first_rows, gc[CHUNK - 1:CHUNK, :], gc[PAIR - 1:PAIR, :])
        kbeta = k * beta
        kbetas.append(kbeta)
        qgs.append(q * eg)
        kdecs.append((k * jnp.exp(glast - gc)).astype(bf16))
        rstacks.append(jnp.concatenate([kbeta * eg, v * beta], axis=1).astype(bf16))
        gdiff = jnp.where(left, gc[:CHUNK], gc[CHUNK:]) - grow_all[pb][DN_HEADS + h:DN_HEADS + h + 1, :]
        decays.append(jnp.where(tri_p, jnp.exp(jnp.where(tri_p, gdiff, 0.0)), 0.0))
        egl.append((jnp.exp(gc[CHUNK - 1:CHUNK, :]), jnp.exp(gc[PAIR - 1:PAIR, :])))

    lps, qkms = [], []
    for kbeta, q, k, decay in zip(kbetas, qs, ks, decays):
        kk = _nt_dot(jnp.concatenate([kbeta, q], axis=0).astype(bf16), k.astype(bf16))
        lps.append(jnp.where(strict_p, jnp.where(left, kk[0:CHUNK], kk[CHUNK:PAIR]) * decay, 0.0))
        qkms.append(jnp.where(tri_p, jnp.where(left, kk[PAIR:PAIR + CHUNK], kk[PAIR + CHUNK:]) * decay, 0.0)
                    .astype(bf16))

    pw = [_split2(lp) for lp in lps]
    pw_bd = [(block_diag(hi), block_diag(lo)) for hi, lo in pw]
    ts = [eye_p - lp for lp in lps]
    for _ in range(5):
        sq = [pair_matmul(hi, lo, bhi, blo) for (hi, lo), (bhi, blo) in zip(pw, pw_bd)]
        pw = [_split2(x) for x in sq]
        pw_bd = [(block_diag(hi), block_diag(lo)) for hi, lo in pw]
        t_split = [_split2(t) for t in ts]
        ts = [t + pair_matmul(thi, tlo, bhi, blo) for t, (thi, tlo), (bhi, blo) in zip(ts, t_split, pw_bd)]

    zero_p = jnp.zeros((CHUNK, LANES), bf16)
    mq, bo = [], []
    for t, rstack, kdec, qkm, qg in zip(ts, rstacks, kdecs, qkms, qgs):
        t16 = t.astype(bf16)
        halves = (jnp.where(left, t16, zero_p), jnp.where(left, zero_p, t16))
        qk_halves = (jnp.where(left, qkm, zero_p), jnp.where(left, zero_p, qkm))
        wus = [jnp.dot(th, rstack, preferred_element_type=f32).astype(bf16) for th in halves]
        wu_stack = jnp.concatenate(wus, axis=0)
        mq_u, bo_u = [], []
        for c in range(2):
            cs = slice(c * CHUNK, (c + 1) * CHUNK)
            kw = _tn_dot(kdec[cs], wus[c])
            qw = jnp.dot(qk_halves[c], wu_stack, preferred_element_type=f32)
            mq_u.append(jnp.concatenate([kw[:, :DN_DK], qg[cs] - qw[:, :DN_DK]], axis=0).astype(bf16))
            bo_u.append((kw[:, DN_DK:], qw[:, DN_DK:]))
        mq.append(mq_u)
        bo.append(bo_u)

    nw = nw_ref[...]
    states = [state_ref[h] for h in range(DN_HEADS)]
    for pb in range(n_pb):
        for c in range(2):
            for h in range(DN_HEADS):
                ui = pb * DN_HEADS + h
                ms = jnp.dot(mq[ui][c], states[h].astype(bf16), preferred_element_type=f32)
                b_c, o_c = bo[ui][c]
                o = ms[DN_DK:] + o_c
                states[h] = states[h] * egl[ui][c] - ms[:DN_DK] + b_c
                o = o * lax.rsqrt(jnp.mean(o * o, axis=-1, keepdims=True) + RMS_EPS) * nw
                r0 = pb * PAIR + c * CHUNK
                z = p_ref[r0:r0 + CHUNK, 3 * DN_WIDTH + h * DN_DV:3 * DN_WIDTH + (h + 1) * DN_DV]
                o_ref[r0:r0 + CHUNK, h * DN_DV:(h + 1) * DN_DV] = o * (z * _sigmoid(z))
    for h in range(DN_HEADS):
        state_ref[h] = states[h]


def _deltanet(p_dn, conv_w8, alog_row, dtb_row, nw_row):
    s = p_dn.shape[0]
    fixed = lambda i: (0, 0)
    return pl.pallas_call(
        _dn_kernel,
        grid=(s // DN_ROWS,),
        in_specs=[
            pl.BlockSpec((DN_ROWS, DN_SLAB), lambda i: (i, 0)),
            pl.BlockSpec((8, CONV_CH), fixed),
            pl.BlockSpec((1, LANES), fixed),
            pl.BlockSpec((1, LANES), fixed),
            pl.BlockSpec((1, DN_DV), fixed),
        ],
        out_specs=pl.BlockSpec((DN_ROWS, DN_HEADS * DN_DV), lambda i: (i, 0)),
        out_shape=jax.ShapeDtypeStruct((s, DN_HEADS * DN_DV), f32),
        scratch_shapes=[
            pltpu.VMEM((DN_HEADS, DN_DK, DN_DV), f32),
            pltpu.VMEM((TAIL, CONV_CH), f32),
            pltpu.VMEM((TAIL + DN_ROWS, CONV_CH), f32),
        ],
        compiler_params=pltpu.CompilerParams(
            dimension_semantics=("arbitrary",), vmem_limit_bytes=VMEM_LIMIT),
        name="deltanet",
    )(p_dn, conv_w8, alog_row, dtb_row, nw_row)


N_PAIRS = SB_HEADS // 2


def _sb_kernel(q_ref, kd_ref, k1_ref, vd_ref, v1_ref, kv_hbm, o_ref, kbuf, vbuf, sem):
    qb = pl.program_id(0)
    blk = SB_BLOCK
    row = lax.broadcasted_iota(jnp.int32, (blk, blk), 0)
    lane = lax.broadcasted_iota(jnp.int32, (blk, blk), 1)
    diag_mask = row > lane
    even = lane < SB_DH
    suffix = jnp.where(row > lane, 1.0, 0.0).astype(bf16)
    suffix2 = jnp.concatenate([suffix, suffix], axis=0)

    def split_heads(x):
        zero = jnp.zeros_like(x)
        return jnp.where(even, x, zero), jnp.where(even, zero, x)

    def suffix_sums(spm):
        hi = spm.astype(bf16)
        lo = (spm - hi.astype(f32)).astype(bf16)
        return jnp.dot(jnp.concatenate([hi, lo], axis=1), suffix2, preferred_element_type=f32)

    pairs = [slice(p * LANES, (p + 1) * LANES) for p in range(N_PAIRS)]
    q_heads = [split_heads(q_ref[:, ps]) for ps in pairs]
    units = [(p, b, hh) for b in range(SB_STATIC_BLOCKS) for p in range(N_PAIRS) for hh in range(2)]
    k_refs = (kd_ref, k1_ref)
    v_refs = (vd_ref, v1_ref)

    z = {u: _nt_dot(q_heads[u[0]][u[2]], k_refs[u[1]][:, pairs[u[0]]]) for u in units}
    sp = {u: _softplus(z[u]) for u in units}
    spm = {u: (jnp.where(diag_mask, sp[u], 0.0) if u[1] == 0 else sp[u]) for u in units}
    logw = {u: z[u] - sp[u] - suffix_sums(spm[u]) for u in units}
    keep = {u: jnp.sum(spm[u], axis=1, keepdims=True) for u in units}
    pen1 = jnp.where(qb >= 1, 0.0, -SB_MASK_PENALTY).astype(f32)
    att = {}
    for p, b, hh in units:
        if b == 0:
            att[p, b, hh] = jnp.where(diag_mask, jnp.exp(logw[p, b, hh]), 0.0).astype(bf16)
        else:
            att[p, b, hh] = jnp.exp(logw[p, b, hh] - (keep[p, 0, hh] + pen1)).astype(bf16)
    accs = []
    for p in range(N_PAIRS):
        vals = jnp.concatenate([h for b in range(SB_STATIC_BLOCKS) for h in split_heads(v_refs[b][:, pairs[p]])],
                               axis=0)
        lhs = jnp.concatenate([att[p, b, hh] for b in range(SB_STATIC_BLOCKS) for hh in range(2)], axis=1)
        accs.append(jnp.dot(lhs, vals, preferred_element_type=f32))
    carries = [keep[p, 0, hh] + keep[p, 1, hh] for p in range(N_PAIRS) for hh in range(2)]

    def live(cr):
        m = jnp.min(cr[0])
        for c in cr[1:]:
            m = jnp.minimum(m, jnp.min(c))
        return m <= -SB_LOG_ZERO

    def cond(st):
        kb, more = st[0], st[1]
        return jnp.logical_and(kb >= 0, more)

    def body(st):
        kb = st[0]
        acc_l = list(st[2:2 + N_PAIRS])
        car_l = list(st[2 + N_PAIRS:])
        start = pl.multiple_of(kb * blk, blk)
        cpk = pltpu.make_async_copy(kv_hbm.at[pl.ds(start, blk), pl.ds(SB_WIDTH, SB_WIDTH)], kbuf, sem.at[0])
        cpv = pltpu.make_async_copy(kv_hbm.at[pl.ds(start, blk), pl.ds(2 * SB_WIDTH, SB_WIDTH)], vbuf, sem.at[1])
        cpk.start()
        cpv.start()
        cpk.wait()
        cpv.wait()
        new_car = []
        for p, ps in enumerate(pairs):
            atts = []
            for hh, qh in enumerate(split_heads(q_ref[:, ps])):
                zz = _nt_dot(qh, kbuf[:, ps])
                spp = _softplus(zz)
                carry = car_l[2 * p + hh]
                atts.append(jnp.exp(zz - spp - suffix_sums(spp) - carry).astype(bf16))
                new_car.append(carry + jnp.sum(spp, axis=1, keepdims=True))
            vals = jnp.concatenate(split_heads(vbuf[:, ps]), axis=0)
            acc_l[p] = acc_l[p] + jnp.dot(jnp.concatenate(atts, axis=1), vals, preferred_element_type=f32)
        return (kb - 1, live(new_car), *acc_l, *new_car)

    st = lax.while_loop(cond, body, (qb - SB_STATIC_BLOCKS, live(carries), *accs, *carries))
    for p, ps in enumerate(pairs):
        o_ref[:, ps] = st[2 + p]


def _sb_attention(p_sb):
    s = p_sb.shape[0]
    blk = SB_BLOCK

    def spec(col, back):
        return pl.BlockSpec((blk, SB_WIDTH), lambda i: (jnp.maximum(i - back, 0), col))

    return pl.pallas_call(
        _sb_kernel,
        grid=(s // blk,),
        in_specs=[spec(0, 0), spec(1, 0), spec(1, 1), spec(2, 0), spec(2, 1),
                  pl.BlockSpec(memory_space=pl.ANY)],
        out_specs=pl.BlockSpec((blk, SB_WIDTH), lambda i: (i, 0)),
        out_shape=jax.ShapeDtypeStruct((s, SB_WIDTH), f32),
        scratch_shapes=[
            pltpu.VMEM((blk, SB_WIDTH), bf16),
            pltpu.VMEM((blk, SB_WIDTH), bf16),
            pltpu.SemaphoreType.DMA((2,)),
        ],
        compiler_params=pltpu.CompilerParams(
            dimension_semantics=("arbitrary",), vmem_limit_bytes=VMEM_LIMIT),
        name="sb_attention",
    )(p_sb, p_sb, p_sb, p_sb, p_sb, p_sb)


MOE_TILE = 256
PAIRS_PER_GROUP = 6
N_CLASSES = N_GROUPS * PAIRS_PER_GROUP
X1E_W = D_MODEL + LANES
MERGE_ROWS = 512
MERGE_PARTS = 2


def _route(aff, sel, lane):
    grp = lax.shift_right_logical(lane, 2)
    neg = -jnp.inf
    best = None
    for gidx in range(N_GROUPS):
        sg = jnp.where(grp == gidx, sel, neg)
        m1 = jnp.max(sg, axis=1, keepdims=True)
        i1 = jnp.min(jnp.where(sg == m1, lane, LANES), axis=1, keepdims=True)
        sg2 = jnp.where(lane == i1, neg, sg)
        m2 = jnp.max(sg2, axis=1, keepdims=True)
        i2 = jnp.min(jnp.where(sg2 == m2, lane, LANES), axis=1, keepdims=True)
        score = m1 + m2
        if best is None:
            best = (score, i1, i2)
        else:
            better = score > best[0]
            best = (jnp.where(better, score, best[0]),
                    jnp.where(better, i1, best[1]),
                    jnp.where(better, i2, best[2]))
    _, i1, i2 = best
    w1 = jnp.sum(jnp.where(lane == i1, aff, 0.0), axis=1, keepdims=True)
    w2 = jnp.sum(jnp.where(lane == i2, aff, 0.0), axis=1, keepdims=True)
    denom = w1 + w2
    w1, w2 = w1 / denom, w2 / denom
    first_low = i1 < i2
    e_lo = jnp.minimum(i1, i2)
    e_hi = jnp.maximum(i1, i2)
    a = jnp.bitwise_and(e_lo, EXPERTS_PER_GROUP - 1)
    b = jnp.bitwise_and(e_hi, EXPERTS_PER_GROUP - 1)
    pair = jnp.where(a == 0, 0, jnp.where(a == 1, 3, 5)) + (b - a - 1)
    cls = lax.shift_right_logical(e_lo, 2) * PAIRS_PER_GROUP + pair
    return cls, jnp.where(first_low, w1, w2), jnp.where(first_low, w2, w1)


def _merge_kernel(x_ref, oa_ref, ob_ref, g_ref, pa_ref, pb_ref, wo_ref, lg_ref, lb_ref, wr_ref, rb_ref,
                  x1e_ref, route_ref, cnt_ref, run_ref):
    step = pl.program_id(0)

    @pl.when(step == 0)
    def _init():
        run_ref[...] = jnp.zeros_like(run_ref)

    rows = x_ref.shape[0] // MERGE_PARTS
    parts = [slice(i * rows, (i + 1) * rows) for i in range(MERGE_PARTS)]
    a = [jnp.dot(oa_ref[p, :].astype(bf16), pa_ref[...], preferred_element_type=f32) for p in parts]
    b = [jnp.dot(ob_ref[p, :].astype(bf16), pb_ref[...], preferred_element_type=f32) for p in parts]
    merged = [(_sigmoid(g_ref[p, :D_MODEL]) * ai + _sigmoid(g_ref[p, D_MODEL:]) * bi).astype(bf16)
              for p, ai, bi in zip(parts, a, b)]
    mix = [jnp.dot(m, wo_ref[...], preferred_element_type=f32) for m in merged]
    x1 = [_layer_norm(DEEPNORM_ALPHA * x_ref[p, :] + mi, lg_ref[...], lb_ref[...]) for p, mi in zip(parts, mix)]
    x1b = [xi.astype(bf16) for xi in x1]
    for p, xi in zip(parts, x1):
        x1e_ref[p, :D_MODEL] = xi

    x_lo = [(xi - xbi.astype(f32)).astype(bf16) for xi, xbi in zip(x1, x1b)]
    t = [jnp.dot(xbi, wr_ref[...], preferred_element_type=f32) for xbi in x1b]
    logits = [ti[:, :LANES] + ti[:, LANES:] + jnp.dot(xl, wr_ref[:, :LANES], preferred_element_type=f32)
              for ti, xl in zip(t, x_lo)]
    lane = lax.broadcasted_iota(jnp.int32, (rows, LANES), 1)
    aff = [_sigmoid(lg) for lg in logits]
    routed = [_route(af, af + rb_ref[...], lane) for af in aff]

    rr = lax.broadcasted_iota(jnp.int32, (rows, rows), 0)
    cc = lax.broadcasted_iota(jnp.int32, (rows, rows), 1)
    tril = jnp.where(rr >= cc, 1.0, 0.0).astype(bf16)
    pick = jnp.where(lax.broadcasted_iota(jnp.int32, (8, LANES), 0)
                     == lax.broadcasted_iota(jnp.int32, (8, LANES), 1), 1.0, 0.0).astype(bf16)
    for p, (cls, w_lo, w_hi) in zip(parts, routed):
        onehot = lane == cls
        prefix = jnp.dot(tril, jnp.where(onehot, 1.0, 0.0).astype(bf16), preferred_element_type=f32)
        run = run_ref[...]
        rank = jnp.sum(jnp.where(onehot, prefix + run, 0.0), axis=1, keepdims=True) - 1.0
        run_ref[...] = run + prefix[rows - 1:rows, :]
        x1e_ref[p, D_MODEL:] = jnp.where(lane == 0, w_lo, jnp.where(lane == 1, w_hi, 0.0))
        rank_i = rank.astype(jnp.int32)
        digits = jnp.where(lane == 0, cls, jnp.where(lane == 1, lax.shift_right_logical(rank_i, 7),
                                                     jnp.where(lane == 2, jnp.bitwise_and(rank_i, LANES - 1), 0)))
        route_ref[0, :, p] = _nt_dot(pick, digits.astype(f32).astype(bf16)).astype(jnp.int32)
    cnt_ref[...] = jnp.broadcast_to(run_ref[...], cnt_ref.shape).astype(jnp.int32)


def _merge(x, o_a, o_b, gates, p_a, p_b, w_out, ln_g, ln_b, wr_cat, r_bias):
    s = x.shape[0]
    tm = MERGE_ROWS
    row = lambda i: (i, 0)
    fixed = lambda i: (0, 0)
    return pl.pallas_call(
        _merge_kernel,
        grid=(s // tm,),
        in_specs=[
            pl.BlockSpec((tm, D_MODEL), row),
            pl.BlockSpec((tm, DN_HEADS * DN_DV), row),
            pl.BlockSpec((tm, SB_WIDTH), row),
            pl.BlockSpec((tm, GATE_SLAB), row),
            pl.BlockSpec((DN_HEADS * DN_DV, D_MODEL), fixed),
            pl.BlockSpec((SB_WIDTH, D_MODEL), fixed),
            pl.BlockSpec((D_MODEL, D_MODEL), fixed),
            pl.BlockSpec((1, D_MODEL), fixed),
            pl.BlockSpec((1, D_MODEL), fixed),
            pl.BlockSpec((D_MODEL, 2 * LANES), fixed),
            pl.BlockSpec((1, LANES), fixed),
        ],
        out_specs=[
            pl.BlockSpec((tm, X1E_W), row),
            pl.BlockSpec((1, 8, tm), lambda i: (i, 0, 0)),
            pl.BlockSpec((8, LANES), fixed),
        ],
        out_shape=[
            jax.ShapeDtypeStruct((s, X1E_W), f32),
            jax.ShapeDtypeStruct((s // tm, 8, tm), jnp.int32),
            jax.ShapeDtypeStruct((8, LANES), jnp.int32),
        ],
        scratch_shapes=[pltpu.VMEM((1, LANES), f32)],
        compiler_params=pltpu.CompilerParams(
            dimension_semantics=("arbitrary",), vmem_limit_bytes=VMEM_LIMIT),
        name="merge_router",
    )(x, o_a, o_b, gates, p_a, p_b, w_out, ln_g, ln_b, wr_cat, r_bias)


def _class_experts():
    lo, hi = [], []
    for g in range(N_GROUPS):
        for a in range(EXPERTS_PER_GROUP):
            for b in range(a + 1, EXPERTS_PER_GROUP):
                lo.append(g * EXPERTS_PER_GROUP + a)
                hi.append(g * EXPERTS_PER_GROUP + b)
    return jnp.array(lo, jnp.int32), jnp.array(hi, jnp.int32)


def _n_tiles(s):
    return -(-(s + N_CLASSES * (MOE_TILE - 1)) // MOE_TILE)


def _route_tables(route, cnt, s):
    counts = cnt[0, :N_CLASSES]
    padded = (counts + (MOE_TILE - 1)) // MOE_TILE * MOE_TILE
    ends = jnp.cumsum(padded)
    offs = ends - padded
    cls, rank_hi, rank_lo = (route[:, r, :].reshape(s) for r in range(3))
    dest = offs[cls] + rank_hi * LANES + rank_lo
    n_active = (ends[-1] // MOE_TILE).astype(jnp.int32)[None]
    tile_row = jnp.minimum(jnp.arange(_n_tiles(s), dtype=jnp.int32) * MOE_TILE, ends[-1] - 1)
    tile_cls = jnp.minimum(jnp.sum(tile_row[:, None] >= ends[None, :], axis=1), N_CLASSES - 1)
    e_lo, e_hi = _class_experts()
    tail = jnp.arange(s // MOE_TILE, _n_tiles(s), dtype=jnp.int32)
    pad_start = jnp.concatenate([ends - MOE_TILE, tail * MOE_TILE]).astype(jnp.int32)
    pad_valid = jnp.concatenate([padded > 0, tail >= n_active[0]]).astype(jnp.int32)
    return (dest.astype(jnp.int32), n_active, e_lo[tile_cls], e_hi[tile_cls], pad_start, pad_valid)


def _permute_kernel(dest_ref, pstart_ref, pvalid_ref, x_ref, xs_hbm, zero_ref, sem):
    step = pl.program_id(0)
    tp = x_ref.shape[0]

    @pl.when(step == 0)
    def _fill():
        zero_ref[...] = jnp.zeros_like(zero_ref)

        def fill_copy(c):
            start = pl.multiple_of(pstart_ref[c], MOE_TILE)
            return pltpu.make_async_copy(zero_ref, xs_hbm.at[pl.ds(start, MOE_TILE)], sem.at[1])

        for c in range(pstart_ref.shape[0]):
            @pl.when(pvalid_ref[c] != 0)
            def _start():
                fill_copy(c).start()
        for c in range(pstart_ref.shape[0]):
            @pl.when(pvalid_ref[c] != 0)
            def _wait():
                fill_copy(c).wait()

    base = step * tp

    def issue(r, carry):
        d = dest_ref[base + r]
        pltpu.make_async_copy(x_ref.at[pl.ds(r, 1)], xs_hbm.at[pl.ds(d, 1)], sem.at[0]).start()
        return carry

    lax.fori_loop(0, tp, issue, 0, unroll=8)
    pltpu.make_async_copy(x_ref, xs_hbm.at[pl.ds(0, tp)], sem.at[0]).wait()


def _permute(x1e, dest, pad_start, pad_valid, tp=512):
    s = x1e.shape[0]
    return pl.pallas_call(
        _permute_kernel,
        grid_spec=pltpu.PrefetchScalarGridSpec(
            num_scalar_prefetch=3,
            grid=(s // tp,),
            in_specs=[pl.BlockSpec((tp, X1E_W), lambda i, d, ps, pv: (i, 0))],
            out_specs=pl.BlockSpec(memory_space=pl.ANY),
            scratch_shapes=[pltpu.VMEM((MOE_TILE, X1E_W), f32), pltpu.SemaphoreType.DMA((2,))],
        ),
        out_shape=jax.ShapeDtypeStruct((_n_tiles(s) * MOE_TILE, X1E_W), f32),
        compiler_params=pltpu.CompilerParams(
            dimension_semantics=("arbitrary",), vmem_limit_bytes=VMEM_LIMIT),
        name="moe_permute",
    )(dest, pad_start, pad_valid, x1e)


def _moe_kernel(nact_ref, elo_ref, ehi_ref, xs_ref, wg0_ref, wu0_ref, wd0_ref, wg1_ref, wu1_ref, wd1_ref,
                ys_ref):
    @pl.when(pl.program_id(0) < nact_ref[0])
    def _tile():
        xb = xs_ref[:, :D_MODEL].astype(bf16)
        acc = None
        for col, (wg_ref, wu_ref, wd_ref) in enumerate(((wg0_ref, wu0_ref, wd0_ref),
                                                        (wg1_ref, wu1_ref, wd1_ref))):
            gate = jnp.dot(xb, wg_ref[0], preferred_element_type=f32)
            up = jnp.dot(xb, wu_ref[0], preferred_element_type=f32)
            hid = (gate * _sigmoid(gate)) * up * xs_ref[:, D_MODEL + col:D_MODEL + col + 1]
            part = jnp.dot(hid.astype(bf16), wd_ref[0], preferred_element_type=f32)
            acc = part if acc is None else acc + part
        ys_ref[...] = acc

    @pl.when(pl.program_id(0) >= nact_ref[0])
    def _unused_tile():
        ys_ref[...] = jnp.zeros_like(ys_ref)


def _moe(xs, n_active, tile_lo, tile_hi, w_gate, w_up, w_down):
    n_tiles = xs.shape[0] // MOE_TILE
    tile = lambda j, na, lo, hi: (jnp.minimum(j, na[0] - 1), 0)
    low = lambda j, na, lo, hi: (lo[j], 0, 0)
    high = lambda j, na, lo, hi: (hi[j], 0, 0)
    up_shape = (1, D_MODEL, D_FF_EXPERT)
    down_shape = (1, D_FF_EXPERT, D_MODEL)
    return pl.pallas_call(
        _moe_kernel,
        grid_spec=pltpu.PrefetchScalarGridSpec(
            num_scalar_prefetch=3,
            grid=(n_tiles,),
            in_specs=[
                pl.BlockSpec((MOE_TILE, X1E_W), tile),
                pl.BlockSpec(up_shape, low), pl.BlockSpec(up_shape, low), pl.BlockSpec(down_shape, low),
                pl.BlockSpec(up_shape, high), pl.BlockSpec(up_shape, high), pl.BlockSpec(down_shape, high),
            ],
            out_specs=pl.BlockSpec((MOE_TILE, D_MODEL), lambda j, na, lo, hi: (j, 0)),
        ),
        out_shape=jax.ShapeDtypeStruct((n_tiles * MOE_TILE, D_MODEL), f32),
        compiler_params=pltpu.CompilerParams(
            dimension_semantics=("arbitrary",), vmem_limit_bytes=VMEM_LIMIT),
        name="moe_ffn",
    )(n_active, tile_lo, tile_hi, xs, w_gate, w_up, w_down, w_gate, w_up, w_down)


def _unpermute_kernel(dest_ref, x1_ref, ys_hbm, lg_ref, lb_ref, out_ref, outb_ref, ybuf, sem):
    i = pl.program_id(0)
    n = pl.num_programs(0)
    tu = x1_ref.shape[0]

    def gather(tile, slot):
        base = tile * tu

        def issue(r, carry):
            d = dest_ref[base + r]
            pltpu.make_async_copy(ys_hbm.at[pl.ds(d, 1)], ybuf.at[slot, pl.ds(r, 1)], sem.at[slot]).start()
            return carry

        lax.fori_loop(0, tu, issue, 0, unroll=8)

    slot = lax.rem(i, 2)

    @pl.when(i == 0)
    def _first():
        gather(0, 0)

    @pl.when(i + 1 < n)
    def _next():
        gather(i + 1, 1 - slot)

    pltpu.make_async_copy(ys_hbm.at[pl.ds(0, tu)], ybuf.at[slot], sem.at[slot]).wait()
    y = _layer_norm(DEEPNORM_ALPHA * x1_ref[...] + ybuf[slot], lg_ref[...], lb_ref[...])
    out_ref[...] = y
    outb_ref[...] = y.astype(bf16)


def _unpermute(x1e, ys, dest, ln_g, ln_b, tu=256):
    s = x1e.shape[0]
    row = lambda i, d: (i, 0)
    fixed = lambda i, d: (0, 0)
    return pl.pallas_call(
        _unpermute_kernel,
        grid_spec=pltpu.PrefetchScalarGridSpec(
            num_scalar_prefetch=1,
            grid=(s // tu,),
            in_specs=[
                pl.BlockSpec((tu, D_MODEL), row),
                pl.BlockSpec(memory_space=pl.ANY),
                pl.BlockSpec((1, D_MODEL), fixed),
                pl.BlockSpec((1, D_MODEL), fixed),
            ],
            out_specs=[pl.BlockSpec((tu, D_MODEL), row), pl.BlockSpec((tu, D_MODEL), row)],
            scratch_shapes=[pltpu.VMEM((2, tu, D_MODEL), f32), pltpu.SemaphoreType.DMA((2,))],
        ),
        out_shape=[jax.ShapeDtypeStruct((s, D_MODEL), f32), jax.ShapeDtypeStruct((s, D_MODEL), bf16)],
        compiler_params=pltpu.CompilerParams(
            dimension_semantics=("arbitrary",), vmem_limit_bytes=VMEM_LIMIT),
        name="moe_unpermute_ln2",
    )(dest, x1e, ys, ln_g, ln_b)


def _pad_lanes(a, width=LANES):
    return jnp.pad(a, ((0, 0), (0, width - a.shape[1])))


def kernel(x, w_in, conv_w, dn_a_log, dn_dt_bias, dn_norm_w, p_a, p_b, w_out, ln1_g, ln1_b, w_router, router_bias, w_gate, w_up, w_down, ln2_g, ln2_b):
    bsz, s, _ = x.shape
    assert bsz == 1 and s % 512 == 0
    xf = x[0]
    xb = xf

    c_ba = 4 * DN_WIDTH
    c_sb = c_ba + 2 * DN_HEADS
    c_gate = c_sb + SB_SLAB
    wr_pad = _pad_lanes(w_router)
    wr_hi = wr_pad.astype(bf16)
    wr_cat = jnp.concatenate([wr_hi, (wr_pad - wr_hi.astype(f32)).astype(bf16)], axis=1)
    rb_pad = jnp.pad(router_bias[None, :], ((0, 0), (0, LANES - N_EXPERTS)), constant_values=-jnp.inf)
    head_pad = ((0, 0), (DN_HEADS, LANES - 2 * DN_HEADS))
    col = jnp.arange(w_in.shape[-1])
    q_scale = jnp.where(jnp.logical_and(col >= c_sb, col < c_sb + SB_WIDTH), SB_DH ** -0.5, 1.0).astype(f32)
    w_in16 = (w_in * q_scale).astype(bf16)
    wg16, wu16, wd16 = w_gate.astype(bf16), w_up.astype(bf16), w_down.astype(bf16)

    for l in range(DEPTH):
        w = w_in16[l]
        w_dn = jnp.concatenate([w[:, :c_ba], _pad_lanes(w[:, c_ba:c_sb])], axis=1)
        w_sb = w[:, c_sb:c_gate]
        w_g = w[:, c_gate:]

        p_dn, p_sb, gates = _proj(xb, w_dn, w_sb, w_g)
        o_a = _deltanet(p_dn,
                        jnp.pad(conv_w[l], ((0, 8 - CONV_K), (0, 0))),
                        jnp.pad(dn_a_log[l][None, :], head_pad),
                        jnp.pad(dn_dt_bias[l][None, :], head_pad),
                        dn_norm_w[l][None, :])
        o_b = _sb_attention(p_sb)
        x1e, route, cnt = _merge(xf, o_a, o_b, gates,
                                p_a[l].astype(bf16), p_b[l].astype(bf16), w_out[l].astype(bf16),
                                ln1_g[l][None, :], ln1_b[l][None, :], wr_cat, rb_pad)
        dest, n_active, tile_lo, tile_hi, pad_start, pad_valid = _route_tables(route, cnt, s)
        xs = _permute(x1e, dest, pad_start, pad_valid)
        ys = _moe(xs, n_active, tile_lo, tile_hi,
                  wg16[l], wu16[l], wd16[l])
        xf, xb = _unpermute(x1e, ys, dest, ln2_g[l][None, :], ln2_b[l][None, :])
    return xf[None]
```

```python
import jax
import jax.numpy as jnp
from jax import lax
from jax.experimental import pallas as pl
from jax.experimental.pallas import tpu as pltpu

f32 = jnp.float32
bf16 = jnp.bfloat16
HIGHEST = lax.Precision.HIGHEST

D_MODEL = 1024
DEPTH = 2
CHUNK = 64
DN_HEADS = 4
DN_DK = 128
DN_DV = 128
CONV_K = 4
SB_HEADS = 8
SB_DH = 64
SB_BLOCK = 128
N_EXPERTS = 16
N_GROUPS = 4
EXPERTS_PER_GROUP = N_EXPERTS // N_GROUPS
D_FF_EXPERT = 512
LN_EPS = 1e-5
RMS_EPS = 1e-6
DEEPNORM_ALPHA = (2 * DEPTH) ** 0.25

DN_WIDTH = DN_HEADS * DN_DK
SB_WIDTH = SB_HEADS * SB_DH
CONV_CH = 3 * DN_WIDTH
LANES = 128
SUBLANES = 8
DN_SLAB = 4 * DN_WIDTH + LANES
BA_COL = 4 * DN_WIDTH
SB_SLAB = 3 * SB_WIDTH
GATE_SLAB = 2 * D_MODEL

SB_LOG_ZERO = -88.0
SB_STATIC_BLOCKS = 2
SB_MASK_PENALTY = -1e30

VMEM_LIMIT = 48 * 1024 * 1024


def _sigmoid(x):
    return 1.0 / (1.0 + jnp.exp(-x))


def _softplus(x):
    return jnp.maximum(x, 0.0) + jnp.log1p(jnp.exp(-jnp.abs(x)))


def _nt_dot(a, b):
    return lax.dot_general(a, b, (((1,), (1,)), ((), ())), preferred_element_type=f32)


def _tn_dot(a, b):
    return lax.dot_general(a, b, (((0,), (0,)), ((), ())), preferred_element_type=f32)


def _dot_exact(a, b):
    return jnp.dot(a, b, preferred_element_type=f32, precision=HIGHEST)


def _layer_norm(y, g, b):
    mu = jnp.mean(y, axis=-1, keepdims=True)
    d = y - mu
    var = jnp.mean(d * d, axis=-1, keepdims=True)
    return d * lax.rsqrt(var + LN_EPS) * g + b


def _proj_kernel(x_ref, wdn_ref, wsb_ref, wg_ref, odn_ref, osb_ref, og_ref):
    x = x_ref[...].astype(bf16)
    odn_ref[...] = jnp.dot(x, wdn_ref[...], preferred_element_type=f32)
    osb_ref[...] = jnp.dot(x, wsb_ref[...], preferred_element_type=f32).astype(bf16)
    og_ref[...] = jnp.dot(x, wg_ref[...], preferred_element_type=f32).astype(bf16)


def _proj(x, w_dn, w_sb, w_g, tm=256):
    s = x.shape[0]
    row = lambda i: (i, 0)
    fixed = lambda i: (0, 0)
    return pl.pallas_call(
        _proj_kernel,
        grid=(s // tm,),
        in_specs=[
            pl.BlockSpec((tm, D_MODEL), row),
            pl.BlockSpec((D_MODEL, DN_SLAB), fixed),
            pl.BlockSpec((D_MODEL, SB_SLAB), fixed),
            pl.BlockSpec((D_MODEL, GATE_SLAB), fixed),
        ],
        out_specs=[
            pl.BlockSpec((tm, DN_SLAB), row),
            pl.BlockSpec((tm, SB_SLAB), row),
            pl.BlockSpec((tm, GATE_SLAB), row),
        ],
        out_shape=[
            jax.ShapeDtypeStruct((s, DN_SLAB), f32),
            jax.ShapeDtypeStruct((s, SB_SLAB), bf16),
            jax.ShapeDtypeStruct((s, GATE_SLAB), bf16),
        ],
        compiler_params=pltpu.CompilerParams(
            dimension_semantics=("arbitrary",), vmem_limit_bytes=VMEM_LIMIT),
        name="proj",
    )(x, w_dn, w_sb, w_g)


DN_ROWS = 512
PAIR = 2 * CHUNK
TAIL = 8


def _split2(x):
    hi = x.astype(bf16)
    return hi, (x - hi.astype(f32)).astype(bf16)


def _split3(x):
    hi = x.astype(bf16)
    r = x - hi.astype(f32)
    mid = r.astype(bf16)
    return hi, mid, (r - mid.astype(f32)).astype(bf16)


def _dn_kernel(p_ref, cw_ref, alog_ref, dtb_ref, nw_ref, o_ref, state_ref, tail_ref, xe_ref):
    step = pl.program_id(0)
    rows = p_ref.shape[0]
    n_pb = rows // PAIR

    @pl.when(step == 0)
    def _init():
        state_ref[...] = jnp.zeros_like(state_ref)
        tail_ref[...] = jnp.zeros_like(tail_ref)

    xe_ref[0:TAIL, :] = tail_ref[...]
    xe_ref[TAIL:TAIL + rows, :] = p_ref[:, 0:CONV_CH]
    tail_ref[...] = p_ref[rows - TAIL:rows, 0:CONV_CH]

    row128 = lax.broadcasted_iota(jnp.int32, (PAIR, LANES), 0)
    lane128 = lax.broadcasted_iota(jnp.int32, (PAIR, LANES), 1)
    same_chunk = (row128 >= CHUNK) == (lane128 >= CHUNK)
    tril_bd = jnp.where(jnp.logical_and(row128 >= lane128, same_chunk), 1.0, 0.0).astype(bf16)
    triu_bd = jnp.where(jnp.logical_and(row128 <= lane128, same_chunk), 1.0, 0.0).astype(bf16)
    first_rows = row128 < CHUNK
    row64 = lax.broadcasted_iota(jnp.int32, (CHUNK, LANES), 0)
    lane64 = lax.broadcasted_iota(jnp.int32, (CHUNK, LANES), 1)
    left = lane64 < CHUNK
    col_in_chunk = jnp.bitwise_and(lane64, CHUNK - 1)
    tri_p = row64 >= col_in_chunk
    strict_p = row64 > col_in_chunk
    eye_p = jnp.where(row64 == col_in_chunk, 1.0, 0.0).astype(f32)

    def block_diag(z):
        zero = jnp.zeros_like(z)
        return jnp.concatenate([jnp.where(left, z, zero), jnp.where(left, zero, z)], axis=0)

    def pair_matmul(y_hi, y_lo, zbd_hi, zbd_lo):
        return (jnp.dot(jnp.concatenate([y_hi, y_lo], axis=1), jnp.concatenate([zbd_hi, zbd_hi], axis=0),
                        preferred_element_type=f32)
                + jnp.dot(y_hi, zbd_lo, preferred_element_type=f32))

    nw = nw_ref[...]
    heads = range(DN_HEADS)
    base = TAIL - (CONV_K - 1)

    def pair_block(pb):
        r0 = pb * PAIR
        ba = p_ref[r0:r0 + PAIR, BA_COL:BA_COL + LANES]
        beta_all = _sigmoid(ba)
        parts = _split3(-jnp.exp(alog_ref[...]) * _softplus(ba + dtb_ref[...]))
        gcol = sum(jnp.dot(tril_bd, pt, preferred_element_type=f32) for pt in parts)
        grow = sum(_tn_dot(pt, triu_bd) for pt in parts)
        yield

        qkv = []
        for grp in range(3):
            outs = []
            for h in heads:
                col = grp * DN_WIDTH + h * DN_DK
                acc = xe_ref[base + r0:base + r0 + PAIR, col:col + DN_DK] * cw_ref[0:1, col:col + DN_DK]
                for j in range(1, CONV_K):
                    acc = acc + (xe_ref[base + r0 + j:base + r0 + j + PAIR, col:col + DN_DK]
                                 * cw_ref[j:j + 1, col:col + DN_DK])
                y = acc * _sigmoid(acc)
                if grp < 2:
                    y = y * lax.rsqrt(jnp.sum(y * y, axis=-1, keepdims=True) + RMS_EPS)
                outs.append(y * (DN_DK ** -0.5) if grp == 0 else y)
                yield
            qkv.append(outs)
        qs, ks, vs = qkv

        kbetas, qgs, kdecs, rstacks, decays, egl = [], [], [], [], [], []
        for h in heads:
            gc = jnp.broadcast_to(gcol[:, DN_HEADS + h:DN_HEADS + h + 1], (PAIR, LANES))
            beta = jnp.broadcast_to(beta_all[:, h:h + 1], (PAIR, LANES))
            eg = jnp.exp(gc)
            glast = jnp.where(first_rows, gc[CHUNK - 1:CHUNK, :], gc[PAIR - 1:PAIR, :])
            kbeta = ks[h] * beta
            kbetas.append(kbeta)
            qgs.append(qs[h] * eg)
            kdecs.append((ks[h] * jnp.exp(glast - gc)).astype(bf16))
            rstacks.append(jnp.concatenate([kbeta * eg, vs[h] * beta], axis=1).astype(bf16))
            gdiff = jnp.where(left, gc[:CHUNK], gc[CHUNK:]) - grow[DN_HEADS + h:DN_HEADS + h + 1, :]
            decays.append(jnp.where(tri_p, jnp.exp(jnp.where(tri_p, gdiff, 0.0)), 0.0))
            egl.append((jnp.exp(gc[CHUNK - 1:CHUNK, :]), jnp.exp(gc[PAIR - 1:PAIR, :])))
            yield

        lps, qkms = [], []
        for h in heads:
            kk = _nt_dot(jnp.concatenate([kbetas[h], qs[h]], axis=0).astype(bf16), ks[h].astype(bf16))
            lps.append(jnp.where(strict_p, jnp.where(left, kk[0:CHUNK], kk[CHUNK:PAIR]) * decays[h], 0.0))
            qkms.append(jnp.where(tri_p, jnp.where(left, kk[PAIR:PAIR + CHUNK], kk[PAIR + CHUNK:]) * decays[h], 0.0)
                        .astype(bf16))
            yield

        pw = [_split2(lp) for lp in lps]
        pw_bd = [(block_diag(hi), block_diag(lo)) for hi, lo in pw]
        ts = [eye_p - lp for lp in lps]
        for _ in range(5):
            pw = [_split2(pair_matmul(*pw[h], *pw_bd[h])) for h in heads]
            pw_bd = [(block_diag(hi), block_diag(lo)) for hi, lo in pw]
            yield
            ts = [ts[h] + pair_matmul(*_split2(ts[h]), *pw_bd[h]) for h in heads]
            yield

        zero_p = jnp.zeros((CHUNK, LANES), bf16)
        mq, bo = [], []
        for h in heads:
            t16 = ts[h].astype(bf16)
            halves = (jnp.where(left, t16, zero_p), jnp.where(left, zero_p, t16))
            qk_halves = (jnp.where(left, qkms[h], zero_p), jnp.where(left, zero_p, qkms[h]))
            wus = [jnp.dot(th, rstacks[h], preferred_element_type=f32).astype(bf16) for th in halves]
            wu_stack = jnp.concatenate(wus, axis=0)
            mq_h, bo_h = [], []
            for c in range(2):
                cs = slice(c * CHUNK, (c + 1) * CHUNK)
                kw = _tn_dot(kdecs[h][cs], wus[c])
                qw = jnp.dot(qk_halves[c], wu_stack, preferred_element_type=f32)
                mq_h.append(jnp.concatenate([kw[:, :DN_DK], qgs[h][cs] - qw[:, :DN_DK]], axis=0).astype(bf16))
                bo_h.append((kw[:, DN_DK:], qw[:, DN_DK:]))
            mq.append(mq_h)
            bo.append(bo_h)
            yield

        prepared[pb] = (mq, bo, egl)

    prepared = [None] * n_pb
    pipes = [pair_block(pb) for pb in range(n_pb)]
    while pipes:
        pipes = [pipe for pipe in pipes if next(pipe, "done") != "done"]

    states = [state_ref[h] for h in heads]
    for pb in range(n_pb):
        mq, bo, egl = prepared[pb]
        for c in range(2):
            for h in heads:
                ms = jnp.dot(mq[h][c], states[h].astype(bf16), preferred_element_type=f32)
                b_c, o_c = bo[h][c]
                o = ms[DN_DK:] + o_c
                states[h] = states[h] * egl[h][c] - ms[:DN_DK] + b_c
                o = o * lax.rsqrt(jnp.mean(o * o, axis=-1, keepdims=True) + RMS_EPS) * nw
                rc = pb * PAIR + c * CHUNK
                z = p_ref[rc:rc + CHUNK, 3 * DN_WIDTH + h * DN_DV:3 * DN_WIDTH + (h + 1) * DN_DV]
                o_ref[rc:rc + CHUNK, h * DN_DV:(h + 1) * DN_DV] = o * (z * _sigmoid(z))
    for h in heads:
        state_ref[h] = states[h]


def _deltanet(p_dn, conv_w8, alog_row, dtb_row, nw_row):
    s = p_dn.shape[0]
    fixed = lambda i: (0, 0)
    return pl.pallas_call(
        _dn_kernel,
        grid=(s // DN_ROWS,),
        in_specs=[
            pl.BlockSpec((DN_ROWS, DN_SLAB), lambda i: (i, 0)),
            pl.BlockSpec((8, CONV_CH), fixed),
            pl.BlockSpec((1, LANES), fixed),
            pl.BlockSpec((1, LANES), fixed),
            pl.BlockSpec((1, DN_DV), fixed),
        ],
        out_specs=pl.BlockSpec((DN_ROWS, DN_HEADS * DN_DV), lambda i: (i, 0)),
        out_shape=jax.ShapeDtypeStruct((s, DN_HEADS * DN_DV), f32),
        scratch_shapes=[
            pltpu.VMEM((DN_HEADS, DN_DK, DN_DV), f32),
            pltpu.VMEM((TAIL, CONV_CH), f32),
            pltpu.VMEM((TAIL + DN_ROWS, CONV_CH), f32),
        ],
        compiler_params=pltpu.CompilerParams(
            dimension_semantics=("arbitrary",), vmem_limit_bytes=VMEM_LIMIT),
        name="deltanet",
    )(p_dn, conv_w8, alog_row, dtb_row, nw_row)


N_PAIRS = SB_HEADS // 2


def _sb_kernel(q_ref, kd_ref, k1_ref, vd_ref, v1_ref, kv_hbm, o_ref, kbuf, vbuf, sem):
    qb = pl.program_id(0)
    blk = SB_BLOCK
    row = lax.broadcasted_iota(jnp.int32, (blk, blk), 0)
    lane = lax.broadcasted_iota(jnp.int32, (blk, blk), 1)
    diag_mask = row > lane
    even = lane < SB_DH
    suffix = jnp.where(row > lane, 1.0, 0.0).astype(bf16)
    suffix2 = jnp.concatenate([suffix, suffix], axis=0)

    def split_heads(x):
        zero = jnp.zeros_like(x)
        return jnp.where(even, x, zero), jnp.where(even, zero, x)

    def suffix_sums(spm):
        hi = spm.astype(bf16)
        lo = (spm - hi.astype(f32)).astype(bf16)
        return jnp.dot(jnp.concatenate([hi, lo], axis=1), suffix2, preferred_element_type=f32)

    pairs = [slice(p * LANES, (p + 1) * LANES) for p in range(N_PAIRS)]
    q_heads = [split_heads(q_ref[:, ps]) for ps in pairs]
    units = [(p, b, hh) for b in range(SB_STATIC_BLOCKS) for p in range(N_PAIRS) for hh in range(2)]
    k_refs = (kd_ref, k1_ref)
    v_refs = (vd_ref, v1_ref)

    z = {u: _nt_dot(q_heads[u[0]][u[2]], k_refs[u[1]][:, pairs[u[0]]]) for u in units}
    sp = {u: _softplus(z[u]) for u in units}
    spm = {u: (jnp.where(diag_mask, sp[u], 0.0) if u[1] == 0 else sp[u]) for u in units}
    logw = {u: z[u] - sp[u] - suffix_sums(spm[u]) for u in units}
    keep = {u: jnp.sum(spm[u], axis=1, keepdims=True) for u in units}
    pen1 = jnp.where(qb >= 1, 0.0, -SB_MASK_PENALTY).astype(f32)
    att = {}
    for p, b, hh in units:
        if b == 0:
            att[p, b, hh] = jnp.where(diag_mask, jnp.exp(logw[p, b, hh]), 0.0).astype(bf16)
        else:
            att[p, b, hh] = jnp.exp(logw[p, b, hh] - (keep[p, 0, hh] + pen1)).astype(bf16)
    accs = []
    for p in range(N_PAIRS):
        vals = jnp.concatenate([h for b in range(SB_STATIC_BLOCKS) for h in split_heads(v_refs[b][:, pairs[p]])],
                               axis=0)
        lhs = jnp.concatenate([att[p, b, hh] for b in range(SB_STATIC_BLOCKS) for hh in range(2)], axis=1)
        accs.append(jnp.dot(lhs, vals, preferred_element_type=f32))
    carries = [keep[p, 0, hh] + keep[p, 1, hh] for p in range(N_PAIRS) for hh in range(2)]

    def live(cr):
        m = jnp.min(cr[0])
        for c in cr[1:]:
            m = jnp.minimum(m, jnp.min(c))
        return m <= -SB_LOG_ZERO

    def cond(st):
        kb, more = st[0], st[1]
        return jnp.logical_and(kb >= 0, more)

    def body(st):
        kb = st[0]
        acc_l = list(st[2:2 + N_PAIRS])
        car_l = list(st[2 + N_PAIRS:])
        start = pl.multiple_of(kb * blk, blk)
        cpk = pltpu.make_async_copy(kv_hbm.at[pl.ds(start, blk), pl.ds(SB_WIDTH, SB_WIDTH)], kbuf, sem.at[0])
        cpv = pltpu.make_async_copy(kv_hbm.at[pl.ds(start, blk), pl.ds(2 * SB_WIDTH, SB_WIDTH)], vbuf, sem.at[1])
        cpk.start()
        cpv.start()
        cpk.wait()
        cpv.wait()
        new_car = []
        for p, ps in enumerate(pairs):
            atts = []
            for hh, qh in enumerate(split_heads(q_ref[:, ps])):
                zz = _nt_dot(qh, kbuf[:, ps])
                spp = _softplus(zz)
                carry = car_l[2 * p + hh]
                atts.append(jnp.exp(zz - spp - suffix_sums(spp) - carry).astype(bf16))
                new_car.append(carry + jnp.sum(spp, axis=1, keepdims=True))
            vals = jnp.concatenate(split_heads(vbuf[:, ps]), axis=0)
            acc_l[p] = acc_l[p] + jnp.dot(jnp.concatenate(atts, axis=1), vals, preferred_element_type=f32)
        return (kb - 1, live(new_car), *acc_l, *new_car)

    st = lax.while_loop(cond, body, (qb - SB_STATIC_BLOCKS, live(carries), *accs, *carries))
    for p, ps in enumerate(pairs):
        o_ref[:, ps] = st[2 + p]


def _sb_attention(p_sb):
    s = p_sb.shape[0]
    blk = SB_BLOCK

    def spec(col, back):
        return pl.BlockSpec((blk, SB_WIDTH), lambda i: (jnp.maximum(i - back, 0), col))

    return pl.pallas_call(
        _sb_kernel,
        grid=(s // blk,),
        in_specs=[spec(0, 0), spec(1, 0), spec(1, 1), spec(2, 0), spec(2, 1),
                  pl.BlockSpec(memory_space=pl.ANY)],
        out_specs=pl.BlockSpec((blk, SB_WIDTH), lambda i: (i, 0)),
        out_shape=jax.ShapeDtypeStruct((s, SB_WIDTH), f32),
        scratch_shapes=[
            pltpu.VMEM((blk, SB_WIDTH), bf16),
            pltpu.VMEM((blk, SB_WIDTH), bf16),
            pltpu.SemaphoreType.DMA((2,)),
        ],
        compiler_params=pltpu.CompilerParams(
            dimension_semantics=("arbitrary",), vmem_limit_bytes=VMEM_LIMIT),
        name="sb_attention",
    )(p_sb, p_sb, p_sb, p_sb, p_sb, p_sb)


MOE_TILE = 256
PAIRS_PER_GROUP = 6
N_CLASSES = N_GROUPS * PAIRS_PER_GROUP
X1E_W = D_MODEL + LANES
MERGE_ROWS = 512
MERGE_PARTS = 2


def _route(aff, sel, lane):
    grp = lax.shift_right_logical(lane, 2)
    neg = -jnp.inf
    best = None
    for gidx in range(N_GROUPS):
        sg = jnp.where(grp == gidx, sel, neg)
        m1 = jnp.max(sg, axis=1, keepdims=True)
        i1 = jnp.min(jnp.where(sg == m1, lane, LANES), axis=1, keepdims=True)
        sg2 = jnp.where(lane == i1, neg, sg)
        m2 = jnp.max(sg2, axis=1, keepdims=True)
        i2 = jnp.min(jnp.where(sg2 == m2, lane, LANES), axis=1, keepdims=True)
        score = m1 + m2
        if best is None:
            best = (score, i1, i2)
        else:
            better = score > best[0]
            best = (jnp.where(better, score, best[0]),
                    jnp.where(better, i1, best[1]),
                    jnp.where(better, i2, best[2]))
    _, i1, i2 = best
    w1 = jnp.sum(jnp.where(lane == i1, aff, 0.0), axis=1, keepdims=True)
    w2 = jnp.sum(jnp.where(lane == i2, aff, 0.0), axis=1, keepdims=True)
    denom = w1 + w2
    w1, w2 = w1 / denom, w2 / denom
    first_low = i1 < i2
    e_lo = jnp.minimum(i1, i2)
    e_hi = jnp.maximum(i1, i2)
    a = jnp.bitwise_and(e_lo, EXPERTS_PER_GROUP - 1)
    b = jnp.bitwise_and(e_hi, EXPERTS_PER_GROUP - 1)
    pair = jnp.where(a == 0, 0, jnp.where(a == 1, 3, 5)) + (b - a - 1)
    cls = lax.shift_right_logical(e_lo, 2) * PAIRS_PER_GROUP + pair
    return cls, jnp.where(first_low, w1, w2), jnp.where(first_low, w2, w1)


def _merge_kernel(x_ref, oa_ref, ob_ref, g_ref, pa_ref, pb_ref, wo_ref, lg_ref, lb_ref, wr_ref, rb_ref,
                  x1e_ref, route_ref, cnt_ref, run_ref):
    step = pl.program_id(0)

    @pl.when(step == 0)
    def _init():
        run_ref[...] = jnp.zeros_like(run_ref)

    rows = x_ref.shape[0] // MERGE_PARTS
    parts = [slice(i * rows, (i + 1) * rows) for i in range(MERGE_PARTS)]
    a = [jnp.dot(oa_ref[p, :].astype(bf16), pa_ref[...], preferred_element_type=f32) for p in parts]
    b = [jnp.dot(ob_ref[p, :].astype(bf16), pb_ref[...], preferred_element_type=f32) for p in parts]
    merged = [(_sigmoid(g_ref[p, :D_MODEL].astype(f32)) * ai
               + _sigmoid(g_ref[p, D_MODEL:].astype(f32)) * bi).astype(bf16)
              for p, ai, bi in zip(parts, a, b)]
    mix = [jnp.dot(m, wo_ref[...], preferred_element_type=f32) for m in merged]
    x1 = [_layer_norm(DEEPNORM_ALPHA * x_ref[p, :] + mi, lg_ref[...], lb_ref[...]) for p, mi in zip(parts, mix)]
    x1b = [xi.astype(bf16) for xi in x1]
    for p, xi in zip(parts, x1):
        x1e_ref[p, :D_MODEL] = xi

    x_lo = [(xi - xbi.astype(f32)).astype(bf16) for xi, xbi in zip(x1, x1b)]
    t = [jnp.dot(xbi, wr_ref[...], preferred_element_type=f32) for xbi in x1b]
    logits = [ti[:, :LANES] + ti[:, LANES:] + jnp.dot(xl, wr_ref[:, :LANES], preferred_element_type=f32)
              for ti, xl in zip(t, x_lo)]
    lane = lax.broadcasted_iota(jnp.int32, (rows, LANES), 1)
    aff = [_sigmoid(lg) for lg in logits]
    routed = [_route(af, af + rb_ref[...], lane) for af in aff]

    rr = lax.broadcasted_iota(jnp.int32, (rows, rows), 0)
    cc = lax.broadcasted_iota(jnp.int32, (rows, rows), 1)
    tril = jnp.where(rr >= cc, 1.0, 0.0).astype(bf16)
    pick = jnp.where(lax.broadcasted_iota(jnp.int32, (8, LANES), 0)
                     == lax.broadcasted_iota(jnp.int32, (8, LANES), 1), 1.0, 0.0).astype(bf16)
    for p, (cls, w_lo, w_hi) in zip(parts, routed):
        onehot = lane == cls
        prefix = jnp.dot(tril, jnp.where(onehot, 1.0, 0.0).astype(bf16), preferred_element_type=f32)
        run = run_ref[...]
        rank = jnp.sum(jnp.where(onehot, prefix + run, 0.0), axis=1, keepdims=True) - 1.0
        run_ref[...] = run + prefix[rows - 1:rows, :]
        x1e_ref[p, D_MODEL:] = jnp.where(lane == 0, w_lo, jnp.where(lane == 1, w_hi, 0.0))
        rank_i = rank.astype(jnp.int32)
        digits = jnp.where(lane == 0, cls, jnp.where(lane == 1, lax.shift_right_logical(rank_i, 7),
                                                     jnp.where(lane == 2, jnp.bitwise_and(rank_i, LANES - 1), 0)))
        route_ref[0, :, p] = _nt_dot(pick, digits.astype(f32).astype(bf16)).astype(jnp.int32)
    cnt_ref[...] = jnp.broadcast_to(run_ref[...], cnt_ref.shape).astype(jnp.int32)


def _merge(x, o_a, o_b, gates, p_a, p_b, w_out, ln_g, ln_b, wr_cat, r_bias):
    s = x.shape[0]
    tm = MERGE_ROWS
    row = lambda i: (i, 0)
    fixed = lambda i: (0, 0)
    return pl.pallas_call(
        _merge_kernel,
        grid=(s // tm,),
        in_specs=[
            pl.BlockSpec((tm, D_MODEL), row),
            pl.BlockSpec((tm, DN_HEADS * DN_DV), row),
            pl.BlockSpec((tm, SB_WIDTH), row),
            pl.BlockSpec((tm, GATE_SLAB), row),
            pl.BlockSpec((DN_HEADS * DN_DV, D_MODEL), fixed),
            pl.BlockSpec((SB_WIDTH, D_MODEL), fixed),
            pl.BlockSpec((D_MODEL, D_MODEL), fixed),
            pl.BlockSpec((1, D_MODEL), fixed),
            pl.BlockSpec((1, D_MODEL), fixed),
            pl.BlockSpec((D_MODEL, 2 * LANES), fixed),
            pl.BlockSpec((1, LANES), fixed),
        ],
        out_specs=[
            pl.BlockSpec((tm, X1E_W), row),
            pl.BlockSpec((1, 8, tm), lambda i: (i, 0, 0)),
            pl.BlockSpec((8, LANES), fixed),
        ],
        out_shape=[
            jax.ShapeDtypeStruct((s, X1E_W), f32),
            jax.ShapeDtypeStruct((s // tm, 8, tm), jnp.int32),
            jax.ShapeDtypeStruct((8, LANES), jnp.int32),
        ],
        scratch_shapes=[pltpu.VMEM((1, LANES), f32)],
        compiler_params=pltpu.CompilerParams(
            dimension_semantics=("arbitrary",), vmem_limit_bytes=VMEM_LIMIT),
        name="merge_router",
    )(x, o_a, o_b, gates, p_a, p_b, w_out, ln_g, ln_b, wr_cat, r_bias)


def _class_experts():
    lo, hi = [], []
    for g in range(N_GROUPS):
        for a in range(EXPERTS_PER_GROUP):
            for b in range(a + 1, EXPERTS_PER_GROUP):
                lo.append(g * EXPERTS_PER_GROUP + a)
                hi.append(g * EXPERTS_PER_GROUP + b)
    return jnp.array(lo, jnp.int32), jnp.array(hi, jnp.int32)


def _n_tiles(s):
    return -(-(s + N_CLASSES * (MOE_TILE - 1)) // MOE_TILE)


def _route_tables(route, cnt, s):
    counts = cnt[0, :N_CLASSES]
    padded = (counts + (MOE_TILE - 1)) // MOE_TILE * MOE_TILE
    ends = jnp.cumsum(padded)
    offs = ends - padded
    cls, rank_hi, rank_lo = (route[:, r, :].reshape(s) for r in range(3))
    dest = offs[cls] + rank_hi * LANES + rank_lo
    n_active = (ends[-1] // MOE_TILE).astype(jnp.int32)[None]
    tile_row = jnp.minimum(jnp.arange(_n_tiles(s), dtype=jnp.int32) * MOE_TILE, ends[-1] - 1)
    tile_cls = jnp.minimum(jnp.sum(tile_row[:, None] >= ends[None, :], axis=1), N_CLASSES - 1)
    e_lo, e_hi = _class_experts()
    tail = jnp.arange(s // MOE_TILE, _n_tiles(s), dtype=jnp.int32)
    pad_start = jnp.concatenate([ends - MOE_TILE, tail * MOE_TILE]).astype(jnp.int32)
    pad_valid = jnp.concatenate([padded > 0, tail >= n_active[0]]).astype(jnp.int32)
    return (dest.astype(jnp.int32), n_active, e_lo[tile_cls], e_hi[tile_cls], pad_start, pad_valid)


def _permute_kernel(dest_ref, pstart_ref, pvalid_ref, x_ref, xs_hbm, zero_ref, sem):
    step = pl.program_id(0)
    tp = x_ref.shape[0]

    @pl.when(step == 0)
    def _fill():
        zero_ref[...] = jnp.zeros_like(zero_ref)

        def fill_copy(c):
            start = pl.multiple_of(pstart_ref[c], MOE_TILE)
            return pltpu.make_async_copy(zero_ref, xs_hbm.at[pl.ds(start, MOE_TILE)], sem.at[1])

        for c in range(pstart_ref.shape[0]):
            @pl.when(pvalid_ref[c] != 0)
            def _start():
                fill_copy(c).start()
        for c in range(pstart_ref.shape[0]):
            @pl.when(pvalid_ref[c] != 0)
            def _wait():
                fill_copy(c).wait()

    base = step * tp

    def issue(g, carry):
        r8 = pl.multiple_of(g * SUBLANES, SUBLANES)
        for j in range(SUBLANES):
            d = dest_ref[base + r8 + j]
            pltpu.make_async_copy(x_ref.at[pl.ds(r8 + j, 1)], xs_hbm.at[pl.ds(d, 1)], sem.at[0]).start()
        return carry

    lax.fori_loop(0, tp // SUBLANES, issue, 0)
    pltpu.make_async_copy(x_ref, xs_hbm.at[pl.ds(0, tp)], sem.at[0]).wait()


def _permute(x1e, dest, pad_start, pad_valid, tp=512):
    s = x1e.shape[0]
    return pl.pallas_call(
        _permute_kernel,
        grid_spec=pltpu.PrefetchScalarGridSpec(
            num_scalar_prefetch=3,
            grid=(s // tp,),
            in_specs=[pl.BlockSpec((tp, X1E_W), lambda i, d, ps, pv: (i, 0))],
            out_specs=pl.BlockSpec(memory_space=pl.ANY),
            scratch_shapes=[pltpu.VMEM((MOE_TILE, X1E_W), f32), pltpu.SemaphoreType.DMA((2,))],
        ),
        out_shape=jax.ShapeDtypeStruct((_n_tiles(s) * MOE_TILE, X1E_W), f32),
        compiler_params=pltpu.CompilerParams(
            dimension_semantics=("arbitrary",), vmem_limit_bytes=VMEM_LIMIT),
        name="moe_permute",
    )(dest, pad_start, pad_valid, x1e)


def _moe_kernel(nact_ref, elo_ref, ehi_ref, xs_ref, wg0_ref, wu0_ref, wd0_ref, wg1_ref, wu1_ref, wd1_ref,
                ys_ref):
    @pl.when(pl.program_id(0) < nact_ref[0])
    def _tile():
        xb = xs_ref[:, :D_MODEL].astype(bf16)
        acc = None
        for col, (wg_ref, wu_ref, wd_ref) in enumerate(((wg0_ref, wu0_ref, wd0_ref),
                                                        (wg1_ref, wu1_ref, wd1_ref))):
            gate = jnp.dot(xb, wg_ref[0, 0], preferred_element_type=f32)
            up = jnp.dot(xb, wu_ref[0, 0], preferred_element_type=f32)
            hid = (gate * _sigmoid(gate)) * up * xs_ref[:, D_MODEL + col:D_MODEL + col + 1]
            part = jnp.dot(hid.astype(bf16), wd_ref[0, 0], preferred_element_type=f32)
            acc = part if acc is None else acc + part
        ys_ref[...] = acc

    @pl.when(pl.program_id(0) >= nact_ref[0])
    def _unused_tile():
        ys_ref[...] = jnp.zeros_like(ys_ref)


def _moe(xs, n_active, tile_lo, tile_hi, w_gate, w_up, w_down, layer):
    n_tiles = xs.shape[0] // MOE_TILE
    tile = lambda j, na, lo, hi: (jnp.minimum(j, na[0] - 1), 0)
    low = lambda j, na, lo, hi: (layer, lo[j], 0, 0)
    high = lambda j, na, lo, hi: (layer, hi[j], 0, 0)
    up_shape = (1, 1, D_MODEL, D_FF_EXPERT)
    down_shape = (1, 1, D_FF_EXPERT, D_MODEL)
    return pl.pallas_call(
        _moe_kernel,
        grid_spec=pltpu.PrefetchScalarGridSpec(
            num_scalar_prefetch=3,
            grid=(n_tiles,),
            in_specs=[
                pl.BlockSpec((MOE_TILE, X1E_W), tile),
                pl.BlockSpec(up_shape, low), pl.BlockSpec(up_shape, low), pl.BlockSpec(down_shape, low),
                pl.BlockSpec(up_shape, high), pl.BlockSpec(up_shape, high), pl.BlockSpec(down_shape, high),
            ],
            out_specs=pl.BlockSpec((MOE_TILE, D_MODEL), lambda j, na, lo, hi: (j, 0)),
        ),
        out_shape=jax.ShapeDtypeStruct((n_tiles * MOE_TILE, D_MODEL), f32),
        compiler_params=pltpu.CompilerParams(
            dimension_semantics=("arbitrary",), vmem_limit_bytes=VMEM_LIMIT),
        name="moe_ffn",
    )(n_active, tile_lo, tile_hi, xs, w_gate, w_up, w_down, w_gate, w_up, w_down)


def _unpermute_kernel(dest_ref, x1_ref, ys_hbm, lg_ref, lb_ref, out_ref, outb_ref, ybuf, sem):
    i = pl.program_id(0)
    n = pl.num_programs(0)
    tu = x1_ref.shape[0]

    def gather(tile, slot):
        base = tile * tu

        def issue(g, carry):
            r8 = pl.multiple_of(g * SUBLANES, SUBLANES)
            for j in range(SUBLANES):
                d = dest_ref[base + r8 + j]
                pltpu.make_async_copy(ys_hbm.at[pl.ds(d, 1)], ybuf.at[slot, pl.ds(r8 + j, 1)],
                                      sem.at[slot]).start()
            return carry

        lax.fori_loop(0, tu // SUBLANES, issue, 0)

    slot = lax.rem(i, 2)

    @pl.when(i == 0)
    def _first():
        gather(0, 0)

    @pl.when(i + 1 < n)
    def _next():
        gather(i + 1, 1 - slot)

    pltpu.make_async_copy(ys_hbm.at[pl.ds(0, tu)], ybuf.at[slot], sem.at[slot]).wait()
    y = _layer_norm(DEEPNORM_ALPHA * x1_ref[...] + ybuf[slot], lg_ref[...], lb_ref[...])
    out_ref[...] = y
    outb_ref[...] = y.astype(bf16)


def _unpermute(x1e, ys, dest, ln_g, ln_b, tu=256):
    s = x1e.shape[0]
    row = lambda i, d: (i, 0)
    fixed = lambda i, d: (0, 0)
    return pl.pallas_call(
        _unpermute_kernel,
        grid_spec=pltpu.PrefetchScalarGridSpec(
            num_scalar_prefetch=1,
            grid=(s // tu,),
            in_specs=[
                pl.BlockSpec((tu, D_MODEL), row),
                pl.BlockSpec(memory_space=pl.ANY),
                pl.BlockSpec((1, D_MODEL), fixed),
                pl.BlockSpec((1, D_MODEL), fixed),
            ],
            out_specs=[pl.BlockSpec((tu, D_MODEL), row), pl.BlockSpec((tu, D_MODEL), row)],
            scratch_shapes=[pltpu.VMEM((2, tu, D_MODEL), f32), pltpu.SemaphoreType.DMA((2,))],
        ),
        out_shape=[jax.ShapeDtypeStruct((s, D_MODEL), f32), jax.ShapeDtypeStruct((s, D_MODEL), bf16)],
        compiler_params=pltpu.CompilerParams(
            dimension_semantics=("arbitrary",), vmem_limit_bytes=VMEM_LIMIT),
        name="moe_unpermute_ln2",
    )(dest, x1e, ys, ln_g, ln_b)


def _pad_lanes(a, width=LANES):
    return jnp.pad(a, ((0, 0), (0, width - a.shape[1])))


def kernel(x, w_in, conv_w, dn_a_log, dn_dt_bias, dn_norm_w, p_a, p_b, w_out, ln1_g, ln1_b, w_router, router_bias, w_gate, w_up, w_down, ln2_g, ln2_b):
    bsz, s, _ = x.shape
    assert bsz == 1 and s % 512 == 0
    xf = x[0]
    xb = xf

    c_ba = 4 * DN_WIDTH
    c_sb = c_ba + 2 * DN_HEADS
    c_gate = c_sb + SB_SLAB
    wr_pad = _pad_lanes(w_router)
    wr_hi = wr_pad.astype(bf16)
    wr_cat = jnp.concatenate([wr_hi, (wr_pad - wr_hi.astype(f32)).astype(bf16)], axis=1)
    rb_pad = jnp.pad(router_bias[None, :], ((0, 0), (0, LANES - N_EXPERTS)), constant_values=-jnp.inf)
    head_pad = ((0, 0), (DN_HEADS, LANES - 2 * DN_HEADS))
    col = jnp.arange(w_in.shape[-1])
    q_scale = jnp.where(jnp.logical_and(col >= c_sb, col < c_sb + SB_WIDTH), SB_DH ** -0.5, 1.0).astype(f32)
    w_in16 = (w_in * q_scale).astype(bf16)
    wg16, wu16, wd16 = w_gate.astype(bf16), w_up.astype(bf16), w_down.astype(bf16)

    for l in range(DEPTH):
        w = w_in16[l]
        w_dn = jnp.concatenate([w[:, :c_ba], _pad_lanes(w[:, c_ba:c_sb])], axis=1)
        w_sb = w[:, c_sb:c_gate]
        w_g = w[:, c_gate:]

        p_dn, p_sb, gates = _proj(xb, w_dn, w_sb, w_g)
        o_a = _deltanet(p_dn,
                        jnp.pad(conv_w[l], ((0, 8 - CONV_K), (0, 0))),
                        jnp.pad(dn_a_log[l][None, :], head_pad),
                        jnp.pad(dn_dt_bias[l][None, :], head_pad),
                        dn_norm_w[l][None, :])
        o_b = _sb_attention(p_sb)
        x1e, route, cnt = _merge(xf, o_a, o_b, gates,
                                p_a[l].astype(bf16), p_b[l].astype(bf16), w_out[l].astype(bf16),
                                ln1_g[l][None, :], ln1_b[l][None, :], wr_cat, rb_pad)
        dest, n_active, tile_lo, tile_hi, pad_start, pad_valid = _route_tables(route, cnt, s)
        xs = _permute(x1e, dest, pad_start, pad_valid)
        ys = _moe(xs, n_active, tile_lo, tile_hi,
                  wg16, wu16, wd16, l)
        xf, xb = _unpermute(x1e, ys, dest, ln2_g[l][None, :], ln2_b[l][None, :])
    return xf[None]
```

```python
import jax
import jax.numpy as jnp
from jax import lax
from jax.experimental import pallas as pl
from jax.experimental.pallas import tpu as pltpu

f32 = jnp.float32
bf16 = jnp.bfloat16
HIGHEST = lax.Precision.HIGHEST

D_MODEL = 1024
DEPTH = 2
CHUNK = 64
DN_HEADS = 4
DN_DK = 128
DN_DV = 128
CONV_K = 4
SB_HEADS = 8
SB_DH = 64
SB_BLOCK = 128
N_EXPERTS = 16
N_GROUPS = 4
EXPERTS_PER_GROUP = N_EXPERTS // N_GROUPS
D_FF_EXPERT = 512
LN_EPS = 1e-5
RMS_EPS = 1e-6
DEEPNORM_ALPHA = (2 * DEPTH) ** 0.25

DN_WIDTH = DN_HEADS * DN_DK
SB_WIDTH = SB_HEADS * SB_DH
CONV_CH = 3 * DN_WIDTH
LANES = 128
SUBLANES = 8
DN_SLAB = 4 * DN_WIDTH + LANES
BA_COL = 4 * DN_WIDTH
SB_SLAB = 3 * SB_WIDTH
GATE_SLAB = 2 * D_MODEL

SB_LOG_ZERO = -88.0
SB_STATIC_BLOCKS = 2
SB_MASK_PENALTY = -1e30

VMEM_LIMIT = 48 * 1024 * 1024


def _sigmoid(x):
    return 1.0 / (1.0 + jnp.exp(-x))


def _softplus(x):
    return jnp.maximum(x, 0.0) + jnp.log(1.0 + jnp.exp(-jnp.abs(x)))


def _nt_dot(a, b):
    return lax.dot_general(a, b, (((1,), (1,)), ((), ())), preferred_element_type=f32)


def _tn_dot(a, b):
    return lax.dot_general(a, b, (((0,), (0,)), ((), ())), preferred_element_type=f32)


def _layer_norm(y, g, b):
    mu = jnp.mean(y, axis=-1, keepdims=True)
    d = y - mu
    var = jnp.mean(d * d, axis=-1, keepdims=True)
    return d * lax.rsqrt(var + LN_EPS) * g + b


def _proj_kernel(x_ref, wdn_ref, wsb_ref, wg_ref, odn_ref, osb_ref, og_ref):
    x = x_ref[...].astype(bf16)
    odn_ref[...] = jnp.dot(x, wdn_ref[...], preferred_element_type=f32)
    osb_ref[...] = jnp.dot(x, wsb_ref[...], preferred_element_type=f32).astype(bf16)
    og_ref[...] = jnp.dot(x, wg_ref[...], preferred_element_type=f32).astype(bf16)


def _proj(x, w_dn, w_sb, w_g, tm=512):
    s = x.shape[0]
    row = lambda i: (i, 0)
    fixed = lambda i: (0, 0)
    return pl.pallas_call(
        _proj_kernel,
        grid=(s // tm,),
        in_specs=[
            pl.BlockSpec((tm, D_MODEL), row),
            pl.BlockSpec((D_MODEL, DN_SLAB), fixed),
            pl.BlockSpec((D_MODEL, SB_SLAB), fixed),
            pl.BlockSpec((D_MODEL, GATE_SLAB), fixed),
        ],
        out_specs=[
            pl.BlockSpec((tm, DN_SLAB), row),
            pl.BlockSpec((tm, SB_SLAB), row),
            pl.BlockSpec((tm, GATE_SLAB), row),
        ],
        out_shape=[
            jax.ShapeDtypeStruct((s, DN_SLAB), f32),
            jax.ShapeDtypeStruct((s, SB_SLAB), bf16),
            jax.ShapeDtypeStruct((s, GATE_SLAB), bf16),
        ],
        compiler_params=pltpu.CompilerParams(
            dimension_semantics=("arbitrary",), vmem_limit_bytes=VMEM_LIMIT),
        name="proj",
    )(x, w_dn, w_sb, w_g)


DN_ROWS = 512
PAIR = 2 * CHUNK
TAIL = 8


def _split2(x):
    hi = x.astype(bf16)
    return hi, (x - hi.astype(f32)).astype(bf16)


def _split3(x):
    hi = x.astype(bf16)
    r = x - hi.astype(f32)
    mid = r.astype(bf16)
    return hi, mid, (r - mid.astype(f32)).astype(bf16)


def _dn_kernel(p_ref, cw_ref, alog_ref, dtb_ref, nw_ref, o_ref, state_ref, tail_ref, xe_ref):
    step = pl.program_id(0)
    rows = p_ref.shape[0]
    n_pb = rows // PAIR

    @pl.when(step == 0)
    def _init():
        state_ref[...] = jnp.zeros_like(state_ref)
        tail_ref[...] = jnp.zeros_like(tail_ref)

    xe_ref[0:TAIL, :] = tail_ref[...]
    xe_ref[TAIL:TAIL + rows, :] = p_ref[:, 0:CONV_CH]
    tail_ref[...] = p_ref[rows - TAIL:rows, 0:CONV_CH]

    row128 = lax.broadcasted_iota(jnp.int32, (PAIR, LANES), 0)
    lane128 = lax.broadcasted_iota(jnp.int32, (PAIR, LANES), 1)
    same_chunk = (row128 >= CHUNK) == (lane128 >= CHUNK)
    tril_bd = jnp.where(jnp.logical_and(row128 >= lane128, same_chunk), 1.0, 0.0).astype(bf16)
    triu_bd = jnp.where(jnp.logical_and(row128 <= lane128, same_chunk), 1.0, 0.0).astype(bf16)
    first_rows = row128 < CHUNK
    row64 = lax.broadcasted_iota(jnp.int32, (CHUNK, LANES), 0)
    lane64 = lax.broadcasted_iota(jnp.int32, (CHUNK, LANES), 1)
    left = lane64 < CHUNK
    col_in_chunk = jnp.bitwise_and(lane64, CHUNK - 1)
    tri_p = row64 >= col_in_chunk
    strict_p = row64 > col_in_chunk
    eye_p = jnp.where(row64 == col_in_chunk, 1.0, 0.0).astype(f32)

    def block_diag(z):
        zero = jnp.zeros_like(z)
        return jnp.concatenate([jnp.where(left, z, zero), jnp.where(left, zero, z)], axis=0)

    def pair_matmul(y_hi, y_lo, zbd_hi, zbd_lo):
        return (jnp.dot(jnp.concatenate([y_hi, y_lo], axis=1), jnp.concatenate([zbd_hi, zbd_hi], axis=0),
                        preferred_element_type=f32)
                + jnp.dot(y_hi, zbd_lo, preferred_element_type=f32))

    nw = nw_ref[...]
    heads = range(DN_HEADS)
    base = TAIL - (CONV_K - 1)

    def pair_block(pb):
        r0 = pb * PAIR
        ba = p_ref[r0:r0 + PAIR, BA_COL:BA_COL + LANES]
        beta_all = _sigmoid(ba)
        parts = _split3(-jnp.exp(alog_ref[...]) * _softplus(ba + dtb_ref[...]))
        gcol = sum(jnp.dot(tril_bd, pt, preferred_element_type=f32) for pt in parts)
        grow = sum(_tn_dot(pt, triu_bd) for pt in parts)
        yield

        qkv = []
        for grp in range(3):
            outs = []
            for h in heads:
                col = grp * DN_WIDTH + h * DN_DK
                acc = xe_ref[base + r0:base + r0 + PAIR, col:col + DN_DK] * cw_ref[0:1, col:col + DN_DK]
                for j in range(1, CONV_K):
                    acc = acc + (xe_ref[base + r0 + j:base + r0 + j + PAIR, col:col + DN_DK]
                                 * cw_ref[j:j + 1, col:col + DN_DK])
                y = acc * _sigmoid(acc)
                if grp < 2:
                    y = y * lax.rsqrt(jnp.sum(y * y, axis=-1, keepdims=True) + RMS_EPS)
                outs.append(y * (DN_DK ** -0.5) if grp == 0 else y)
                yield
            qkv.append(outs)
        qs, ks, vs = qkv

        kbetas, qgs, kdecs, rstacks, decays, egl = [], [], [], [], [], []
        for h in heads:
            gc = jnp.broadcast_to(gcol[:, DN_HEADS + h:DN_HEADS + h + 1], (PAIR, LANES))
            beta = jnp.broadcast_to(beta_all[:, h:h + 1], (PAIR, LANES))
            eg = jnp.exp(gc)
            glast = jnp.where(first_rows, gc[CHUNK - 1:CHUNK, :], gc[PAIR - 1:PAIR, :])
            kbeta = ks[h] * beta
            kbetas.append(kbeta)
            qgs.append(qs[h] * eg)
            kdecs.append((ks[h] * jnp.exp(glast - gc)).astype(bf16))
            rstacks.append(jnp.concatenate([kbeta * eg, vs[h] * beta], axis=1).astype(bf16))
            gdiff = jnp.where(left, gc[:CHUNK], gc[CHUNK:]) - grow[DN_HEADS + h:DN_HEADS + h + 1, :]
            decays.append(jnp.where(tri_p, jnp.exp(jnp.where(tri_p, gdiff, 0.0)), 0.0))
            egl.append((jnp.exp(gc[CHUNK - 1:CHUNK, :]), jnp.exp(gc[PAIR - 1:PAIR, :])))
            yield

        lps, qkms = [], []
        for h in heads:
            kk = _nt_dot(jnp.concatenate([kbetas[h], qs[h]], axis=0).astype(bf16), ks[h].astype(bf16))
            lps.append(jnp.where(strict_p, jnp.where(left, kk[0:CHUNK], kk[CHUNK:PAIR]) * decays[h], 0.0))
            qkms.append(jnp.where(tri_p, jnp.where(left, kk[PAIR:PAIR + CHUNK], kk[PAIR + CHUNK:]) * decays[h], 0.0)
                        .astype(bf16))
            yield

        pw = [_split2(lp) for lp in lps]
        pw_bd = [(block_diag(hi), block_diag(lo)) for hi, lo in pw]
        ts = [eye_p - lp for lp in lps]
        for _ in range(5):
            pw = [_split2(pair_matmul(*pw[h], *pw_bd[h])) for h in heads]
            pw_bd = [(block_diag(hi), block_diag(lo)) for hi, lo in pw]
            yield
            ts = [ts[h] + pair_matmul(*_split2(ts[h]), *pw_bd[h]) for h in heads]
            yield

        zero_p = jnp.zeros((CHUNK, LANES), bf16)
        mq, bo = [], []
        for h in heads:
            t16 = ts[h].astype(bf16)
            halves = (jnp.where(left, t16, zero_p), jnp.where(left, zero_p, t16))
            qk_halves = (jnp.where(left, qkms[h], zero_p), jnp.where(left, zero_p, qkms[h]))
            wus = [jnp.dot(th, rstacks[h], preferred_element_type=f32).astype(bf16) for th in halves]
            wu_stack = jnp.concatenate(wus, axis=0)
            mq_h, bo_h = [], []
            for c in range(2):
                cs = slice(c * CHUNK, (c + 1) * CHUNK)
                kw = _tn_dot(kdecs[h][cs], wus[c])
                qw = jnp.dot(qk_halves[c], wu_stack, preferred_element_type=f32)
                mq_h.append(jnp.concatenate([kw[:, :DN_DK], qgs[h][cs] - qw[:, :DN_DK]], axis=0).astype(bf16))
                bo_h.append((kw[:, DN_DK:], qw[:, DN_DK:]))
            mq.append(mq_h)
            bo.append(bo_h)
            yield

        prepared[pb] = (mq, bo, egl)

    prepared = [None] * n_pb
    pipes = [pair_block(pb) for pb in range(n_pb)]
    while pipes:
        pipes = [pipe for pipe in pipes if next(pipe, "done") != "done"]

    states = [state_ref[h] for h in heads]
    for pb in range(n_pb):
        mq, bo, egl = prepared[pb]
        for c in range(2):
            for h in heads:
                ms = jnp.dot(mq[h][c], states[h].astype(bf16), preferred_element_type=f32)
                b_c, o_c = bo[h][c]
                o = ms[DN_DK:] + o_c
                states[h] = states[h] * egl[h][c] - ms[:DN_DK] + b_c
                o = o * lax.rsqrt(jnp.mean(o * o, axis=-1, keepdims=True) + RMS_EPS) * nw
                rc = pb * PAIR + c * CHUNK
                z = p_ref[rc:rc + CHUNK, 3 * DN_WIDTH + h * DN_DV:3 * DN_WIDTH + (h + 1) * DN_DV]
                o_ref[rc:rc + CHUNK, h * DN_DV:(h + 1) * DN_DV] = o * (z * _sigmoid(z))
    for h in heads:
        state_ref[h] = states[h]


def _deltanet(p_dn, conv_w8, alog_row, dtb_row, nw_row):
    s = p_dn.shape[0]
    fixed = lambda i: (0, 0)
    return pl.pallas_call(
        _dn_kernel,
        grid=(s // DN_ROWS,),
        in_specs=[
            pl.BlockSpec((DN_ROWS, DN_SLAB), lambda i: (i, 0)),
            pl.BlockSpec((8, CONV_CH), fixed),
            pl.BlockSpec((1, LANES), fixed),
            pl.BlockSpec((1, LANES), fixed),
            pl.BlockSpec((1, DN_DV), fixed),
        ],
        out_specs=pl.BlockSpec((DN_ROWS, DN_HEADS * DN_DV), lambda i: (i, 0)),
        out_shape=jax.ShapeDtypeStruct((s, DN_HEADS * DN_DV), f32),
        scratch_shapes=[
            pltpu.VMEM((DN_HEADS, DN_DK, DN_DV), f32),
            pltpu.VMEM((TAIL, CONV_CH), f32),
            pltpu.VMEM((TAIL + DN_ROWS, CONV_CH), f32),
        ],
        compiler_params=pltpu.CompilerParams(
            dimension_semantics=("arbitrary",), vmem_limit_bytes=VMEM_LIMIT),
        name="deltanet",
    )(p_dn, conv_w8, alog_row, dtb_row, nw_row)


N_PAIRS = SB_HEADS // 2
SB_WAVE = N_PAIRS


def _sb_kernel(q_ref, kd_ref, k1_ref, vd_ref, v1_ref, kv_hbm, o_ref, kbuf, vbuf, sem):
    qb = pl.program_id(0)
    blk = SB_BLOCK
    row = lax.broadcasted_iota(jnp.int32, (blk, blk), 0)
    lane = lax.broadcasted_iota(jnp.int32, (blk, blk), 1)
    diag_mask = row > lane
    even = lane < SB_DH
    suffix = jnp.where(row > lane, 1.0, 0.0).astype(bf16)
    suffix2 = jnp.concatenate([suffix, suffix], axis=0)

    def split_heads(x):
        zero = jnp.zeros_like(x)
        return jnp.where(even, x, zero), jnp.where(even, zero, x)

    def suffix_sums(spm):
        hi = spm.astype(bf16)
        lo = (spm - hi.astype(f32)).astype(bf16)
        return jnp.dot(jnp.concatenate([hi, lo], axis=1), suffix2, preferred_element_type=f32)

    pairs = [slice(p * LANES, (p + 1) * LANES) for p in range(N_PAIRS)]
    k_refs = (kd_ref, k1_ref)
    v_refs = (vd_ref, v1_ref)
    pen1 = jnp.where(qb >= 1, 0.0, -SB_MASK_PENALTY).astype(f32)
    accs, carries = [], []
    for p0 in range(0, N_PAIRS, SB_WAVE):
        wave = range(p0, p0 + SB_WAVE)
        q_heads = {p: split_heads(q_ref[:, pairs[p]]) for p in wave}
        units = [(p, b, hh) for b in range(SB_STATIC_BLOCKS) for p in wave for hh in range(2)]
        z = {u: _nt_dot(q_heads[u[0]][u[2]], k_refs[u[1]][:, pairs[u[0]]]) for u in units}
        sp = {u: _softplus(z[u]) for u in units}
        spm = {u: (jnp.where(diag_mask, sp[u], 0.0) if u[1] == 0 else sp[u]) for u in units}
        logw = {u: z[u] - sp[u] - suffix_sums(spm[u]) for u in units}
        keep = {u: jnp.sum(spm[u], axis=1, keepdims=True) for u in units}
        att = {}
        for p, b, hh in units:
            if b == 0:
                att[p, b, hh] = jnp.where(diag_mask, jnp.exp(logw[p, b, hh]), 0.0).astype(bf16)
            else:
                att[p, b, hh] = jnp.exp(logw[p, b, hh] - (keep[p, 0, hh] + pen1)).astype(bf16)
        for p in wave:
            vals = jnp.concatenate([h for b in range(SB_STATIC_BLOCKS)
                                    for h in split_heads(v_refs[b][:, pairs[p]])], axis=0)
            lhs = jnp.concatenate([att[p, b, hh] for b in range(SB_STATIC_BLOCKS) for hh in range(2)], axis=1)
            accs.append(jnp.dot(lhs, vals, preferred_element_type=f32))
            carries.extend(keep[p, 0, hh] + keep[p, 1, hh] for hh in range(2))

    def live(cr):
        m = cr[0]
        for c in cr[1:]:
            m = jnp.minimum(m, c)
        return jnp.min(m) <= -SB_LOG_ZERO

    def cond(st):
        kb, more = st[0], st[1]
        return jnp.logical_and(kb >= 0, more)

    def body(st):
        kb = st[0]
        acc_l = list(st[2:2 + N_PAIRS])
        car_l = list(st[2 + N_PAIRS:])
        start = pl.multiple_of(kb * blk, blk)
        cpk = pltpu.make_async_copy(kv_hbm.at[pl.ds(start, blk), pl.ds(SB_WIDTH, SB_WIDTH)], kbuf, sem.at[0])
        cpv = pltpu.make_async_copy(kv_hbm.at[pl.ds(start, blk), pl.ds(2 * SB_WIDTH, SB_WIDTH)], vbuf, sem.at[1])
        cpk.start()
        cpv.start()
        cpk.wait()
        cpv.wait()
        new_car = []
        for p, ps in enumerate(pairs):
            atts = []
            for hh, qh in enumerate(split_heads(q_ref[:, ps])):
                zz = _nt_dot(qh, kbuf[:, ps])
                spp = _softplus(zz)
                carry = car_l[2 * p + hh]
                atts.append(jnp.exp(zz - spp - suffix_sums(spp) - carry).astype(bf16))
                new_car.append(carry + jnp.sum(spp, axis=1, keepdims=True))
            vals = jnp.concatenate(split_heads(vbuf[:, ps]), axis=0)
            acc_l[p] = acc_l[p] + jnp.dot(jnp.concatenate(atts, axis=1), vals, preferred_element_type=f32)
        return (kb - 1, live(new_car), *acc_l, *new_car)

    st = lax.while_loop(cond, body, (qb - SB_STATIC_BLOCKS, live(carries), *accs, *carries))
    for p, ps in enumerate(pairs):
        o_ref[:, ps] = st[2 + p]


def _sb_attention(p_sb):
    s = p_sb.shape[0]
    blk = SB_BLOCK

    def spec(col, back):
        return pl.BlockSpec((blk, SB_WIDTH), lambda i: (jnp.maximum(i - back, 0), col))

    return pl.pallas_call(
        _sb_kernel,
        grid=(s // blk,),
        in_specs=[spec(0, 0), spec(1, 0), spec(1, 1), spec(2, 0), spec(2, 1),
                  pl.BlockSpec(memory_space=pl.ANY)],
        out_specs=pl.BlockSpec((blk, SB_WIDTH), lambda i: (i, 0)),
        out_shape=jax.ShapeDtypeStruct((s, SB_WIDTH), f32),
        scratch_shapes=[
            pltpu.VMEM((blk, SB_WIDTH), bf16),
            pltpu.VMEM((blk, SB_WIDTH), bf16),
            pltpu.SemaphoreType.DMA((2,)),
        ],
        compiler_params=pltpu.CompilerParams(
            dimension_semantics=("arbitrary",), vmem_limit_bytes=VMEM_LIMIT),
        name="sb_attention",
    )(p_sb, p_sb, p_sb, p_sb, p_sb, p_sb)


MOE_TILE = 256
PAIRS_PER_GROUP = 6
N_CLASSES = N_GROUPS * PAIRS_PER_GROUP
X1E_W = D_MODEL + LANES
MERGE_ROWS = 512
MERGE_PARTS = 2


def _route(aff, sel):
    neg = -jnp.inf
    best = None
    for gidx in range(N_GROUPS):
        rows = slice(gidx * EXPERTS_PER_GROUP, (gidx + 1) * EXPERTS_PER_GROUP)
        sg = sel[rows]
        idx = lax.broadcasted_iota(jnp.int32, sg.shape, 0) + gidx * EXPERTS_PER_GROUP
        m1 = jnp.max(sg, axis=0, keepdims=True)
        i1 = jnp.min(jnp.where(sg == m1, idx, N_EXPERTS), axis=0, keepdims=True)
        sg2 = jnp.where(idx == i1, neg, sg)
        m2 = jnp.max(sg2, axis=0, keepdims=True)
        i2 = jnp.min(jnp.where(sg2 == m2, idx, N_EXPERTS), axis=0, keepdims=True)
        score = m1 + m2
        if best is None:
            best = (score, i1, i2)
        else:
            better = score > best[0]
            best = (jnp.where(better, score, best[0]),
                    jnp.where(better, i1, best[1]),
                    jnp.where(better, i2, best[2]))
    _, i1, i2 = best
    expert = lax.broadcasted_iota(jnp.int32, aff.shape, 0)
    w1 = jnp.sum(jnp.where(expert == i1, aff, 0.0), axis=0, keepdims=True)
    w2 = jnp.sum(jnp.where(expert == i2, aff, 0.0), axis=0, keepdims=True)
    denom = w1 + w2
    w1, w2 = w1 / denom, w2 / denom
    first_low = i1 < i2
    e_lo = jnp.minimum(i1, i2)
    e_hi = jnp.maximum(i1, i2)
    a = jnp.bitwise_and(e_lo, EXPERTS_PER_GROUP - 1)
    b = jnp.bitwise_and(e_hi, EXPERTS_PER_GROUP - 1)
    pair = jnp.where(a == 0, 0, jnp.where(a == 1, 3, 5)) + (b - a - 1)
    cls = lax.shift_right_logical(e_lo, 2) * PAIRS_PER_GROUP + pair
    return cls, jnp.where(first_low, w1, w2), jnp.where(first_low, w2, w1)


CLASS_ROWS = 32


def _merge_kernel(x_ref, oa_ref, ob_ref, g_ref, pa_ref, pb_ref, wo_ref, lg_ref, lb_ref, wr_ref, rb_ref,
                  x1e_ref, route_ref, cnt_ref, run_ref):
    step = pl.program_id(0)

    @pl.when(step == 0)
    def _init():
        run_ref[...] = jnp.zeros_like(run_ref)

    rows = x_ref.shape[0] // MERGE_PARTS
    parts = [slice(i * rows, (i + 1) * rows) for i in range(MERGE_PARTS)]
    a = [jnp.dot(oa_ref[p, :].astype(bf16), pa_ref[...], preferred_element_type=f32) for p in parts]
    b = [jnp.dot(ob_ref[p, :].astype(bf16), pb_ref[...], preferred_element_type=f32) for p in parts]
    merged = [(_sigmoid(g_ref[p, :D_MODEL].astype(f32)) * ai
               + _sigmoid(g_ref[p, D_MODEL:].astype(f32)) * bi).astype(bf16)
              for p, ai, bi in zip(parts, a, b)]
    mix = [jnp.dot(m, wo_ref[...], preferred_element_type=f32) for m in merged]
    x1 = [_layer_norm(DEEPNORM_ALPHA * x_ref[p, :] + mi, lg_ref[...], lb_ref[...]) for p, mi in zip(parts, mix)]
    x1b = [xi.astype(bf16) for xi in x1]
    for p, xi in zip(parts, x1):
        x1e_ref[p, :D_MODEL] = xi

    x_lo = [(xi - xbi.astype(f32)).astype(bf16) for xi, xbi in zip(x1, x1b)]
    t = [_nt_dot(wr_ref[...], xbi) for xbi in x1b]
    logits = [ti[:N_EXPERTS] + ti[N_EXPERTS:] + _nt_dot(wr_ref[:N_EXPERTS, :], xl) for ti, xl in zip(t, x_lo)]
    aff = [_sigmoid(lg) for lg in logits]
    routed = [_route(af, af + rb_ref[...]) for af in aff]

    rr = lax.broadcasted_iota(jnp.int32, (rows, rows), 0)
    cc = lax.broadcasted_iota(jnp.int32, (rows, rows), 1)
    triu = jnp.where(rr <= cc, 1.0, 0.0).astype(bf16)
    class_id = lax.broadcasted_iota(jnp.int32, (CLASS_ROWS, rows), 0)
    for p, (cls, w_lo, w_hi) in zip(parts, routed):
        onehot = class_id == cls
        prefix = jnp.dot(jnp.where(onehot, 1.0, 0.0).astype(bf16), triu, preferred_element_type=f32)
        run = run_ref[...]
        rank = (jnp.sum(jnp.where(onehot, prefix + run, 0.0), axis=0, keepdims=True) - 1.0).astype(jnp.int32)
        run_ref[...] = run + prefix[:, rows - 1:rows]
        pad = jnp.zeros((SUBLANES - 3, rows), jnp.int32)
        route_ref[0, :, p] = jnp.concatenate(
            [cls, lax.shift_right_logical(rank, 7), jnp.bitwise_and(rank, LANES - 1), pad], axis=0)
        w_rows = jnp.concatenate([w_lo, w_hi, jnp.zeros((LANES - 2, rows), f32)], axis=0)
        x1e_ref[p, D_MODEL:] = w_rows.T
    cnt_ref[...] = jnp.broadcast_to(run_ref[...], cnt_ref.shape).astype(jnp.int32)


def _merge(x, o_a, o_b, gates, p_a, p_b, w_out, ln_g, ln_b, wr_cat, r_bias):
    s = x.shape[0]
    tm = MERGE_ROWS
    row = lambda i: (i, 0)
    fixed = lambda i: (0, 0)
    return pl.pallas_call(
        _merge_kernel,
        grid=(s // tm,),
        in_specs=[
            pl.BlockSpec((tm, D_MODEL), row),
            pl.BlockSpec((tm, DN_HEADS * DN_DV), row),
            pl.BlockSpec((tm, SB_WIDTH), row),
            pl.BlockSpec((tm, GATE_SLAB), row),
            pl.BlockSpec((DN_HEADS * DN_DV, D_MODEL), fixed),
            pl.BlockSpec((SB_WIDTH, D_MODEL), fixed),
            pl.BlockSpec((D_MODEL, D_MODEL), fixed),
            pl.BlockSpec((1, D_MODEL), fixed),
            pl.BlockSpec((1, D_MODEL), fixed),
            pl.BlockSpec((2 * N_EXPERTS, D_MODEL), fixed),
            pl.BlockSpec((N_EXPERTS, 1), fixed),
        ],
        out_specs=[
            pl.BlockSpec((tm, X1E_W), row),
            pl.BlockSpec((1, SUBLANES, tm), lambda i: (i, 0, 0)),
            pl.BlockSpec((CLASS_ROWS, LANES), fixed),
        ],
        out_shape=[
            jax.ShapeDtypeStruct((s, X1E_W), f32),
            jax.ShapeDtypeStruct((s // tm, SUBLANES, tm), jnp.int32),
            jax.ShapeDtypeStruct((CLASS_ROWS, LANES), jnp.int32),
        ],
        scratch_shapes=[pltpu.VMEM((CLASS_ROWS, 1), f32)],
        compiler_params=pltpu.CompilerParams(
            dimension_semantics=("arbitrary",), vmem_limit_bytes=VMEM_LIMIT),
        name="merge_router",
    )(x, o_a, o_b, gates, p_a, p_b, w_out, ln_g, ln_b, wr_cat, r_bias)


def _class_experts():
    lo, hi = [], []
    for g in range(N_GROUPS):
        for a in range(EXPERTS_PER_GROUP):
            for b in range(a + 1, EXPERTS_PER_GROUP):
                lo.append(g * EXPERTS_PER_GROUP + a)
                hi.append(g * EXPERTS_PER_GROUP + b)
    return jnp.array(lo, jnp.int32), jnp.array(hi, jnp.int32)


def _n_tiles(s):
    return -(-(s + N_CLASSES * (MOE_TILE - 1)) // MOE_TILE)


def _route_tables(route, cnt, s):
    counts = cnt[:N_CLASSES, 0]
    padded = (counts + (MOE_TILE - 1)) // MOE_TILE * MOE_TILE
    ends = jnp.cumsum(padded)
    offs = ends - padded
    cls, rank_hi, rank_lo = (route[:, r, :].reshape(s) for r in range(3))
    dest = offs[cls] + rank_hi * LANES + rank_lo
    n_active = (ends[-1] // MOE_TILE).astype(jnp.int32)[None]
    tile_row = jnp.minimum(jnp.arange(_n_tiles(s), dtype=jnp.int32) * MOE_TILE, ends[-1] - 1)
    tile_cls = jnp.minimum(jnp.sum(tile_row[:, None] >= ends[None, :], axis=1), N_CLASSES - 1)
    e_lo, e_hi = _class_experts()
    tail = jnp.arange(s // MOE_TILE, _n_tiles(s), dtype=jnp.int32)
    pad_start = jnp.concatenate([ends - MOE_TILE, tail * MOE_TILE]).astype(jnp.int32)
    pad_valid = jnp.concatenate([padded > 0, tail >= n_active[0]]).astype(jnp.int32)
    return (dest.astype(jnp.int32), n_active, e_lo[tile_cls], e_hi[tile_cls], pad_start, pad_valid)


def _permute_kernel(dest_ref, pstart_ref, pvalid_ref, x_ref, xs_hbm, zero_ref, sem):
    step = pl.program_id(0)
    tp = x_ref.shape[0]

    @pl.when(step == 0)
    def _fill():
        zero_ref[...] = jnp.zeros_like(zero_ref)

        def fill_copy(c):
            start = pl.multiple_of(pstart_ref[c], MOE_TILE)
            return pltpu.make_async_copy(zero_ref, xs_hbm.at[pl.ds(start, MOE_TILE)], sem.at[1])

        for c in range(pstart_ref.shape[0]):
            @pl.when(pvalid_ref[c] != 0)
            def _start():
                fill_copy(c).start()
        for c in range(pstart_ref.shape[0]):
            @pl.when(pvalid_ref[c] != 0)
            def _wait():
                fill_copy(c).wait()

    base = step * tp

    def issue(g, carry):
        r8 = pl.multiple_of(g * SUBLANES, SUBLANES)
        for j in range(SUBLANES):
            d = dest_ref[base + r8 + j]
            pltpu.make_async_copy(x_ref.at[pl.ds(r8 + j, 1)], xs_hbm.at[pl.ds(d, 1)], sem.at[0]).start()
        return carry

    lax.fori_loop(0, tp // SUBLANES, issue, 0)
    pltpu.make_async_copy(x_ref, xs_hbm.at[pl.ds(0, tp)], sem.at[0]).wait()


def _permute(x1e, dest, pad_start, pad_valid, tp=512):
    s = x1e.shape[0]
    return pl.pallas_call(
        _permute_kernel,
        grid_spec=pltpu.PrefetchScalarGridSpec(
            num_scalar_prefetch=3,
            grid=(s // tp,),
            in_specs=[pl.BlockSpec((tp, X1E_W), lambda i, d, ps, pv: (i, 0))],
            out_specs=pl.BlockSpec(memory_space=pl.ANY),
            scratch_shapes=[pltpu.VMEM((MOE_TILE, X1E_W), f32), pltpu.SemaphoreType.DMA((2,))],
        ),
        out_shape=jax.ShapeDtypeStruct((_n_tiles(s) * MOE_TILE, X1E_W), f32),
        compiler_params=pltpu.CompilerParams(
            dimension_semantics=("arbitrary",), vmem_limit_bytes=VMEM_LIMIT),
        name="moe_permute",
    )(dest, pad_start, pad_valid, x1e)


def _moe_kernel(nact_ref, elo_ref, ehi_ref, xs_ref, wg0_ref, wu0_ref, wd0_ref, wg1_ref, wu1_ref, wd1_ref,
                ys_ref):
    @pl.when(pl.program_id(0) < nact_ref[0])
    def _tile():
        xb = xs_ref[:, :D_MODEL].astype(bf16)
        acc = None
        for col, (wg_ref, wu_ref, wd_ref) in enumerate(((wg0_ref, wu0_ref, wd0_ref),
                                                        (wg1_ref, wu1_ref, wd1_ref))):
            gate = jnp.dot(xb, wg_ref[0, 0], preferred_element_type=f32)
            up = jnp.dot(xb, wu_ref[0, 0], preferred_element_type=f32)
            hid = (gate * _sigmoid(gate)) * up * xs_ref[:, D_MODEL + col:D_MODEL + col + 1]
            part = jnp.dot(hid.astype(bf16), wd_ref[0, 0], preferred_element_type=f32)
            acc = part if acc is None else acc + part
        ys_ref[...] = acc

    @pl.when(pl.program_id(0) >= nact_ref[0])
    def _unused_tile():
        ys_ref[...] = jnp.zeros_like(ys_ref)


def _moe(xs, n_active, tile_lo, tile_hi, w_gate, w_up, w_down, layer):
    n_tiles = xs.shape[0] // MOE_TILE
    tile = lambda j, na, lo, hi: (jnp.minimum(j, na[0] - 1), 0)
    low = lambda j, na, lo, hi: (layer, lo[j], 0, 0)
    high = lambda j, na, lo, hi: (layer, hi[j], 0, 0)
    up_shape = (1, 1, D_MODEL, D_FF_EXPERT)
    down_shape = (1, 1, D_FF_EXPERT, D_MODEL)
    return pl.pallas_call(
        _moe_kernel,
        grid_spec=pltpu.PrefetchScalarGridSpec(
            num_scalar_prefetch=3,
            grid=(n_tiles,),
            in_specs=[
                pl.BlockSpec((MOE_TILE, X1E_W), tile),
                pl.BlockSpec(up_shape, low), pl.BlockSpec(up_shape, low), pl.BlockSpec(down_shape, low),
                pl.BlockSpec(up_shape, high), pl.BlockSpec(up_shape, high), pl.BlockSpec(down_shape, high),
            ],
            out_specs=pl.BlockSpec((MOE_TILE, D_MODEL), lambda j, na, lo, hi: (j, 0)),
        ),
        out_shape=jax.ShapeDtypeStruct((n_tiles * MOE_TILE, D_MODEL), f32),
        compiler_params=pltpu.CompilerParams(
            dimension_semantics=("arbitrary",), vmem_limit_bytes=VMEM_LIMIT),
        name="moe_ffn",
    )(n_active, tile_lo, tile_hi, xs, w_gate, w_up, w_down, w_gate, w_up, w_down)


def _unpermute_kernel(dest_ref, x1_ref, ys_hbm, lg_ref, lb_ref, out_ref, outb_ref, ybuf, sem):
    i = pl.program_id(0)
    n = pl.num_programs(0)
    tu = x1_ref.shape[0]

    def gather(tile, slot):
        base = tile * tu

        def issue(g, carry):
            r8 = pl.multiple_of(g * SUBLANES, SUBLANES)
            for j in range(SUBLANES):
                d = dest_ref[base + r8 + j]
                pltpu.make_async_copy(ys_hbm.at[pl.ds(d, 1)], ybuf.at[slot, pl.ds(r8 + j, 1)],
                                      sem.at[slot]).start()
            return carry

        lax.fori_loop(0, tu // SUBLANES, issue, 0)

    slot = lax.rem(i, 2)

    @pl.when(i == 0)
    def _first():
        gather(0, 0)

    @pl.when(i + 1 < n)
    def _next():
        gather(i + 1, 1 - slot)

    pltpu.make_async_copy(ys_hbm.at[pl.ds(0, tu)], ybuf.at[slot], sem.at[slot]).wait()
    y = _layer_norm(DEEPNORM_ALPHA * x1_ref[...] + ybuf[slot], lg_ref[...], lb_ref[...])
    out_ref[...] = y
    outb_ref[...] = y.astype(bf16)


def _unpermute(x1e, ys, dest, ln_g, ln_b, tu=256):
    s = x1e.shape[0]
    row = lambda i, d: (i, 0)
    fixed = lambda i, d: (0, 0)
    return pl.pallas_call(
        _unpermute_kernel,
        grid_spec=pltpu.PrefetchScalarGridSpec(
            num_scalar_prefetch=1,
            grid=(s // tu,),
            in_specs=[
                pl.BlockSpec((tu, D_MODEL), row),
                pl.BlockSpec(memory_space=pl.ANY),
                pl.BlockSpec((1, D_MODEL), fixed),
                pl.BlockSpec((1, D_MODEL), fixed),
            ],
            out_specs=[pl.BlockSpec((tu, D_MODEL), row), pl.BlockSpec((tu, D_MODEL), row)],
            scratch_shapes=[pltpu.VMEM((2, tu, D_MODEL), f32), pltpu.SemaphoreType.DMA((2,))],
        ),
        out_shape=[jax.ShapeDtypeStruct((s, D_MODEL), f32), jax.ShapeDtypeStruct((s, D_MODEL), bf16)],
        compiler_params=pltpu.CompilerParams(
            dimension_semantics=("arbitrary",), vmem_limit_bytes=VMEM_LIMIT),
        name="moe_unpermute_ln2",
    )(dest, x1e, ys, ln_g, ln_b)


def _pad_lanes(a, width=LANES):
    return jnp.pad(a, ((0, 0), (0, width - a.shape[1])))


def kernel(x, w_in, conv_w, dn_a_log, dn_dt_bias, dn_norm_w, p_a, p_b, w_out, ln1_g, ln1_b, w_router, router_bias, w_gate, w_up, w_down, ln2_g, ln2_b):
    bsz, s, _ = x.shape
    assert bsz == 1 and s % 512 == 0
    xf = x[0]
    xb = xf

    c_ba = 4 * DN_WIDTH
    c_sb = c_ba + 2 * DN_HEADS
    c_gate = c_sb + SB_SLAB
    wr_t = w_router.T
    wr_hi = wr_t.astype(bf16)
    wr_cat = jnp.concatenate([wr_hi, (wr_t - wr_hi.astype(f32)).astype(bf16)], axis=0)
    rb_col = router_bias[:, None]
    head_pad = ((0, 0), (DN_HEADS, LANES - 2 * DN_HEADS))
    col = jnp.arange(w_in.shape[-1])
    q_scale = jnp.where(jnp.logical_and(col >= c_sb, col < c_sb + SB_WIDTH), SB_DH ** -0.5, 1.0).astype(f32)
    w_in16 = (w_in * q_scale).astype(bf16)
    wg16, wu16, wd16 = w_gate.astype(bf16), w_up.astype(bf16), w_down.astype(bf16)

    for l in range(DEPTH):
        w = w_in16[l]
        w_dn = jnp.concatenate([w[:, :c_ba], _pad_lanes(w[:, c_ba:c_sb])], axis=1)
        w_sb = w[:, c_sb:c_gate]
        w_g = w[:, c_gate:]

        p_dn, p_sb, gates = _proj(xb, w_dn, w_sb, w_g)
        o_a = _deltanet(p_dn,
                        jnp.pad(conv_w[l], ((0, 8 - CONV_K), (0, 0))),
                        jnp.pad(dn_a_log[l][None, :], head_pad),
                        jnp.pad(dn_dt_bias[l][None, :], head_pad),
                        dn_norm_w[l][None, :])
        o_b = _sb_attention(p_sb)
        x1e, route, cnt = _merge(xf, o_a, o_b, gates,
                                p_a[l].astype(bf16), p_b[l].astype(bf16), w_out[l].astype(bf16),
                                ln1_g[l][None, :], ln1_b[l][None, :], wr_cat, rb_col)
        dest, n_active, tile_lo, tile_hi, pad_start, pad_valid = _route_tables(route, cnt, s)
        xs = _permute(x1e, dest, pad_start, pad_valid)
        ys = _moe(xs, n_active, tile_lo, tile_hi, wg16, wu16, wd16, l)
        xf, xb = _unpermute(x1e, ys, dest, ln2_g[l][None, :], ln2_b[l][None, :])
    return xf[None]
```

```python
import jax
import jax.numpy as jnp
from jax import lax
from jax.experimental import pallas as pl
from jax.experimental.pallas import tpu as pltpu

f32 = jnp.float32
bf16 = jnp.bfloat16
HIGHEST = lax.Precision.HIGHEST

D_MODEL = 1024
DEPTH = 2
CHUNK = 64
DN_HEADS = 4
DN_DK = 128
DN_DV = 128
CONV_K = 4
SB_HEADS = 8
SB_DH = 64
SB_BLOCK = 128
N_EXPERTS = 16
N_GROUPS = 4
EXPERTS_PER_GROUP = N_EXPERTS // N_GROUPS
D_FF_EXPERT = 512
LN_EPS = 1e-5
RMS_EPS = 1e-6
DEEPNORM_ALPHA = (2 * DEPTH) ** 0.25

DN_WIDTH = DN_HEADS * DN_DK
SB_WIDTH = SB_HEADS * SB_DH
CONV_CH = 3 * DN_WIDTH
LANES = 128
SUBLANES = 8
DN_SLAB = 4 * DN_WIDTH + LANES
BA_COL = 4 * DN_WIDTH
SB_SLAB = 3 * SB_WIDTH
GATE_SLAB = 2 * D_MODEL

SB_LOG_ZERO = -88.0
SB_STATIC_BLOCKS = 2
SB_MASK_PENALTY = -1e30

VMEM_LIMIT = 48 * 1024 * 1024


def _sigmoid(x):
    return 1.0 / (1.0 + jnp.exp(-x))


def _softplus(x):
    return jnp.maximum(x, 0.0) + jnp.log(1.0 + jnp.exp(-jnp.abs(x)))


def _nt_dot(a, b):
    return lax.dot_general(a, b, (((1,), (1,)), ((), ())), preferred_element_type=f32)


def _tn_dot(a, b):
    return lax.dot_general(a, b, (((0,), (0,)), ((), ())), preferred_element_type=f32)


def _layer_norm(y, g, b):
    mu = jnp.mean(y, axis=-1, keepdims=True)
    d = y - mu
    var = jnp.mean(d * d, axis=-1, keepdims=True)
    return d * lax.rsqrt(var + LN_EPS) * g + b


def _proj_kernel(x_ref, wdn_ref, wsb_ref, wg_ref, odn_ref, osb_ref, og_ref):
    x = x_ref[...].astype(bf16)
    odn_ref[...] = jnp.dot(x, wdn_ref[...], preferred_element_type=f32)
    osb_ref[...] = jnp.dot(x, wsb_ref[...], preferred_element_type=f32).astype(bf16)
    og_ref[...] = jnp.dot(x, wg_ref[...], preferred_element_type=f32).astype(bf16)


def _proj(x, w_dn, w_sb, w_g, tm=512):
    s = x.shape[0]
    row = lambda i: (i, 0)
    fixed = lambda i: (0, 0)
    return pl.pallas_call(
        _proj_kernel,
        grid=(s // tm,),
        in_specs=[
            pl.BlockSpec((tm, D_MODEL), row),
            pl.BlockSpec((D_MODEL, DN_SLAB), fixed),
            pl.BlockSpec((D_MODEL, SB_SLAB), fixed),
            pl.BlockSpec((D_MODEL, GATE_SLAB), fixed),
        ],
        out_specs=[
            pl.BlockSpec((tm, DN_SLAB), row),
            pl.BlockSpec((tm, SB_SLAB), row),
            pl.BlockSpec((tm, GATE_SLAB), row),
        ],
        out_shape=[
            jax.ShapeDtypeStruct((s, DN_SLAB), f32),
            jax.ShapeDtypeStruct((s, SB_SLAB), bf16),
            jax.ShapeDtypeStruct((s, GATE_SLAB), bf16),
        ],
        compiler_params=pltpu.CompilerParams(
            dimension_semantics=("arbitrary",), vmem_limit_bytes=VMEM_LIMIT),
        name="proj",
    )(x, w_dn, w_sb, w_g)


DN_ROWS = 1024
PAIR = 2 * CHUNK
TAIL = 8
DN_GROUP = 2
DN_SEQ_START = 22
DN_SEQ_EVERY = 5


def _split2(x):
    hi = x.astype(bf16)
    return hi, (x - hi.astype(f32)).astype(bf16)


def _split3(x):
    hi = x.astype(bf16)
    r = x - hi.astype(f32)
    mid = r.astype(bf16)
    return hi, mid, (r - mid.astype(f32)).astype(bf16)


def _dn_kernel(p_ref, cw_ref, alog_ref, dtb_ref, nw_ref, o_ref, state_ref, tail_ref, xe_ref):
    step = pl.program_id(0)
    rows = p_ref.shape[0]
    n_pb = rows // PAIR

    @pl.when(step == 0)
    def _init():
        state_ref[...] = jnp.zeros_like(state_ref)
        tail_ref[...] = jnp.zeros_like(tail_ref)

    xe_ref[0:TAIL, :] = tail_ref[...]
    xe_ref[TAIL:TAIL + rows, :] = p_ref[:, 0:CONV_CH]
    tail_ref[...] = p_ref[rows - TAIL:rows, 0:CONV_CH]

    row128 = lax.broadcasted_iota(jnp.int32, (PAIR, LANES), 0)
    lane128 = lax.broadcasted_iota(jnp.int32, (PAIR, LANES), 1)
    same_chunk = (row128 >= CHUNK) == (lane128 >= CHUNK)
    tril_bd = jnp.where(jnp.logical_and(row128 >= lane128, same_chunk), 1.0, 0.0).astype(bf16)
    triu_bd = jnp.where(jnp.logical_and(row128 <= lane128, same_chunk), 1.0, 0.0).astype(bf16)
    first_rows = row128 < CHUNK
    row64 = lax.broadcasted_iota(jnp.int32, (CHUNK, LANES), 0)
    lane64 = lax.broadcasted_iota(jnp.int32, (CHUNK, LANES), 1)
    left = lane64 < CHUNK
    col_in_chunk = jnp.bitwise_and(lane64, CHUNK - 1)
    tri_p = row64 >= col_in_chunk
    strict_p = row64 > col_in_chunk
    eye_p = jnp.where(row64 == col_in_chunk, 1.0, 0.0).astype(f32)

    def block_diag(z):
        zero = jnp.zeros_like(z)
        return jnp.concatenate([jnp.where(left, z, zero), jnp.where(left, zero, z)], axis=0)

    def pair_matmul(y_hi, y_lo, zbd_hi, zbd_lo):
        return (jnp.dot(jnp.concatenate([y_hi, y_lo], axis=1), jnp.concatenate([zbd_hi, zbd_hi], axis=0),
                        preferred_element_type=f32)
                + jnp.dot(y_hi, zbd_lo, preferred_element_type=f32))

    nw = nw_ref[...]
    heads = range(DN_HEADS)
    base = TAIL - (CONV_K - 1)

    def pair_block(pb):
        r0 = pb * PAIR
        ba = p_ref[r0:r0 + PAIR, BA_COL:BA_COL + LANES]
        beta_all = _sigmoid(ba)
        parts = _split3(-jnp.exp(alog_ref[...]) * _softplus(ba + dtb_ref[...]))
        gcol = sum(jnp.dot(tril_bd, pt, preferred_element_type=f32) for pt in parts)
        grow = sum(_tn_dot(pt, triu_bd) for pt in parts)
        yield

        qkv = []
        for grp in range(3):
            outs = []
            for h in heads:
                col = grp * DN_WIDTH + h * DN_DK
                acc = xe_ref[base + r0:base + r0 + PAIR, col:col + DN_DK] * cw_ref[0:1, col:col + DN_DK]
                for j in range(1, CONV_K):
                    acc = acc + (xe_ref[base + r0 + j:base + r0 + j + PAIR, col:col + DN_DK]
                                 * cw_ref[j:j + 1, col:col + DN_DK])
                y = acc * _sigmoid(acc)
                if grp < 2:
                    y = y * lax.rsqrt(jnp.sum(y * y, axis=-1, keepdims=True) + RMS_EPS)
                outs.append(y * (DN_DK ** -0.5) if grp == 0 else y)
                yield
            qkv.append(outs)
        qs, ks, vs = qkv

        kbetas, qgs, kdecs, rstacks, decays, egl = [], [], [], [], [], []
        for h in heads:
            gc = jnp.broadcast_to(gcol[:, DN_HEADS + h:DN_HEADS + h + 1], (PAIR, LANES))
            beta = jnp.broadcast_to(beta_all[:, h:h + 1], (PAIR, LANES))
            eg = jnp.exp(gc)
            glast = jnp.where(first_rows, gc[CHUNK - 1:CHUNK, :], gc[PAIR - 1:PAIR, :])
            kbeta = ks[h] * beta
            kbetas.append(kbeta)
            qgs.append(qs[h] * eg)
            kdecs.append((ks[h] * jnp.exp(glast - gc)).astype(bf16))
            rstacks.append(jnp.concatenate([kbeta * eg, vs[h] * beta], axis=1).astype(bf16))
            gdiff = jnp.where(left, gc[:CHUNK], gc[CHUNK:]) - grow[DN_HEADS + h:DN_HEADS + h + 1, :]
            decays.append(jnp.where(tri_p, jnp.exp(jnp.where(tri_p, gdiff, 0.0)), 0.0))
            egl.append((jnp.exp(gc[CHUNK - 1:CHUNK, :]), jnp.exp(gc[PAIR - 1:PAIR, :])))
            yield

        lps, qkms = [], []
        for h in heads:
            kk = _nt_dot(jnp.concatenate([kbetas[h], qs[h]], axis=0).astype(bf16), ks[h].astype(bf16))
            lps.append(jnp.where(strict_p, jnp.where(left, kk[0:CHUNK], kk[CHUNK:PAIR]) * decays[h], 0.0))
            qkms.append(jnp.where(tri_p, jnp.where(left, kk[PAIR:PAIR + CHUNK], kk[PAIR + CHUNK:]) * decays[h], 0.0)
                        .astype(bf16))
            yield

        pw = [_split2(lp) for lp in lps]
        pw_bd = [(block_diag(hi), block_diag(lo)) for hi, lo in pw]
        ts = [eye_p - lp for lp in lps]
        for _ in range(5):
            pw = [_split2(pair_matmul(*pw[h], *pw_bd[h])) for h in heads]
            pw_bd = [(block_diag(hi), block_diag(lo)) for hi, lo in pw]
            yield
            ts = [ts[h] + pair_matmul(*_split2(ts[h]), *pw_bd[h]) for h in heads]
            yield

        zero_p = jnp.zeros((CHUNK, LANES), bf16)
        mq, bo = [], []
        for h in heads:
            t16 = ts[h].astype(bf16)
            halves = (jnp.where(left, t16, zero_p), jnp.where(left, zero_p, t16))
            qk_halves = (jnp.where(left, qkms[h], zero_p), jnp.where(left, zero_p, qkms[h]))
            wus = [jnp.dot(th, rstacks[h], preferred_element_type=f32).astype(bf16) for th in halves]
            wu_stack = jnp.concatenate(wus, axis=0)
            mq_h, bo_h = [], []
            for c in range(2):
                cs = slice(c * CHUNK, (c + 1) * CHUNK)
                kw = _tn_dot(kdecs[h][cs], wus[c])
                qw = jnp.dot(qk_halves[c], wu_stack, preferred_element_type=f32)
                mq_h.append(jnp.concatenate([kw[:, :DN_DK], qgs[h][cs] - qw[:, :DN_DK]], axis=0).astype(bf16))
                bo_h.append((kw[:, DN_DK:], qw[:, DN_DK:]))
            mq.append(mq_h)
            bo.append(bo_h)
            yield

        prepared[pb] = (mq, bo, egl)

    states = [state_ref[h] for h in heads]

    def sequential(pbs):
        for pb in pbs:
            mq, bo, egl = prepared[pb]
            for c in range(2):
                for h in heads:
                    ms = jnp.dot(mq[h][c], states[h].astype(bf16), preferred_element_type=f32)
                    b_c, o_c = bo[h][c]
                    o = ms[DN_DK:] + o_c
                    states[h] = states[h] * egl[h][c] - ms[:DN_DK] + b_c
                    o = o * lax.rsqrt(jnp.mean(o * o, axis=-1, keepdims=True) + RMS_EPS) * nw
                    rc = pb * PAIR + c * CHUNK
                    z = p_ref[rc:rc + CHUNK, 3 * DN_WIDTH + h * DN_DV:3 * DN_WIDTH + (h + 1) * DN_DV]
                    o_ref[rc:rc + CHUNK, h * DN_DV:(h + 1) * DN_DV] = o * (z * _sigmoid(z))
                yield

    prepared = [None] * n_pb
    pending = iter(())
    for g0 in range(0, n_pb, DN_GROUP):
        group = range(g0, min(g0 + DN_GROUP, n_pb))
        pipes = [pair_block(pb) for pb in group]
        slot = 0
        while pipes:
            pipes = [pipe for pipe in pipes if next(pipe, "done") != "done"]
            slot += 1
            if slot >= DN_SEQ_START and (slot - DN_SEQ_START) % DN_SEQ_EVERY == 0:
                next(pending, None)
        for _ in pending:
            pass
        pending = sequential(group)
    for _ in pending:
        pass
    for h in heads:
        state_ref[h] = states[h]


def _deltanet(p_dn, conv_w8, alog_row, dtb_row, nw_row):
    s = p_dn.shape[0]
    fixed = lambda i: (0, 0)
    return pl.pallas_call(
        _dn_kernel,
        grid=(s // DN_ROWS,),
        in_specs=[
            pl.BlockSpec((DN_ROWS, DN_SLAB), lambda i: (i, 0)),
            pl.BlockSpec((8, CONV_CH), fixed),
            pl.BlockSpec((1, LANES), fixed),
            pl.BlockSpec((1, LANES), fixed),
            pl.BlockSpec((1, DN_DV), fixed),
        ],
        out_specs=pl.BlockSpec((DN_ROWS, DN_HEADS * DN_DV), lambda i: (i, 0)),
        out_shape=jax.ShapeDtypeStruct((s, DN_HEADS * DN_DV), f32),
        scratch_shapes=[
            pltpu.VMEM((DN_HEADS, DN_DK, DN_DV), f32),
            pltpu.VMEM((TAIL, CONV_CH), f32),
            pltpu.VMEM((TAIL + DN_ROWS, CONV_CH), f32),
        ],
        compiler_params=pltpu.CompilerParams(
            dimension_semantics=("arbitrary",), vmem_limit_bytes=VMEM_LIMIT),
        name="deltanet",
    )(p_dn, conv_w8, alog_row, dtb_row, nw_row)


N_PAIRS = SB_HEADS // 2
SB_WAVE = N_PAIRS
SB_QBLOCKS = 2


def _sb_kernel(q_ref, kc_ref, kp_ref, vc_ref, vp_ref, kv_hbm, o_ref, kbuf, vbuf, sem):
    for sub in range(SB_QBLOCKS):
        cur = pl.ds(sub * SB_BLOCK, SB_BLOCK)
        prev = pl.ds((sub - 1) * SB_BLOCK, SB_BLOCK)
        _sb_query_block(pl.program_id(0) * SB_QBLOCKS + sub, q_ref.at[cur],
                        kc_ref.at[cur], kc_ref.at[prev] if sub else kp_ref,
                        vc_ref.at[cur], vc_ref.at[prev] if sub else vp_ref,
                        kv_hbm, o_ref.at[cur], kbuf, vbuf, sem)


def _sb_query_block(qb, q_ref, kd_ref, k1_ref, vd_ref, v1_ref, kv_hbm, o_ref, kbuf, vbuf, sem):
    blk = SB_BLOCK
    row = lax.broadcasted_iota(jnp.int32, (blk, blk), 0)
    lane = lax.broadcasted_iota(jnp.int32, (blk, blk), 1)
    diag_mask = row > lane
    even = lane < SB_DH
    suffix = jnp.where(row > lane, 1.0, 0.0).astype(bf16)
    suffix2 = jnp.concatenate([suffix, suffix], axis=0)

    def split_heads(x):
        zero = jnp.zeros_like(x)
        return jnp.where(even, x, zero), jnp.where(even, zero, x)

    def suffix_sums(spm):
        hi = spm.astype(bf16)
        lo = (spm - hi.astype(f32)).astype(bf16)
        return jnp.dot(jnp.concatenate([hi, lo], axis=1), suffix2, preferred_element_type=f32)

    pairs = [slice(p * LANES, (p + 1) * LANES) for p in range(N_PAIRS)]
    k_refs = (kd_ref, k1_ref)
    v_refs = (vd_ref, v1_ref)
    pen1 = jnp.where(qb >= 1, 0.0, -SB_MASK_PENALTY).astype(f32)
    accs, carries = [], []
    for p0 in range(0, N_PAIRS, SB_WAVE):
        wave = range(p0, p0 + SB_WAVE)
        q_heads = {p: split_heads(q_ref[:, pairs[p]]) for p in wave}
        units = [(p, b, hh) for b in range(SB_STATIC_BLOCKS) for p in wave for hh in range(2)]
        z = {u: _nt_dot(q_heads[u[0]][u[2]], k_refs[u[1]][:, pairs[u[0]]]) for u in units}
        sp = {u: _softplus(z[u]) for u in units}
        spm = {u: (jnp.where(diag_mask, sp[u], 0.0) if u[1] == 0 else sp[u]) for u in units}
        logw = {u: z[u] - sp[u] - suffix_sums(spm[u]) for u in units}
        keep = {u: jnp.sum(spm[u], axis=1, keepdims=True) for u in units}
        att = {}
        for p, b, hh in units:
            if b == 0:
                att[p, b, hh] = jnp.where(diag_mask, jnp.exp(logw[p, b, hh]), 0.0).astype(bf16)
            else:
                att[p, b, hh] = jnp.exp(logw[p, b, hh] - (keep[p, 0, hh] + pen1)).astype(bf16)
        for p in wave:
            vals = jnp.concatenate([h for b in range(SB_STATIC_BLOCKS)
                                    for h in split_heads(v_refs[b][:, pairs[p]])], axis=0)
            lhs = jnp.concatenate([att[p, b, hh] for b in range(SB_STATIC_BLOCKS) for hh in range(2)], axis=1)
            accs.append(jnp.dot(lhs, vals, preferred_element_type=f32))
            carries.extend(keep[p, 0, hh] + keep[p, 1, hh] for hh in range(2))

    def live(cr):
        m = cr[0]
        for c in cr[1:]:
            m = jnp.minimum(m, c)
        return jnp.min(m) <= -SB_LOG_ZERO

    def cond(st):
        kb, more = st[0], st[1]
        return jnp.logical_and(kb >= 0, more)

    def body(st):
        kb = st[0]
        acc_l = list(st[2:2 + N_PAIRS])
        car_l = list(st[2 + N_PAIRS:])
        start = pl.multiple_of(kb * blk, blk)
        cpk = pltpu.make_async_copy(kv_hbm.at[pl.ds(start, blk), pl.ds(SB_WIDTH, SB_WIDTH)], kbuf, sem.at[0])
        cpv = pltpu.make_async_copy(kv_hbm.at[pl.ds(start, blk), pl.ds(2 * SB_WIDTH, SB_WIDTH)], vbuf, sem.at[1])
        cpk.start()
        cpv.start()
        cpk.wait()
        cpv.wait()
        new_car = []
        for p, ps in enumerate(pairs):
            atts = []
            for hh, qh in enumerate(split_heads(q_ref[:, ps])):
                zz = _nt_dot(qh, kbuf[:, ps])
                spp = _softplus(zz)
                carry = car_l[2 * p + hh]
                atts.append(jnp.exp(zz - spp - suffix_sums(spp) - carry).astype(bf16))
                new_car.append(carry + jnp.sum(spp, axis=1, keepdims=True))
            vals = jnp.concatenate(split_heads(vbuf[:, ps]), axis=0)
            acc_l[p] = acc_l[p] + jnp.dot(jnp.concatenate(atts, axis=1), vals, preferred_element_type=f32)
        return (kb - 1, live(new_car), *acc_l, *new_car)

    st = lax.while_loop(cond, body, (qb - SB_STATIC_BLOCKS, live(carries), *accs, *carries))
    for p, ps in enumerate(pairs):
        o_ref[:, ps] = st[2 + p]


def _sb_attention(p_sb):
    s = p_sb.shape[0]
    blk = SB_BLOCK

    step_rows = SB_QBLOCKS * blk

    def current(col):
        return pl.BlockSpec((step_rows, SB_WIDTH), lambda i: (i, col))

    def previous(col):
        return pl.BlockSpec((blk, SB_WIDTH), lambda i: (jnp.maximum(i * SB_QBLOCKS - 1, 0), col))

    return pl.pallas_call(
        _sb_kernel,
        grid=(s // step_rows,),
        in_specs=[current(0), current(1), previous(1), current(2), previous(2),
                  pl.BlockSpec(memory_space=pl.ANY)],
        out_specs=pl.BlockSpec((step_rows, SB_WIDTH), lambda i: (i, 0)),
        out_shape=jax.ShapeDtypeStruct((s, SB_WIDTH), f32),
        scratch_shapes=[
            pltpu.VMEM((blk, SB_WIDTH), bf16),
            pltpu.VMEM((blk, SB_WIDTH), bf16),
            pltpu.SemaphoreType.DMA((2,)),
        ],
        compiler_params=pltpu.CompilerParams(
            dimension_semantics=("arbitrary",), vmem_limit_bytes=VMEM_LIMIT),
        name="sb_attention",
    )(p_sb, p_sb, p_sb, p_sb, p_sb, p_sb)


MOE_TILE = 256
PAIRS_PER_GROUP = 6
N_CLASSES = N_GROUPS * PAIRS_PER_GROUP
X1E_W = D_MODEL + LANES
MERGE_ROWS = 512
MERGE_PARTS = 2


def _route(aff, sel):
    neg = -jnp.inf
    best = None
    for gidx in range(N_GROUPS):
        rows = slice(gidx * EXPERTS_PER_GROUP, (gidx + 1) * EXPERTS_PER_GROUP)
        sg = sel[rows]
        idx = lax.broadcasted_iota(jnp.int32, sg.shape, 0) + gidx * EXPERTS_PER_GROUP
        m1 = jnp.max(sg, axis=0, keepdims=True)
        i1 = jnp.min(jnp.where(sg == m1, idx, N_EXPERTS), axis=0, keepdims=True)
        sg2 = jnp.where(idx == i1, neg, sg)
        m2 = jnp.max(sg2, axis=0, keepdims=True)
        i2 = jnp.min(jnp.where(sg2 == m2, idx, N_EXPERTS), axis=0, keepdims=True)
        score = m1 + m2
        if best is None:
            best = (score, i1, i2)
        else:
            better = score > best[0]
            best = (jnp.where(better, score, best[0]),
                    jnp.where(better, i1, best[1]),
                    jnp.where(better, i2, best[2]))
    _, i1, i2 = best
    expert = lax.broadcasted_iota(jnp.int32, aff.shape, 0)
    w1 = jnp.sum(jnp.where(expert == i1, aff, 0.0), axis=0, keepdims=True)
    w2 = jnp.sum(jnp.where(expert == i2, aff, 0.0), axis=0, keepdims=True)
    denom = w1 + w2
    w1, w2 = w1 / denom, w2 / denom
    first_low = i1 < i2
    e_lo = jnp.minimum(i1, i2)
    e_hi = jnp.maximum(i1, i2)
    a = jnp.bitwise_and(e_lo, EXPERTS_PER_GROUP - 1)
    b = jnp.bitwise_and(e_hi, EXPERTS_PER_GROUP - 1)
    pair = jnp.where(a == 0, 0, jnp.where(a == 1, 3, 5)) + (b - a - 1)
    cls = lax.shift_right_logical(e_lo, 2) * PAIRS_PER_GROUP + pair
    return cls, jnp.where(first_low, w1, w2), jnp.where(first_low, w2, w1)


CLASS_ROWS = 32


def _merge_kernel(x_ref, oa_ref, ob_ref, g_ref, pa_ref, pb_ref, wo_ref, lg_ref, lb_ref, wr_ref, rb_ref,
                  x1e_ref, route_ref, cnt_ref, run_ref):
    step = pl.program_id(0)

    @pl.when(step == 0)
    def _init():
        run_ref[...] = jnp.zeros_like(run_ref)

    rows = x_ref.shape[0] // MERGE_PARTS
    parts = [slice(i * rows, (i + 1) * rows) for i in range(MERGE_PARTS)]
    a = [jnp.dot(oa_ref[p, :].astype(bf16), pa_ref[...], preferred_element_type=f32) for p in parts]
    b = [jnp.dot(ob_ref[p, :].astype(bf16), pb_ref[...], preferred_element_type=f32) for p in parts]
    merged = [(_sigmoid(g_ref[p, :D_MODEL].astype(f32)) * ai
               + _sigmoid(g_ref[p, D_MODEL:].astype(f32)) * bi).astype(bf16)
              for p, ai, bi in zip(parts, a, b)]
    mix = [jnp.dot(m, wo_ref[...], preferred_element_type=f32) for m in merged]
    x1 = [_layer_norm(DEEPNORM_ALPHA * x_ref[p, :] + mi, lg_ref[...], lb_ref[...]) for p, mi in zip(parts, mix)]
    x1b = [xi.astype(bf16) for xi in x1]
    for p, xi in zip(parts, x1):
        x1e_ref[p, :D_MODEL] = xi

    x_lo = [(xi - xbi.astype(f32)).astype(bf16) for xi, xbi in zip(x1, x1b)]
    t = [_nt_dot(wr_ref[...], xbi) for xbi in x1b]
    logits = [ti[:N_EXPERTS] + ti[N_EXPERTS:] + _nt_dot(wr_ref[:N_EXPERTS, :], xl) for ti, xl in zip(t, x_lo)]
    aff = [_sigmoid(lg) for lg in logits]
    routed = [_route(af, af + rb_ref[...]) for af in aff]

    rr = lax.broadcasted_iota(jnp.int32, (rows, rows), 0)
    cc = lax.broadcasted_iota(jnp.int32, (rows, rows), 1)
    triu = jnp.where(rr <= cc, 1.0, 0.0).astype(bf16)
    class_id = lax.broadcasted_iota(jnp.int32, (CLASS_ROWS, rows), 0)
    for p, (cls, w_lo, w_hi) in zip(parts, routed):
        onehot = class_id == cls
        prefix = jnp.dot(jnp.where(onehot, 1.0, 0.0).astype(bf16), triu, preferred_element_type=f32)
        run = run_ref[...]
        rank = (jnp.sum(jnp.where(onehot, prefix + run, 0.0), axis=0, keepdims=True) - 1.0).astype(jnp.int32)
        run_ref[...] = run + prefix[:, rows - 1:rows]
        pad = jnp.zeros((SUBLANES - 3, rows), jnp.int32)
        route_ref[0, :, p] = jnp.concatenate(
            [cls, lax.shift_right_logical(rank, 7), jnp.bitwise_and(rank, LANES - 1), pad], axis=0)
        w_rows = jnp.concatenate([w_lo, w_hi, jnp.zeros((LANES - 2, rows), f32)], axis=0)
        x1e_ref[p, D_MODEL:] = w_rows.T
    cnt_ref[...] = jnp.broadcast_to(run_ref[...], cnt_ref.shape).astype(jnp.int32)


def _merge(x, o_a, o_b, gates, p_a, p_b, w_out, ln_g, ln_b, wr_cat, r_bias):
    s = x.shape[0]
    tm = MERGE_ROWS
    row = lambda i: (i, 0)
    fixed = lambda i: (0, 0)
    return pl.pallas_call(
        _merge_kernel,
        grid=(s // tm,),
        in_specs=[
            pl.BlockSpec((tm, D_MODEL), row),
            pl.BlockSpec((tm, DN_HEADS * DN_DV), row),
            pl.BlockSpec((tm, SB_WIDTH), row),
            pl.BlockSpec((tm, GATE_SLAB), row),
            pl.BlockSpec((DN_HEADS * DN_DV, D_MODEL), fixed),
            pl.BlockSpec((SB_WIDTH, D_MODEL), fixed),
            pl.BlockSpec((D_MODEL, D_MODEL), fixed),
            pl.BlockSpec((1, D_MODEL), fixed),
            pl.BlockSpec((1, D_MODEL), fixed),
            pl.BlockSpec((2 * N_EXPERTS, D_MODEL), fixed),
            pl.BlockSpec((N_EXPERTS, 1), fixed),
        ],
        out_specs=[
            pl.BlockSpec((tm, X1E_W), row),
            pl.BlockSpec((1, SUBLANES, tm), lambda i: (i, 0, 0)),
            pl.BlockSpec((CLASS_ROWS, LANES), fixed),
        ],
        out_shape=[
            jax.ShapeDtypeStruct((s, X1E_W), f32),
            jax.ShapeDtypeStruct((s // tm, SUBLANES, tm), jnp.int32),
            jax.ShapeDtypeStruct((CLASS_ROWS, LANES), jnp.int32),
        ],
        scratch_shapes=[pltpu.VMEM((CLASS_ROWS, 1), f32)],
        compiler_params=pltpu.CompilerParams(
            dimension_semantics=("arbitrary",), vmem_limit_bytes=VMEM_LIMIT),
        name="merge_router",
    )(x, o_a, o_b, gates, p_a, p_b, w_out, ln_g, ln_b, wr_cat, r_bias)


def _class_experts():
    lo, hi = [], []
    for g in range(N_GROUPS):
        for a in range(EXPERTS_PER_GROUP):
            for b in range(a + 1, EXPERTS_PER_GROUP):
                lo.append(g * EXPERTS_PER_GROUP + a)
                hi.append(g * EXPERTS_PER_GROUP + b)
    return jnp.array(lo, jnp.int32), jnp.array(hi, jnp.int32)


def _n_tiles(s):
    return -(-(s + N_CLASSES * (MOE_TILE - 1)) // MOE_TILE)


def _route_tables(route, cnt, s):
    counts = cnt[:N_CLASSES, 0]
    padded = (counts + (MOE_TILE - 1)) // MOE_TILE * MOE_TILE
    ends = jnp.cumsum(padded)
    offs = ends - padded
    cls, rank_hi, rank_lo = (route[:, r, :].reshape(s) for r in range(3))
    dest = offs[cls] + rank_hi * LANES + rank_lo
    n_active = (ends[-1] // MOE_TILE).astype(jnp.int32)[None]
    tile_row = jnp.minimum(jnp.arange(_n_tiles(s), dtype=jnp.int32) * MOE_TILE, ends[-1] - 1)
    tile_cls = jnp.minimum(jnp.sum(tile_row[:, None] >= ends[None, :], axis=1), N_CLASSES - 1)
    e_lo, e_hi = _class_experts()
    tail = jnp.arange(s // MOE_TILE, _n_tiles(s), dtype=jnp.int32)
    pad_start = jnp.concatenate([ends - MOE_TILE, tail * MOE_TILE]).astype(jnp.int32)
    pad_valid = jnp.concatenate([padded > 0, tail >= n_active[0]]).astype(jnp.int32)
    return (dest.astype(jnp.int32), n_active, e_lo[tile_cls], e_hi[tile_cls], pad_start, pad_valid)


def _permute_kernel(dest_ref, pstart_ref, pvalid_ref, x_ref, xs_hbm, zero_ref, sem):
    step = pl.program_id(0)
    tp = x_ref.shape[0]

    @pl.when(step == 0)
    def _fill():
        zero_ref[...] = jnp.zeros_like(zero_ref)

        def fill_copy(c):
            start = pl.multiple_of(pstart_ref[c], MOE_TILE)
            return pltpu.make_async_copy(zero_ref, xs_hbm.at[pl.ds(start, MOE_TILE)], sem.at[1])

        for c in range(pstart_ref.shape[0]):
            @pl.when(pvalid_ref[c] != 0)
            def _start():
                fill_copy(c).start()
        for c in range(pstart_ref.shape[0]):
            @pl.when(pvalid_ref[c] != 0)
            def _wait():
                fill_copy(c).wait()

    base = step * tp

    def issue(g, carry):
        r8 = pl.multiple_of(g * SUBLANES, SUBLANES)
        for j in range(SUBLANES):
            d = dest_ref[base + r8 + j]
            pltpu.make_async_copy(x_ref.at[pl.ds(r8 + j, 1)], xs_hbm.at[pl.ds(d, 1)], sem.at[0]).start()
        return carry

    lax.fori_loop(0, tp // SUBLANES, issue, 0)
    pltpu.make_async_copy(x_ref, xs_hbm.at[pl.ds(0, tp)], sem.at[0]).wait()


def _permute(x1e, dest, pad_start, pad_valid, tp=512):
    s = x1e.shape[0]
    return pl.pallas_call(
        _permute_kernel,
        grid_spec=pltpu.PrefetchScalarGridSpec(
            num_scalar_prefetch=3,
            grid=(s // tp,),
            in_specs=[pl.BlockSpec((tp, X1E_W), lambda i, d, ps, pv: (i, 0))],
            out_specs=pl.BlockSpec(memory_space=pl.ANY),
            scratch_shapes=[pltpu.VMEM((MOE_TILE, X1E_W), f32), pltpu.SemaphoreType.DMA((2,))],
        ),
        out_shape=jax.ShapeDtypeStruct((_n_tiles(s) * MOE_TILE, X1E_W), f32),
        compiler_params=pltpu.CompilerParams(
            dimension_semantics=("arbitrary",), vmem_limit_bytes=VMEM_LIMIT),
        name="moe_permute",
    )(dest, pad_start, pad_valid, x1e)


def _moe_kernel(nact_ref, elo_ref, ehi_ref, xs_ref, wg0_ref, wu0_ref, wd0_ref, wg1_ref, wu1_ref, wd1_ref,
                ys_ref):
    @pl.when(pl.program_id(0) < nact_ref[0])
    def _tile():
        xb = xs_ref[:, :D_MODEL].astype(bf16)
        acc = None
        for col, (wg_ref, wu_ref, wd_ref) in enumerate(((wg0_ref, wu0_ref, wd0_ref),
                                                        (wg1_ref, wu1_ref, wd1_ref))):
            gate = jnp.dot(xb, wg_ref[0, 0], preferred_element_type=f32)
            up = jnp.dot(xb, wu_ref[0, 0], preferred_element_type=f32)
            hid = (gate * _sigmoid(gate)) * up * xs_ref[:, D_MODEL + col:D_MODEL + col + 1]
            part = jnp.dot(hid.astype(bf16), wd_ref[0, 0], preferred_element_type=f32)
            acc = part if acc is None else acc + part
        ys_ref[...] = acc

    @pl.when(pl.program_id(0) >= nact_ref[0])
    def _unused_tile():
        ys_ref[...] = jnp.zeros_like(ys_ref)


def _moe(xs, n_active, tile_lo, tile_hi, w_gate, w_up, w_down, layer):
    n_tiles = xs.shape[0] // MOE_TILE
    tile = lambda j, na, lo, hi: (jnp.minimum(j, na[0] - 1), 0)
    low = lambda j, na, lo, hi: (layer, lo[j], 0, 0)
    high = lambda j, na, lo, hi: (layer, hi[j], 0, 0)
    up_shape = (1, 1, D_MODEL, D_FF_EXPERT)
    down_shape = (1, 1, D_FF_EXPERT, D_MODEL)
    return pl.pallas_call(
        _moe_kernel,
        grid_spec=pltpu.PrefetchScalarGridSpec(
            num_scalar_prefetch=3,
            grid=(n_tiles,),
            in_specs=[
                pl.BlockSpec((MOE_TILE, X1E_W), tile),
                pl.BlockSpec(up_shape, low), pl.BlockSpec(up_shape, low), pl.BlockSpec(down_shape, low),
                pl.BlockSpec(up_shape, high), pl.BlockSpec(up_shape, high), pl.BlockSpec(down_shape, high),
            ],
            out_specs=pl.BlockSpec((MOE_TILE, D_MODEL), lambda j, na, lo, hi: (j, 0)),
        ),
        out_shape=jax.ShapeDtypeStruct((n_tiles * MOE_TILE, D_MODEL), f32),
        compiler_params=pltpu.CompilerParams(
            dimension_semantics=("arbitrary",), vmem_limit_bytes=VMEM_LIMIT),
        name="moe_ffn",
    )(n_active, tile_lo, tile_hi, xs, w_gate, w_up, w_down, w_gate, w_up, w_down)


def _unpermute_kernel(dest_ref, x1_ref, ys_hbm, lg_ref, lb_ref, out_ref, outb_ref, ybuf, sem):
    i = pl.program_id(0)
    n = pl.num_programs(0)
    tu = x1_ref.shape[0]

    def gather(tile, slot):
        base = tile * tu

        def issue(g, carry):
            r8 = pl.multiple_of(g * SUBLANES, SUBLANES)
            for j in range(SUBLANES):
                d = dest_ref[base + r8 + j]
                pltpu.make_async_copy(ys_hbm.at[pl.ds(d, 1)], ybuf.at[slot, pl.ds(r8 + j, 1)],
                                      sem.at[slot]).start()
            return carry

        lax.fori_loop(0, tu // SUBLANES, issue, 0)

    slot = lax.rem(i, 2)

    @pl.when(i == 0)
    def _first():
        gather(0, 0)

    @pl.when(i + 1 < n)
    def _next():
        gather(i + 1, 1 - slot)

    pltpu.make_async_copy(ys_hbm.at[pl.ds(0, tu)], ybuf.at[slot], sem.at[slot]).wait()
    y = _layer_norm(DEEPNORM_ALPHA * x1_ref[...] + ybuf[slot], lg_ref[...], lb_ref[...])
    out_ref[...] = y
    outb_ref[...] = y.astype(bf16)


def _unpermute(x1e, ys, dest, ln_g, ln_b, tu=256):
    s = x1e.shape[0]
    row = lambda i, d: (i, 0)
    fixed = lambda i, d: (0, 0)
    return pl.pallas_call(
        _unpermute_kernel,
        grid_spec=pltpu.PrefetchScalarGridSpec(
            num_scalar_prefetch=1,
            grid=(s // tu,),
            in_specs=[
                pl.BlockSpec((tu, D_MODEL), row),
                pl.BlockSpec(memory_space=pl.ANY),
                pl.BlockSpec((1, D_MODEL), fixed),
                pl.BlockSpec((1, D_MODEL), fixed),
            ],
            out_specs=[pl.BlockSpec((tu, D_MODEL), row), pl.BlockSpec((tu, D_MODEL), row)],
            scratch_shapes=[pltpu.VMEM((2, tu, D_MODEL), f32), pltpu.SemaphoreType.DMA((2,))],
        ),
        out_shape=[jax.ShapeDtypeStruct((s, D_MODEL), f32), jax.ShapeDtypeStruct((s, D_MODEL), bf16)],
        compiler_params=pltpu.CompilerParams(
            dimension_semantics=("arbitrary",), vmem_limit_bytes=VMEM_LIMIT),
        name="moe_unpermute_ln2",
    )(dest, x1e, ys, ln_g, ln_b)


def _pad_lanes(a, width=LANES):
    return jnp.pad(a, ((0, 0), (0, width - a.shape[1])))


def kernel(x, w_in, conv_w, dn_a_log, dn_dt_bias, dn_norm_w, p_a, p_b, w_out, ln1_g, ln1_b, w_router, router_bias, w_gate, w_up, w_down, ln2_g, ln2_b):
    bsz, s, _ = x.shape
    assert bsz == 1 and s % 512 == 0
    xf = x[0]
    xb = xf

    c_ba = 4 * DN_WIDTH
    c_sb = c_ba + 2 * DN_HEADS
    c_gate = c_sb + SB_SLAB
    wr_t = w_router.T
    wr_hi = wr_t.astype(bf16)
    wr_cat = jnp.concatenate([wr_hi, (wr_t - wr_hi.astype(f32)).astype(bf16)], axis=0)
    rb_col = router_bias[:, None]
    head_pad = ((0, 0), (DN_HEADS, LANES - 2 * DN_HEADS))
    col = jnp.arange(w_in.shape[-1])
    q_scale = jnp.where(jnp.logical_and(col >= c_sb, col < c_sb + SB_WIDTH), SB_DH ** -0.5, 1.0).astype(f32)
    w_in16 = (w_in * q_scale).astype(bf16)
    wg16, wu16, wd16 = w_gate.astype(bf16), w_up.astype(bf16), w_down.astype(bf16)

    for l in range(DEPTH):
        w = w_in16[l]
        w_dn = jnp.concatenate([w[:, :c_ba], _pad_lanes(w[:, c_ba:c_sb])], axis=1)
        w_sb = w[:, c_sb:c_gate]
        w_g = w[:, c_gate:]

        p_dn, p_sb, gates = _proj(xb, w_dn, w_sb, w_g)
        o_a = _deltanet(p_dn,
                        jnp.pad(conv_w[l], ((0, 8 - CONV_K), (0, 0))),
                        jnp.pad(dn_a_log[l][None, :], head_pad),
                        jnp.pad(dn_dt_bias[l][None, :], head_pad),
                        dn_norm_w[l][None, :])
        o_b = _sb_attention(p_sb)
        x1e, route, cnt = _merge(xf, o_a, o_b, gates,
                                p_a[l].astype(bf16), p_b[l].astype(bf16), w_out[l].astype(bf16),
                                ln1_g[l][None, :], ln1_b[l][None, :], wr_cat, rb_col)
        dest, n_active, tile_lo, tile_hi, pad_start, pad_valid = _route_tables(route, cnt, s)
        xs = _permute(x1e, dest, pad_start, pad_valid)
        ys = _moe(xs, n_active, tile_lo, tile_hi, wg16, wu16, wd16, l)
        xf, xb = _unpermute(x1e, ys, dest, ln2_g[l][None, :], ln2_b[l][None, :])
    return xf[None]
```

```python
import jax
import jax.numpy as jnp
from jax import lax
from jax.experimental import pallas as pl
from jax.experimental.pallas import tpu as pltpu

f32 = jnp.float32
bf16 = jnp.bfloat16
HIGHEST = lax.Precision.HIGHEST

D_MODEL = 1024
DEPTH = 2
CHUNK = 64
DN_HEADS = 4
DN_DK = 128
DN_DV = 128
CONV_K = 4
SB_HEADS = 8
SB_DH = 64
SB_BLOCK = 128
N_EXPERTS = 16
N_GROUPS = 4
EXPERTS_PER_GROUP = N_EXPERTS // N_GROUPS
D_FF_EXPERT = 512
LN_EPS = 1e-5
RMS_EPS = 1e-6
DEEPNORM_ALPHA = (2 * DEPTH) ** 0.25

DN_WIDTH = DN_HEADS * DN_DK
SB_WIDTH = SB_HEADS * SB_DH
CONV_CH = 3 * DN_WIDTH
LANES = 128
SUBLANES = 8
DN_SLAB = 4 * DN_WIDTH + LANES
BA_COL = 4 * DN_WIDTH
SB_SLAB = 3 * SB_WIDTH
GATE_SLAB = 2 * D_MODEL

SB_LOG_ZERO = -88.0
SB_STATIC_BLOCKS = 2
SB_MASK_PENALTY = -1e30

VMEM_LIMIT = 48 * 1024 * 1024


def _sigmoid(x):
    return 1.0 / (1.0 + jnp.exp(-x))


def _softplus(x):
    return jnp.maximum(x, 0.0) + jnp.log(1.0 + jnp.exp(-jnp.abs(x)))


def _nt_dot(a, b):
    return lax.dot_general(a, b, (((1,), (1,)), ((), ())), preferred_element_type=f32)


def _tn_dot(a, b):
    return lax.dot_general(a, b, (((0,), (0,)), ((), ())), preferred_element_type=f32)


def _layer_norm(y, g, b):
    mu = jnp.mean(y, axis=-1, keepdims=True)
    d = y - mu
    var = jnp.mean(d * d, axis=-1, keepdims=True)
    return d * lax.rsqrt(var + LN_EPS) * g + b


def _proj_kernel(x_ref, wdn_ref, wsb_ref, wg_ref, odn_ref, osb_ref, og_ref):
    x = x_ref[...].astype(bf16)
    odn_ref[...] = _nt_dot(x, wdn_ref[...])
    osb_ref[...] = _nt_dot(x, wsb_ref[...]).astype(bf16)
    og_ref[...] = _nt_dot(x, wg_ref[...]).astype(bf16)


def _proj(x, w_dn, w_sb, w_g, tm=512):
    s = x.shape[0]
    row = lambda i: (i, 0)
    fixed = lambda i: (0, 0)
    return pl.pallas_call(
        _proj_kernel,
        grid=(s // tm,),
        in_specs=[
            pl.BlockSpec((tm, D_MODEL), row),
            pl.BlockSpec((DN_SLAB, D_MODEL), fixed),
            pl.BlockSpec((SB_SLAB, D_MODEL), fixed),
            pl.BlockSpec((GATE_SLAB, D_MODEL), fixed),
        ],
        out_specs=[
            pl.BlockSpec((tm, DN_SLAB), row),
            pl.BlockSpec((tm, SB_SLAB), row),
            pl.BlockSpec((tm, GATE_SLAB), row),
        ],
        out_shape=[
            jax.ShapeDtypeStruct((s, DN_SLAB), f32),
            jax.ShapeDtypeStruct((s, SB_SLAB), bf16),
            jax.ShapeDtypeStruct((s, GATE_SLAB), bf16),
        ],
        compiler_params=pltpu.CompilerParams(
            dimension_semantics=("arbitrary",), vmem_limit_bytes=VMEM_LIMIT),
        name="proj",
    )(x, w_dn, w_sb, w_g)


DN_ROWS = 1024
PAIR = 2 * CHUNK
TAIL = 8
DN_GROUP = 2
DN_SEQ_START = 22
DN_SEQ_EVERY = 5


def _split2(x):
    hi = x.astype(bf16)
    return hi, (x - hi.astype(f32)).astype(bf16)


def _split3(x):
    hi = x.astype(bf16)
    r = x - hi.astype(f32)
    mid = r.astype(bf16)
    return hi, mid, (r - mid.astype(f32)).astype(bf16)


def _dn_kernel(p_ref, cw_ref, alog_ref, dtb_ref, nw_ref, o_ref, state_ref, tail_ref, xe_ref):
    step = pl.program_id(0)
    rows = p_ref.shape[0]
    n_pb = rows // PAIR

    @pl.when(step == 0)
    def _init():
        state_ref[...] = jnp.zeros_like(state_ref)
        tail_ref[...] = jnp.zeros_like(tail_ref)

    xe_ref[0:TAIL, :] = tail_ref[...]
    xe_ref[TAIL:TAIL + rows, :] = p_ref[:, 0:CONV_CH]
    tail_ref[...] = p_ref[rows - TAIL:rows, 0:CONV_CH]

    row128 = lax.broadcasted_iota(jnp.int32, (PAIR, LANES), 0)
    lane128 = lax.broadcasted_iota(jnp.int32, (PAIR, LANES), 1)
    same_chunk = (row128 >= CHUNK) == (lane128 >= CHUNK)
    tril_bd = jnp.where(jnp.logical_and(row128 >= lane128, same_chunk), 1.0, 0.0).astype(bf16)
    triu_bd = jnp.where(jnp.logical_and(row128 <= lane128, same_chunk), 1.0, 0.0).astype(bf16)
    first_rows = row128 < CHUNK
    row64 = lax.broadcasted_iota(jnp.int32, (CHUNK, LANES), 0)
    lane64 = lax.broadcasted_iota(jnp.int32, (CHUNK, LANES), 1)
    left = lane64 < CHUNK
    col_in_chunk = jnp.bitwise_and(lane64, CHUNK - 1)
    tri_p = row64 >= col_in_chunk
    strict_p = row64 > col_in_chunk
    eye_p = jnp.where(row64 == col_in_chunk, 1.0, 0.0).astype(f32)

    def block_diag(z):
        zero = jnp.zeros_like(z)
        return jnp.concatenate([jnp.where(left, z, zero), jnp.where(left, zero, z)], axis=0)

    def pair_matmul(y_hi, y_lo, zbd_hi, zbd_lo):
        return (jnp.dot(jnp.concatenate([y_hi, y_lo], axis=1), jnp.concatenate([zbd_hi, zbd_hi], axis=0),
                        preferred_element_type=f32)
                + jnp.dot(y_hi, zbd_lo, preferred_element_type=f32))

    nw = nw_ref[...]
    heads = range(DN_HEADS)
    base = TAIL - (CONV_K - 1)

    def pair_block(pb):
        r0 = pb * PAIR
        ba = p_ref[r0:r0 + PAIR, BA_COL:BA_COL + LANES]
        beta_all = _sigmoid(ba)
        parts = _split3(-jnp.exp(alog_ref[...]) * _softplus(ba + dtb_ref[...]))
        gcol = sum(jnp.dot(tril_bd, pt, preferred_element_type=f32) for pt in parts)
        grow = sum(_tn_dot(pt, triu_bd) for pt in parts)
        yield

        qkv = []
        for grp in range(3):
            outs = []
            for h in heads:
                col = grp * DN_WIDTH + h * DN_DK
                acc = xe_ref[base + r0:base + r0 + PAIR, col:col + DN_DK] * cw_ref[0:1, col:col + DN_DK]
                for j in range(1, CONV_K):
                    acc = acc + (xe_ref[base + r0 + j:base + r0 + j + PAIR, col:col + DN_DK]
                                 * cw_ref[j:j + 1, col:col + DN_DK])
                y = acc * _sigmoid(acc)
                if grp < 2:
                    y = y * lax.rsqrt(jnp.sum(y * y, axis=-1, keepdims=True) + RMS_EPS)
                outs.append(y * (DN_DK ** -0.5) if grp == 0 else y)
                yield
            qkv.append(outs)
        qs, ks, vs = qkv

        kbetas, qgs, kdecs, rstacks, decays, egl = [], [], [], [], [], []
        for h in heads:
            gc = jnp.broadcast_to(gcol[:, DN_HEADS + h:DN_HEADS + h + 1], (PAIR, LANES))
            beta = jnp.broadcast_to(beta_all[:, h:h + 1], (PAIR, LANES))
            eg = jnp.exp(gc)
            glast = jnp.where(first_rows, gc[CHUNK - 1:CHUNK, :], gc[PAIR - 1:PAIR, :])
            kbeta = ks[h] * beta
            kbetas.append(kbeta)
            qgs.append(qs[h] * eg)
            kdecs.append((ks[h] * jnp.exp(glast - gc)).astype(bf16))
            rstacks.append(jnp.concatenate([kbeta * eg, vs[h] * beta], axis=1).astype(bf16))
            gdiff = jnp.where(left, gc[:CHUNK], gc[CHUNK:]) - grow[DN_HEADS + h:DN_HEADS + h + 1, :]
            decays.append(jnp.where(tri_p, jnp.exp(jnp.where(tri_p, gdiff, 0.0)), 0.0))
            egl.append((jnp.exp(gc[CHUNK - 1:CHUNK, :]), jnp.exp(gc[PAIR - 1:PAIR, :])))
            yield

        lps, qkms = [], []
        for h in heads:
            kk = _nt_dot(jnp.concatenate([kbetas[h], qs[h]], axis=0).astype(bf16), ks[h].astype(bf16))
            lps.append(jnp.where(strict_p, jnp.where(left, kk[0:CHUNK], kk[CHUNK:PAIR]) * decays[h], 0.0))
            qkms.append(jnp.where(tri_p, jnp.where(left, kk[PAIR:PAIR + CHUNK], kk[PAIR + CHUNK:]) * decays[h], 0.0)
                        .astype(bf16))
            yield

        pw = [_split2(lp) for lp in lps]
        pw_bd = [(block_diag(hi), block_diag(lo)) for hi, lo in pw]
        ts = [eye_p - lp for lp in lps]
        for _ in range(5):
            pw = [_split2(pair_matmul(*pw[h], *pw_bd[h])) for h in heads]
            pw_bd = [(block_diag(hi), block_diag(lo)) for hi, lo in pw]
            yield
            ts = [ts[h] + pair_matmul(*_split2(ts[h]), *pw_bd[h]) for h in heads]
            yield

        zero_p = jnp.zeros((CHUNK, LANES), bf16)
        mq, bo = [], []
        for h in heads:
            t16 = ts[h].astype(bf16)
            halves = (jnp.where(left, t16, zero_p), jnp.where(left, zero_p, t16))
            qk_halves = (jnp.where(left, qkms[h], zero_p), jnp.where(left, zero_p, qkms[h]))
            wus = [jnp.dot(th, rstacks[h], preferred_element_type=f32).astype(bf16) for th in halves]
            wu_stack = jnp.concatenate(wus, axis=0)
            mq_h, bo_h = [], []
            for c in range(2):
                cs = slice(c * CHUNK, (c + 1) * CHUNK)
                kw = _tn_dot(kdecs[h][cs], wus[c])
                qw = jnp.dot(qk_halves[c], wu_stack, preferred_element_type=f32)
                mq_h.append(jnp.concatenate([kw[:, :DN_DK], qgs[h][cs] - qw[:, :DN_DK]], axis=0).astype(bf16))
                bo_h.append((kw[:, DN_DK:], qw[:, DN_DK:]))
            mq.append(mq_h)
            bo.append(bo_h)
            yield

        prepared[pb] = (mq, bo, egl)

    states = [state_ref[h] for h in heads]

    def sequential(pbs):
        for pb in pbs:
            mq, bo, egl = prepared[pb]
            for c in range(2):
                for h in heads:
                    ms = jnp.dot(mq[h][c], states[h].astype(bf16), preferred_element_type=f32)
                    b_c, o_c = bo[h][c]
                    o = ms[DN_DK:] + o_c
                    states[h] = states[h] * egl[h][c] - ms[:DN_DK] + b_c
                    o = o * lax.rsqrt(jnp.mean(o * o, axis=-1, keepdims=True) + RMS_EPS) * nw
                    rc = pb * PAIR + c * CHUNK
                    z = p_ref[rc:rc + CHUNK, 3 * DN_WIDTH + h * DN_DV:3 * DN_WIDTH + (h + 1) * DN_DV]
                    o_ref[rc:rc + CHUNK, h * DN_DV:(h + 1) * DN_DV] = o * (z * _sigmoid(z))
                yield

    prepared = [None] * n_pb
    pending = iter(())
    for g0 in range(0, n_pb, DN_GROUP):
        group = range(g0, min(g0 + DN_GROUP, n_pb))
        pipes = [pair_block(pb) for pb in group]
        slot = 0
        while pipes:
            pipes = [pipe for pipe in pipes if next(pipe, "done") != "done"]
            slot += 1
            if slot >= DN_SEQ_START and (slot - DN_SEQ_START) % DN_SEQ_EVERY == 0:
                next(pending, None)
        for _ in pending:
            pass
        pending = sequential(group)
    for _ in pending:
        pass
    for h in heads:
        state_ref[h] = states[h]


def _deltanet(p_dn, conv_w8, alog_row, dtb_row, nw_row):
    s = p_dn.shape[0]
    fixed = lambda i: (0, 0)
    return pl.pallas_call(
        _dn_kernel,
        grid=(s // DN_ROWS,),
        in_specs=[
            pl.BlockSpec((DN_ROWS, DN_SLAB), lambda i: (i, 0)),
            pl.BlockSpec((8, CONV_CH), fixed),
            pl.BlockSpec((1, LANES), fixed),
            pl.BlockSpec((1, LANES), fixed),
            pl.BlockSpec((1, DN_DV), fixed),
        ],
        out_specs=pl.BlockSpec((DN_ROWS, DN_HEADS * DN_DV), lambda i: (i, 0)),
        out_shape=jax.ShapeDtypeStruct((s, DN_HEADS * DN_DV), f32),
        scratch_shapes=[
            pltpu.VMEM((DN_HEADS, DN_DK, DN_DV), f32),
            pltpu.VMEM((TAIL, CONV_CH), f32),
            pltpu.VMEM((TAIL + DN_ROWS, CONV_CH), f32),
        ],
        compiler_params=pltpu.CompilerParams(
            dimension_semantics=("arbitrary",), vmem_limit_bytes=VMEM_LIMIT),
        name="deltanet",
    )(p_dn, conv_w8, alog_row, dtb_row, nw_row)


N_PAIRS = SB_HEADS // 2
SB_QBLOCKS = 2


def _sb_kernel(q_ref, kc_ref, kp_ref, vc_ref, vp_ref, kv_hbm, o_ref, kbuf, vbuf, sem):
    for sub in range(SB_QBLOCKS):
        cur = pl.ds(sub * SB_BLOCK, SB_BLOCK)
        prev = pl.ds((sub - 1) * SB_BLOCK, SB_BLOCK)
        _sb_query_block(pl.program_id(0) * SB_QBLOCKS + sub, q_ref.at[cur],
                        kc_ref.at[cur], kc_ref.at[prev] if sub else kp_ref,
                        vc_ref.at[cur], vc_ref.at[prev] if sub else vp_ref,
                        kv_hbm, o_ref.at[cur], kbuf, vbuf, sem)


def _sb_query_block(qb, q_ref, kd_ref, k1_ref, vd_ref, v1_ref, kv_hbm, o_ref, kbuf, vbuf, sem):
    blk = SB_BLOCK
    row = lax.broadcasted_iota(jnp.int32, (blk, blk), 0)
    lane = lax.broadcasted_iota(jnp.int32, (blk, blk), 1)
    diag_mask = row > lane
    even = lane < SB_DH
    suffix = jnp.where(row > lane, 1.0, 0.0).astype(bf16)
    suffix2 = jnp.concatenate([suffix, suffix], axis=0)

    def split_heads(x):
        zero = jnp.zeros_like(x)
        return jnp.where(even, x, zero), jnp.where(even, zero, x)

    def suffix_sums(spm):
        hi = spm.astype(bf16)
        lo = (spm - hi.astype(f32)).astype(bf16)
        return jnp.dot(jnp.concatenate([hi, lo], axis=1), suffix2, preferred_element_type=f32)

    pairs = [slice(p * LANES, (p + 1) * LANES) for p in range(N_PAIRS)]
    k_refs = (kd_ref, k1_ref)
    v_refs = (vd_ref, v1_ref)

    def fetch(kb):
        start = pl.multiple_of(kb * blk, blk)
        return (pltpu.make_async_copy(kv_hbm.at[pl.ds(start, blk), pl.ds(SB_WIDTH, SB_WIDTH)], kbuf, sem.at[0]),
                pltpu.make_async_copy(kv_hbm.at[pl.ds(start, blk), pl.ds(2 * SB_WIDTH, SB_WIDTH)], vbuf,
                                      sem.at[1]))

    has_older = qb >= SB_STATIC_BLOCKS

    @pl.when(has_older)
    def _prefetch():
        for cp in fetch(qb - SB_STATIC_BLOCKS):
            cp.start()

    def live(cr):
        m = cr[0]
        for c in cr[1:]:
            m = jnp.minimum(m, c)
        return jnp.min(m) <= -SB_LOG_ZERO

    pen1 = jnp.where(qb >= 1, 0.0, -SB_MASK_PENALTY).astype(f32)
    q_heads = [split_heads(q_ref[:, ps]) for ps in pairs]
    units = [(p, b, hh) for b in range(SB_STATIC_BLOCKS) for p in range(N_PAIRS) for hh in range(2)]
    z = {u: _nt_dot(q_heads[u[0]][u[2]], k_refs[u[1]][:, pairs[u[0]]]) for u in units}
    sp = {u: _softplus(z[u]) for u in units}
    spm = {u: (jnp.where(diag_mask, sp[u], 0.0) if u[1] == 0 else sp[u]) for u in units}
    logw = {u: z[u] - sp[u] - suffix_sums(spm[u]) for u in units}
    keep = {u: jnp.sum(spm[u], axis=1, keepdims=True) for u in units}
    carries = [keep[p, 0, hh] + keep[p, 1, hh] for p in range(N_PAIRS) for hh in range(2)]
    more = live(carries)
    att = {}
    for p, b, hh in units:
        if b == 0:
            att[p, b, hh] = jnp.where(diag_mask, jnp.exp(logw[p, b, hh]), 0.0).astype(bf16)
        else:
            att[p, b, hh] = jnp.exp(logw[p, b, hh] - (keep[p, 0, hh] + pen1)).astype(bf16)
    accs = []
    for p in range(N_PAIRS):
        vals = jnp.concatenate([h for b in range(SB_STATIC_BLOCKS)
                                for h in split_heads(v_refs[b][:, pairs[p]])], axis=0)
        lhs = jnp.concatenate([att[p, b, hh] for b in range(SB_STATIC_BLOCKS) for hh in range(2)], axis=1)
        accs.append(jnp.dot(lhs, vals, preferred_element_type=f32))

    @pl.when(has_older)
    def _landed():
        for cp in fetch(qb - SB_STATIC_BLOCKS):
            cp.wait()

    def cond(st):
        kb, go = st[0], st[1]
        return jnp.logical_and(kb >= 0, go)

    def body(st):
        kb = st[0]
        acc_l = list(st[2:2 + N_PAIRS])
        car_l = list(st[2 + N_PAIRS:])
        new_car = []
        for p, ps in enumerate(pairs):
            atts = []
            for hh, qh in enumerate(split_heads(q_ref[:, ps])):
                zz = _nt_dot(qh, kbuf[:, ps])
                spp = _softplus(zz)
                carry = car_l[2 * p + hh]
                atts.append(jnp.exp(zz - spp - suffix_sums(spp) - carry).astype(bf16))
                new_car.append(carry + jnp.sum(spp, axis=1, keepdims=True))
            vals = jnp.concatenate(split_heads(vbuf[:, ps]), axis=0)
            acc_l[p] = acc_l[p] + jnp.dot(jnp.concatenate(atts, axis=1), vals, preferred_element_type=f32)
        go = live(new_car)

        @pl.when(jnp.logical_and(go, kb >= 1))
        def _next_block():
            for cp in fetch(kb - 1):
                cp.start()
            for cp in fetch(kb - 1):
                cp.wait()

        return (kb - 1, go, *acc_l, *new_car)

    st = lax.while_loop(cond, body, (qb - SB_STATIC_BLOCKS, more, *accs, *carries))
    for p, ps in enumerate(pairs):
        o_ref[:, ps] = st[2 + p]


def _sb_attention(p_sb):
    s = p_sb.shape[0]
    blk = SB_BLOCK

    step_rows = SB_QBLOCKS * blk

    def current(col):
        return pl.BlockSpec((step_rows, SB_WIDTH), lambda i: (i, col))

    def previous(col):
        return pl.BlockSpec((blk, SB_WIDTH), lambda i: (jnp.maximum(i * SB_QBLOCKS - 1, 0), col))

    return pl.pallas_call(
        _sb_kernel,
        grid=(s // step_rows,),
        in_specs=[current(0), current(1), previous(1), current(2), previous(2),
                  pl.BlockSpec(memory_space=pl.ANY)],
        out_specs=pl.BlockSpec((step_rows, SB_WIDTH), lambda i: (i, 0)),
        out_shape=jax.ShapeDtypeStruct((s, SB_WIDTH), f32),
        scratch_shapes=[
            pltpu.VMEM((blk, SB_WIDTH), bf16),
            pltpu.VMEM((blk, SB_WIDTH), bf16),
            pltpu.SemaphoreType.DMA((2,)),
        ],
        compiler_params=pltpu.CompilerParams(
            dimension_semantics=("arbitrary",), vmem_limit_bytes=VMEM_LIMIT),
        name="sb_attention",
    )(p_sb, p_sb, p_sb, p_sb, p_sb, p_sb)


MOE_TILE = 256
PAIRS_PER_GROUP = 6
N_CLASSES = N_GROUPS * PAIRS_PER_GROUP
X1E_W = D_MODEL + LANES
MERGE_ROWS = 512
MERGE_PARTS = 2


def _route(aff, sel):
    neg = -jnp.inf
    best = None
    for gidx in range(N_GROUPS):
        rows = slice(gidx * EXPERTS_PER_GROUP, (gidx + 1) * EXPERTS_PER_GROUP)
        sg = sel[rows]
        idx = lax.broadcasted_iota(jnp.int32, sg.shape, 0) + gidx * EXPERTS_PER_GROUP
        m1 = jnp.max(sg, axis=0, keepdims=True)
        i1 = jnp.min(jnp.where(sg == m1, idx, N_EXPERTS), axis=0, keepdims=True)
        sg2 = jnp.where(idx == i1, neg, sg)
        m2 = jnp.max(sg2, axis=0, keepdims=True)
        i2 = jnp.min(jnp.where(sg2 == m2, idx, N_EXPERTS), axis=0, keepdims=True)
        score = m1 + m2
        if best is None:
            best = (score, i1, i2)
        else:
            better = score > best[0]
            best = (jnp.where(better, score, best[0]),
                    jnp.where(better, i1, best[1]),
                    jnp.where(better, i2, best[2]))
    _, i1, i2 = best
    expert = lax.broadcasted_iota(jnp.int32, aff.shape, 0)
    w1 = jnp.sum(jnp.where(expert == i1, aff, 0.0), axis=0, keepdims=True)
    w2 = jnp.sum(jnp.where(expert == i2, aff, 0.0), axis=0, keepdims=True)
    denom = w1 + w2
    w1, w2 = w1 / denom, w2 / denom
    first_low = i1 < i2
    e_lo = jnp.minimum(i1, i2)
    e_hi = jnp.maximum(i1, i2)
    a = jnp.bitwise_and(e_lo, EXPERTS_PER_GROUP - 1)
    b = jnp.bitwise_and(e_hi, EXPERTS_PER_GROUP - 1)
    pair = jnp.where(a == 0, 0, jnp.where(a == 1, 3, 5)) + (b - a - 1)
    cls = lax.shift_right_logical(e_lo, 2) * PAIRS_PER_GROUP + pair
    return cls, jnp.where(first_low, w1, w2), jnp.where(first_low, w2, w1)


CLASS_ROWS = 32


def _merge_kernel(x_ref, oa_ref, ob_ref, g_ref, pa_ref, pb_ref, wo_ref, lg_ref, lb_ref, wr_ref, rb_ref,
                  x1e_ref, route_ref, cnt_ref, run_ref):
    step = pl.program_id(0)

    @pl.when(step == 0)
    def _init():
        run_ref[...] = jnp.zeros_like(run_ref)

    rows = x_ref.shape[0] // MERGE_PARTS
    parts = [slice(i * rows, (i + 1) * rows) for i in range(MERGE_PARTS)]
    a = [jnp.dot(oa_ref[p, :].astype(bf16), pa_ref[...], preferred_element_type=f32) for p in parts]
    b = [jnp.dot(ob_ref[p, :].astype(bf16), pb_ref[...], preferred_element_type=f32) for p in parts]
    merged = [(_sigmoid(g_ref[p, :D_MODEL].astype(f32)) * ai
               + _sigmoid(g_ref[p, D_MODEL:].astype(f32)) * bi).astype(bf16)
              for p, ai, bi in zip(parts, a, b)]
    mix = [jnp.dot(m, wo_ref[...], preferred_element_type=f32) for m in merged]
    x1 = [_layer_norm(DEEPNORM_ALPHA * x_ref[p, :] + mi, lg_ref[...], lb_ref[...]) for p, mi in zip(parts, mix)]
    x1b = [xi.astype(bf16) for xi in x1]
    for p, xi in zip(parts, x1):
        x1e_ref[p, :D_MODEL] = xi

    x_lo = [(xi - xbi.astype(f32)).astype(bf16) for xi, xbi in zip(x1, x1b)]
    t = [_nt_dot(wr_ref[...], xbi) for xbi in x1b]
    logits = [ti[:N_EXPERTS] + ti[N_EXPERTS:] + _nt_dot(wr_ref[:N_EXPERTS, :], xl) for ti, xl in zip(t, x_lo)]
    aff = [_sigmoid(lg) for lg in logits]
    routed = [_route(af, af + rb_ref[...]) for af in aff]

    rr = lax.broadcasted_iota(jnp.int32, (rows, rows), 0)
    cc = lax.broadcasted_iota(jnp.int32, (rows, rows), 1)
    triu = jnp.where(rr <= cc, 1.0, 0.0).astype(bf16)
    class_id = lax.broadcasted_iota(jnp.int32, (CLASS_ROWS, rows), 0)
    for p, (cls, w_lo, w_hi) in zip(parts, routed):
        onehot = class_id == cls
        prefix = jnp.dot(jnp.where(onehot, 1.0, 0.0).astype(bf16), triu, preferred_element_type=f32)
        run = run_ref[...]
        rank = (jnp.sum(jnp.where(onehot, prefix + run, 0.0), axis=0, keepdims=True) - 1.0).astype(jnp.int32)
        run_ref[...] = run + prefix[:, rows - 1:rows]
        pad = jnp.zeros((SUBLANES - 3, rows), jnp.int32)
        route_ref[0, :, p] = jnp.concatenate(
            [cls, lax.shift_right_logical(rank, 7), jnp.bitwise_and(rank, LANES - 1), pad], axis=0)
        w_rows = jnp.concatenate([w_lo, w_hi, jnp.zeros((LANES - 2, rows), f32)], axis=0)
        x1e_ref[p, D_MODEL:] = w_rows.T
    cnt_ref[...] = jnp.broadcast_to(run_ref[...], cnt_ref.shape).astype(jnp.int32)


def _merge(x, o_a, o_b, gates, p_a, p_b, w_out, ln_g, ln_b, wr_cat, r_bias):
    s = x.shape[0]
    tm = MERGE_ROWS
    row = lambda i: (i, 0)
    fixed = lambda i: (0, 0)
    return pl.pallas_call(
        _merge_kernel,
        grid=(s // tm,),
        in_specs=[
            pl.BlockSpec((tm, D_MODEL), row),
            pl.BlockSpec((tm, DN_HEADS * DN_DV), row),
            pl.BlockSpec((tm, SB_WIDTH), row),
            pl.BlockSpec((tm, GATE_SLAB), row),
            pl.BlockSpec((DN_HEADS * DN_DV, D_MODEL), fixed),
            pl.BlockSpec((SB_WIDTH, D_MODEL), fixed),
            pl.BlockSpec((D_MODEL, D_MODEL), fixed),
            pl.BlockSpec((1, D_MODEL), fixed),
            pl.BlockSpec((1, D_MODEL), fixed),
            pl.BlockSpec((2 * N_EXPERTS, D_MODEL), fixed),
            pl.BlockSpec((N_EXPERTS, 1), fixed),
        ],
        out_specs=[
            pl.BlockSpec((tm, X1E_W), row),
            pl.BlockSpec((1, SUBLANES, tm), lambda i: (i, 0, 0)),
            pl.BlockSpec((CLASS_ROWS, LANES), fixed),
        ],
        out_shape=[
            jax.ShapeDtypeStruct((s, X1E_W), f32),
            jax.ShapeDtypeStruct((s // tm, SUBLANES, tm), jnp.int32),
            jax.ShapeDtypeStruct((CLASS_ROWS, LANES), jnp.int32),
        ],
        scratch_shapes=[pltpu.VMEM((CLASS_ROWS, 1), f32)],
        compiler_params=pltpu.CompilerParams(
            dimension_semantics=("arbitrary",), vmem_limit_bytes=VMEM_LIMIT),
        name="merge_router",
    )(x, o_a, o_b, gates, p_a, p_b, w_out, ln_g, ln_b, wr_cat, r_bias)


def _class_experts():
    lo, hi = [], []
    for g in range(N_GROUPS):
        for a in range(EXPERTS_PER_GROUP):
            for b in range(a + 1, EXPERTS_PER_GROUP):
                lo.append(g * EXPERTS_PER_GROUP + a)
                hi.append(g * EXPERTS_PER_GROUP + b)
    return jnp.array(lo, jnp.int32), jnp.array(hi, jnp.int32)


def _n_tiles(s):
    return -(-(s + N_CLASSES * (MOE_TILE - 1)) // MOE_TILE)


def _route_tables(route, cnt, s):
    counts = cnt[:N_CLASSES, 0]
    padded = (counts + (MOE_TILE - 1)) // MOE_TILE * MOE_TILE
    ends = jnp.cumsum(padded)
    offs = ends - padded
    cls, rank_hi, rank_lo = (route[:, r, :].reshape(s) for r in range(3))
    dest = offs[cls] + rank_hi * LANES + rank_lo
    n_active = (ends[-1] // MOE_TILE).astype(jnp.int32)[None]
    tile_row = jnp.minimum(jnp.arange(_n_tiles(s), dtype=jnp.int32) * MOE_TILE, ends[-1] - 1)
    tile_cls = jnp.minimum(jnp.sum(tile_row[:, None] >= ends[None, :], axis=1), N_CLASSES - 1)
    e_lo, e_hi = _class_experts()
    tail = jnp.arange(s // MOE_TILE, _n_tiles(s), dtype=jnp.int32)
    pad_start = jnp.concatenate([ends - MOE_TILE, tail * MOE_TILE]).astype(jnp.int32)
    pad_valid = jnp.concatenate([padded > 0, tail >= n_active[0]]).astype(jnp.int32)
    return (dest.astype(jnp.int32), n_active, e_lo[tile_cls], e_hi[tile_cls], pad_start, pad_valid)


def _permute_kernel(dest_ref, pstart_ref, pvalid_ref, x_ref, xs_hbm, zero_ref, sem):
    step = pl.program_id(0)
    tp = x_ref.shape[0]

    @pl.when(step == 0)
    def _fill():
        zero_ref[...] = jnp.zeros_like(zero_ref)

        def fill_copy(c):
            start = pl.multiple_of(pstart_ref[c], MOE_TILE)
            return pltpu.make_async_copy(zero_ref, xs_hbm.at[pl.ds(start, MOE_TILE)], sem.at[1])

        for c in range(pstart_ref.shape[0]):
            @pl.when(pvalid_ref[c] != 0)
            def _start():
                fill_copy(c).start()
        for c in range(pstart_ref.shape[0]):
            @pl.when(pvalid_ref[c] != 0)
            def _wait():
                fill_copy(c).wait()

    base = step * tp

    def issue(g, carry):
        r8 = pl.multiple_of(g * SUBLANES, SUBLANES)
        for j in range(SUBLANES):
            d = dest_ref[base + r8 + j]
            pltpu.make_async_copy(x_ref.at[pl.ds(r8 + j, 1)], xs_hbm.at[pl.ds(d, 1)], sem.at[0]).start()
        return carry

    lax.fori_loop(0, tp // SUBLANES, issue, 0)
    pltpu.make_async_copy(x_ref, xs_hbm.at[pl.ds(0, tp)], sem.at[0]).wait()


def _permute(x1e, dest, pad_start, pad_valid, tp=512):
    s = x1e.shape[0]
    return pl.pallas_call(
        _permute_kernel,
        grid_spec=pltpu.PrefetchScalarGridSpec(
            num_scalar_prefetch=3,
            grid=(s // tp,),
            in_specs=[pl.BlockSpec((tp, X1E_W), lambda i, d, ps, pv: (i, 0))],
            out_specs=pl.BlockSpec(memory_space=pl.ANY),
            scratch_shapes=[pltpu.VMEM((MOE_TILE, X1E_W), f32), pltpu.SemaphoreType.DMA((2,))],
        ),
        out_shape=jax.ShapeDtypeStruct((_n_tiles(s) * MOE_TILE, X1E_W), f32),
        compiler_params=pltpu.CompilerParams(
            dimension_semantics=("arbitrary",), vmem_limit_bytes=VMEM_LIMIT),
        name="moe_permute",
    )(dest, pad_start, pad_valid, x1e)


def _moe_kernel(nact_ref, elo_ref, ehi_ref, xs_ref, wg0_ref, wu0_ref, wd0_ref, wg1_ref, wu1_ref, wd1_ref,
                ys_ref):
    @pl.when(pl.program_id(0) < nact_ref[0])
    def _tile():
        xb = xs_ref[:, :D_MODEL].astype(bf16)
        acc = None
        for col, (wg_ref, wu_ref, wd_ref) in enumerate(((wg0_ref, wu0_ref, wd0_ref),
                                                        (wg1_ref, wu1_ref, wd1_ref))):
            gate = jnp.dot(xb, wg_ref[0, 0], preferred_element_type=f32)
            up = jnp.dot(xb, wu_ref[0, 0], preferred_element_type=f32)
            hid = (gate * _sigmoid(gate)) * up * xs_ref[:, D_MODEL + col:D_MODEL + col + 1]
            part = jnp.dot(hid.astype(bf16), wd_ref[0, 0], preferred_element_type=f32)
            acc = part if acc is None else acc + part
        ys_ref[...] = acc

    @pl.when(pl.program_id(0) >= nact_ref[0])
    def _unused_tile():
        ys_ref[...] = jnp.zeros_like(ys_ref)


def _moe(xs, n_active, tile_lo, tile_hi, w_gate, w_up, w_down, layer):
    n_tiles = xs.shape[0] // MOE_TILE
    tile = lambda j, na, lo, hi: (jnp.minimum(j, na[0] - 1), 0)
    low = lambda j, na, lo, hi: (layer, lo[j], 0, 0)
    high = lambda j, na, lo, hi: (layer, hi[j], 0, 0)
    up_shape = (1, 1, D_MODEL, D_FF_EXPERT)
    down_shape = (1, 1, D_FF_EXPERT, D_MODEL)
    return pl.pallas_call(
        _moe_kernel,
        grid_spec=pltpu.PrefetchScalarGridSpec(
            num_scalar_prefetch=3,
            grid=(n_tiles,),
            in_specs=[
                pl.BlockSpec((MOE_TILE, X1E_W), tile),
                pl.BlockSpec(up_shape, low), pl.BlockSpec(up_shape, low), pl.BlockSpec(down_shape, low),
                pl.BlockSpec(up_shape, high), pl.BlockSpec(up_shape, high), pl.BlockSpec(down_shape, high),
            ],
            out_specs=pl.BlockSpec((MOE_TILE, D_MODEL), lambda j, na, lo, hi: (j, 0)),
        ),
        out_shape=jax.ShapeDtypeStruct((n_tiles * MOE_TILE, D_MODEL), f32),
        compiler_params=pltpu.CompilerParams(
            dimension_semantics=("arbitrary",), vmem_limit_bytes=VMEM_LIMIT),
        name="moe_ffn",
    )(n_active, tile_lo, tile_hi, xs, w_gate, w_up, w_down, w_gate, w_up, w_down)


def _unpermute_kernel(dest_ref, x1_ref, ys_hbm, lg_ref, lb_ref, out_ref, outb_ref, ybuf, sem):
    i = pl.program_id(0)
    n = pl.num_programs(0)
    tu = x1_ref.shape[0]

    def gather(tile, slot):
        base = tile * tu

        def issue(g, carry):
            r8 = pl.multiple_of(g * SUBLANES, SUBLANES)
            for j in range(SUBLANES):
                d = dest_ref[base + r8 + j]
                pltpu.make_async_copy(ys_hbm.at[pl.ds(d, 1)], ybuf.at[slot, pl.ds(r8 + j, 1)],
                                      sem.at[slot]).start()
            return carry

        lax.fori_loop(0, tu // SUBLANES, issue, 0)

    slot = lax.rem(i, 2)

    @pl.when(i == 0)
    def _first():
        gather(0, 0)

    @pl.when(i + 1 < n)
    def _next():
        gather(i + 1, 1 - slot)

    pltpu.make_async_copy(ys_hbm.at[pl.ds(0, tu)], ybuf.at[slot], sem.at[slot]).wait()
    y = _layer_norm(DEEPNORM_ALPHA * x1_ref[...] + ybuf[slot], lg_ref[...], lb_ref[...])
    out_ref[...] = y
    outb_ref[...] = y.astype(bf16)


def _unpermute(x1e, ys, dest, ln_g, ln_b, tu=256):
    s = x1e.shape[0]
    row = lambda i, d: (i, 0)
    fixed = lambda i, d: (0, 0)
    return pl.pallas_call(
        _unpermute_kernel,
        grid_spec=pltpu.PrefetchScalarGridSpec(
            num_scalar_prefetch=1,
            grid=(s // tu,),
            in_specs=[
                pl.BlockSpec((tu, D_MODEL), row),
                pl.BlockSpec(memory_space=pl.ANY),
                pl.BlockSpec((1, D_MODEL), fixed),
                pl.BlockSpec((1, D_MODEL), fixed),
            ],
            out_specs=[pl.BlockSpec((tu, D_MODEL), row), pl.BlockSpec((tu, D_MODEL), row)],
            scratch_shapes=[pltpu.VMEM((2, tu, D_MODEL), f32), pltpu.SemaphoreType.DMA((2,))],
        ),
        out_shape=[jax.ShapeDtypeStruct((s, D_MODEL), f32), jax.ShapeDtypeStruct((s, D_MODEL), bf16)],
        compiler_params=pltpu.CompilerParams(
            dimension_semantics=("arbitrary",), vmem_limit_bytes=VMEM_LIMIT),
        name="moe_unpermute_ln2",
    )(dest, x1e, ys, ln_g, ln_b)


def _pad_lanes(a, width=LANES):
    return jnp.pad(a, ((0, 0), (0, width - a.shape[1])))


def kernel(x, w_in, conv_w, dn_a_log, dn_dt_bias, dn_norm_w, p_a, p_b, w_out, ln1_g, ln1_b, w_router, router_bias, w_gate, w_up, w_down, ln2_g, ln2_b):
    bsz, s, _ = x.shape
    assert bsz == 1 and s % 512 == 0
    xf = x[0]
    xb = xf

    c_ba = 4 * DN_WIDTH
    c_sb = c_ba + 2 * DN_HEADS
    c_gate = c_sb + SB_SLAB
    wr_t = w_router.T
    wr_hi = wr_t.astype(bf16)
    wr_cat = jnp.concatenate([wr_hi, (wr_t - wr_hi.astype(f32)).astype(bf16)], axis=0)
    rb_col = router_bias[:, None]
    head_pad = ((0, 0), (DN_HEADS, LANES - 2 * DN_HEADS))
    w_in_t = jnp.swapaxes(w_in, 1, 2)
    wg16, wu16, wd16 = w_gate.astype(bf16), w_up.astype(bf16), w_down.astype(bf16)

    for l in range(DEPTH):
        w = w_in_t[l]
        w_dn = jnp.concatenate([w[:c_ba], jnp.pad(w[c_ba:c_sb], ((0, LANES - 2 * DN_HEADS), (0, 0)))],
                               axis=0).astype(bf16)
        w_sb = jnp.concatenate([w[c_sb:c_sb + SB_WIDTH] * (SB_DH ** -0.5),
                                w[c_sb + SB_WIDTH:c_gate]], axis=0).astype(bf16)
        w_g = w[c_gate:].astype(bf16)

        p_dn, p_sb, gates = _proj(xb, w_dn, w_sb, w_g)
        o_a = _deltanet(p_dn,
                        jnp.pad(conv_w[l], ((0, 8 - CONV_K), (0, 0))),
                        jnp.pad(dn_a_log[l][None, :], head_pad),
                        jnp.pad(dn_dt_bias[l][None, :], head_pad),
                        dn_norm_w[l][None, :])
        o_b = _sb_attention(p_sb)
        x1e, route, cnt = _merge(xf, o_a, o_b, gates,
                                p_a[l].astype(bf16), p_b[l].astype(bf16), w_out[l].astype(bf16),
                                ln1_g[l][None, :], ln1_b[l][None, :], wr_cat, rb_col)
        dest, n_active, tile_lo, tile_hi, pad_start, pad_valid = _route_tables(route, cnt, s)
        xs = _permute(x1e, dest, pad_start, pad_valid)
        ys = _moe(xs, n_active, tile_lo, tile_hi, wg16, wu16, wd16, l)
        xf, xb = _unpermute(x1e, ys, dest, ln2_g[l][None, :], ln2_b[l][None, :])
    return xf[None]
```

```python
import jax
import jax.numpy as jnp
from jax import lax
from jax.experimental import pallas as pl
from jax.experimental.pallas import tpu as pltpu

f32 = jnp.float32
bf16 = jnp.bfloat16
HIGHEST = lax.Precision.HIGHEST

D_MODEL = 1024
DEPTH = 2
CHUNK = 64
DN_HEADS = 4
DN_DK = 128
DN_DV = 128
CONV_K = 4
SB_HEADS = 8
SB_DH = 64
SB_BLOCK = 128
N_EXPERTS = 16
N_GROUPS = 4
EXPERTS_PER_GROUP = N_EXPERTS // N_GROUPS
D_FF_EXPERT = 512
LN_EPS = 1e-5
RMS_EPS = 1e-6
DEEPNORM_ALPHA = (2 * DEPTH) ** 0.25

DN_WIDTH = DN_HEADS * DN_DK
SB_WIDTH = SB_HEADS * SB_DH
CONV_CH = 3 * DN_WIDTH
LANES = 128
SUBLANES = 8
DN_SLAB = 4 * DN_WIDTH + LANES
BA_COL = 4 * DN_WIDTH
SB_SLAB = 3 * SB_WIDTH
GATE_SLAB = 2 * D_MODEL

SB_LOG_ZERO = -88.0
SB_STATIC_BLOCKS = 2
SB_MASK_PENALTY = -1e30

VMEM_LIMIT = 48 * 1024 * 1024


def _sigmoid(x):
    return 1.0 / (1.0 + jnp.exp(-x))


def _softplus(x):
    return jnp.maximum(x, 0.0) + jnp.log(1.0 + jnp.exp(-jnp.abs(x)))


def _nt_dot(a, b):
    return lax.dot_general(a, b, (((1,), (1,)), ((), ())), preferred_element_type=f32)


def _tn_dot(a, b):
    return lax.dot_general(a, b, (((0,), (0,)), ((), ())), preferred_element_type=f32)


def _layer_norm(y, g, b):
    mu = jnp.mean(y, axis=-1, keepdims=True)
    d = y - mu
    var = jnp.mean(d * d, axis=-1, keepdims=True)
    return d * lax.rsqrt(var + LN_EPS) * g + b


def _proj_kernel(x_ref, wdn_ref, wsb_ref, wg_ref, odn_ref, osb_ref, og_ref):
    x = x_ref[...].astype(bf16)
    odn_ref[...] = _nt_dot(x, wdn_ref[...])
    osb_ref[...] = _nt_dot(x, wsb_ref[...]).astype(bf16)
    og_ref[...] = _nt_dot(x, wg_ref[...]).astype(bf16)


def _proj(x, w_dn, w_sb, w_g, tm=512):
    s = x.shape[0]
    row = lambda i: (i, 0)
    fixed = lambda i: (0, 0)
    return pl.pallas_call(
        _proj_kernel,
        grid=(s // tm,),
        in_specs=[
            pl.BlockSpec((tm, D_MODEL), row),
            pl.BlockSpec((DN_SLAB, D_MODEL), fixed),
            pl.BlockSpec((SB_SLAB, D_MODEL), fixed),
            pl.BlockSpec((GATE_SLAB, D_MODEL), fixed),
        ],
        out_specs=[
            pl.BlockSpec((tm, DN_SLAB), row),
            pl.BlockSpec((tm, SB_SLAB), row),
            pl.BlockSpec((tm, GATE_SLAB), row),
        ],
        out_shape=[
            jax.ShapeDtypeStruct((s, DN_SLAB), f32),
            jax.ShapeDtypeStruct((s, SB_SLAB), bf16),
            jax.ShapeDtypeStruct((s, GATE_SLAB), bf16),
        ],
        compiler_params=pltpu.CompilerParams(
            dimension_semantics=("arbitrary",), vmem_limit_bytes=VMEM_LIMIT),
        name="proj",
    )(x, w_dn, w_sb, w_g)


DN_ROWS = 1024
PAIR = 2 * CHUNK
TAIL = 8
DN_GROUP = 2
DN_SEQ_START = 22
DN_SEQ_EVERY = 5


def _split2(x):
    hi = x.astype(bf16)
    return hi, (x - hi.astype(f32)).astype(bf16)


def _split3(x):
    hi = x.astype(bf16)
    r = x - hi.astype(f32)
    mid = r.astype(bf16)
    return hi, mid, (r - mid.astype(f32)).astype(bf16)


def _dn_kernel(p_ref, cw_ref, alog_ref, dtb_ref, nw_ref, o_ref, state_ref, tail_ref, xe_ref):
    step = pl.program_id(0)
    rows = p_ref.shape[0]
    n_pb = rows // PAIR

    @pl.when(step == 0)
    def _init():
        state_ref[...] = jnp.zeros_like(state_ref)
        tail_ref[...] = jnp.zeros_like(tail_ref)

    xe_ref[0:TAIL, :] = tail_ref[...]
    xe_ref[TAIL:TAIL + rows, :] = p_ref[:, 0:CONV_CH]
    tail_ref[...] = p_ref[rows - TAIL:rows, 0:CONV_CH]

    row128 = lax.broadcasted_iota(jnp.int32, (PAIR, LANES), 0)
    lane128 = lax.broadcasted_iota(jnp.int32, (PAIR, LANES), 1)
    same_chunk = (row128 >= CHUNK) == (lane128 >= CHUNK)
    tril_bd = jnp.where(jnp.logical_and(row128 >= lane128, same_chunk), 1.0, 0.0).astype(bf16)
    triu_bd = jnp.where(jnp.logical_and(row128 <= lane128, same_chunk), 1.0, 0.0).astype(bf16)
    first_rows = row128 < CHUNK
    row64 = lax.broadcasted_iota(jnp.int32, (CHUNK, LANES), 0)
    lane64 = lax.broadcasted_iota(jnp.int32, (CHUNK, LANES), 1)
    left = lane64 < CHUNK
    col_in_chunk = jnp.bitwise_and(lane64, CHUNK - 1)
    tri_p = row64 >= col_in_chunk
    strict_p = row64 > col_in_chunk
    eye_p = jnp.where(row64 == col_in_chunk, 1.0, 0.0).astype(f32)

    def block_diag(z):
        zero = jnp.zeros_like(z)
        return jnp.concatenate([jnp.where(left, z, zero), jnp.where(left, zero, z)], axis=0)

    def pair_matmul(y_hi, y_lo, zbd_hi, zbd_lo):
        return (jnp.dot(jnp.concatenate([y_hi, y_lo], axis=1), jnp.concatenate([zbd_hi, zbd_hi], axis=0),
                        preferred_element_type=f32)
                + jnp.dot(y_hi, zbd_lo, preferred_element_type=f32))

    nw = nw_ref[...]
    heads = range(DN_HEADS)
    base = TAIL - (CONV_K - 1)

    def pair_block(pb):
        r0 = pb * PAIR
        ba = p_ref[r0:r0 + PAIR, BA_COL:BA_COL + LANES]
        beta_all = _sigmoid(ba)
        parts = _split3(-jnp.exp(alog_ref[...]) * _softplus(ba + dtb_ref[...]))
        gcol = sum(jnp.dot(tril_bd, pt, preferred_element_type=f32) for pt in parts)
        grow = sum(_tn_dot(pt, triu_bd) for pt in parts)
        yield

        qkv = []
        for grp in range(3):
            outs = []
            for h in heads:
                col = grp * DN_WIDTH + h * DN_DK
                acc = xe_ref[base + r0:base + r0 + PAIR, col:col + DN_DK] * cw_ref[0:1, col:col + DN_DK]
                for j in range(1, CONV_K):
                    acc = acc + (xe_ref[base + r0 + j:base + r0 + j + PAIR, col:col + DN_DK]
                                 * cw_ref[j:j + 1, col:col + DN_DK])
                y = acc * _sigmoid(acc)
                if grp < 2:
                    y = y * lax.rsqrt(jnp.sum(y * y, axis=-1, keepdims=True) + RMS_EPS)
                outs.append(y * (DN_DK ** -0.5) if grp == 0 else y)
                yield
            qkv.append(outs)
        qs, ks, vs = qkv

        kbetas, qgs, kdecs, rstacks, decays, egl = [], [], [], [], [], []
        for h in heads:
            gc = jnp.broadcast_to(gcol[:, DN_HEADS + h:DN_HEADS + h + 1], (PAIR, LANES))
            beta = jnp.broadcast_to(beta_all[:, h:h + 1], (PAIR, LANES))
            eg = jnp.exp(gc)
            glast = jnp.where(first_rows, gc[CHUNK - 1:CHUNK, :], gc[PAIR - 1:PAIR, :])
            kbeta = ks[h] * beta
            kbetas.append(kbeta)
            qgs.append(qs[h] * eg)
            kdecs.append((ks[h] * jnp.exp(glast - gc)).astype(bf16))
            rstacks.append(jnp.concatenate([kbeta * eg, vs[h] * beta], axis=1).astype(bf16))
            gdiff = jnp.where(left, gc[:CHUNK], gc[CHUNK:]) - grow[DN_HEADS + h:DN_HEADS + h + 1, :]
            decays.append(jnp.where(tri_p, jnp.exp(jnp.where(tri_p, gdiff, 0.0)), 0.0))
            egl.append((jnp.exp(gc[CHUNK - 1:CHUNK, :]), jnp.exp(gc[PAIR - 1:PAIR, :])))
            yield

        lps, qkms = [], []
        for h in heads:
            kk = _nt_dot(jnp.concatenate([kbetas[h], qs[h]], axis=0).astype(bf16), ks[h].astype(bf16))
            lps.append(jnp.where(strict_p, jnp.where(left, kk[0:CHUNK], kk[CHUNK:PAIR]) * decays[h], 0.0))
            qkms.append(jnp.where(tri_p, jnp.where(left, kk[PAIR:PAIR + CHUNK], kk[PAIR + CHUNK:]) * decays[h], 0.0)
                        .astype(bf16))
            yield

        pw = [_split2(lp) for lp in lps]
        pw_bd = [(block_diag(hi), block_diag(lo)) for hi, lo in pw]
        ts = [eye_p - lp for lp in lps]
        for _ in range(5):
            pw = [_split2(pair_matmul(*pw[h], *pw_bd[h])) for h in heads]
            pw_bd = [(block_diag(hi), block_diag(lo)) for hi, lo in pw]
            yield
            ts = [ts[h] + pair_matmul(*_split2(ts[h]), *pw_bd[h]) for h in heads]
            yield

        zero_p = jnp.zeros((CHUNK, LANES), bf16)
        mq, bo = [], []
        for h in heads:
            t16 = ts[h].astype(bf16)
            halves = (jnp.where(left, t16, zero_p), jnp.where(left, zero_p, t16))
            qk_halves = (jnp.where(left, qkms[h], zero_p), jnp.where(left, zero_p, qkms[h]))
            wus = [jnp.dot(th, rstacks[h], preferred_element_type=f32).astype(bf16) for th in halves]
            wu_stack = jnp.concatenate(wus, axis=0)
            mq_h, bo_h = [], []
            for c in range(2):
                cs = slice(c * CHUNK, (c + 1) * CHUNK)
                kw = _tn_dot(kdecs[h][cs], wus[c])
                qw = jnp.dot(qk_halves[c], wu_stack, preferred_element_type=f32)
                mq_h.append(jnp.concatenate([kw[:, :DN_DK], qgs[h][cs] - qw[:, :DN_DK]], axis=0).astype(bf16))
                bo_h.append((kw[:, DN_DK:], qw[:, DN_DK:]))
            mq.append(mq_h)
            bo.append(bo_h)
            yield

        prepared[pb] = (mq, bo, egl)

    states = [state_ref[h] for h in heads]

    def sequential(pbs):
        for pb in pbs:
            mq, bo, egl = prepared[pb]
            for c in range(2):
                for h in heads:
                    ms = jnp.dot(mq[h][c], states[h].astype(bf16), preferred_element_type=f32)
                    b_c, o_c = bo[h][c]
                    o = ms[DN_DK:] + o_c
                    states[h] = states[h] * egl[h][c] - ms[:DN_DK] + b_c
                    o = o * lax.rsqrt(jnp.mean(o * o, axis=-1, keepdims=True) + RMS_EPS) * nw
                    rc = pb * PAIR + c * CHUNK
                    z = p_ref[rc:rc + CHUNK, 3 * DN_WIDTH + h * DN_DV:3 * DN_WIDTH + (h + 1) * DN_DV]
                    o_ref[rc:rc + CHUNK, h * DN_DV:(h + 1) * DN_DV] = o * (z * _sigmoid(z))
                yield

    prepared = [None] * n_pb
    pending = iter(())
    for g0 in range(0, n_pb, DN_GROUP):
        group = range(g0, min(g0 + DN_GROUP, n_pb))
        pipes = [pair_block(pb) for pb in group]
        slot = 0
        while pipes:
            pipes = [pipe for pipe in pipes if next(pipe, "done") != "done"]
            slot += 1
            if slot >= DN_SEQ_START and (slot - DN_SEQ_START) % DN_SEQ_EVERY == 0:
                next(pending, None)
        for _ in pending:
            pass
        pending = sequential(group)
    for _ in pending:
        pass
    for h in heads:
        state_ref[h] = states[h]


def _deltanet(p_dn, conv_w8, alog_row, dtb_row, nw_row):
    s = p_dn.shape[0]
    fixed = lambda i: (0, 0)
    return pl.pallas_call(
        _dn_kernel,
        grid=(s // DN_ROWS,),
        in_specs=[
            pl.BlockSpec((DN_ROWS, DN_SLAB), lambda i: (i, 0)),
            pl.BlockSpec((8, CONV_CH), fixed),
            pl.BlockSpec((1, LANES), fixed),
            pl.BlockSpec((1, LANES), fixed),
            pl.BlockSpec((1, DN_DV), fixed),
        ],
        out_specs=pl.BlockSpec((DN_ROWS, DN_HEADS * DN_DV), lambda i: (i, 0)),
        out_shape=jax.ShapeDtypeStruct((s, DN_HEADS * DN_DV), f32),
        scratch_shapes=[
            pltpu.VMEM((DN_HEADS, DN_DK, DN_DV), f32),
            pltpu.VMEM((TAIL, CONV_CH), f32),
            pltpu.VMEM((TAIL + DN_ROWS, CONV_CH), f32),
        ],
        compiler_params=pltpu.CompilerParams(
            dimension_semantics=("arbitrary",), vmem_limit_bytes=VMEM_LIMIT),
        name="deltanet",
    )(p_dn, conv_w8, alog_row, dtb_row, nw_row)


N_PAIRS = SB_HEADS // 2
SB_QBLOCKS = 2


def _sb_kernel(q_ref, kc_ref, kp_ref, vc_ref, vp_ref, kv_hbm, o_ref, kbuf, vbuf, sem):
    for sub in range(SB_QBLOCKS):
        cur = pl.ds(sub * SB_BLOCK, SB_BLOCK)
        prev = pl.ds((sub - 1) * SB_BLOCK, SB_BLOCK)
        _sb_query_block(pl.program_id(0) * SB_QBLOCKS + sub, q_ref.at[cur],
                        kc_ref.at[cur], kc_ref.at[prev] if sub else kp_ref,
                        vc_ref.at[cur], vc_ref.at[prev] if sub else vp_ref,
                        kv_hbm, o_ref.at[cur], kbuf, vbuf, sem)


def _sb_query_block(qb, q_ref, kd_ref, k1_ref, vd_ref, v1_ref, kv_hbm, o_ref, kbuf, vbuf, sem):
    blk = SB_BLOCK
    row = lax.broadcasted_iota(jnp.int32, (blk, blk), 0)
    lane = lax.broadcasted_iota(jnp.int32, (blk, blk), 1)
    diag_mask = row > lane
    even = lane < SB_DH
    suffix = jnp.where(row > lane, 1.0, 0.0).astype(bf16)
    suffix2 = jnp.concatenate([suffix, suffix], axis=0)

    def split_heads(x):
        zero = jnp.zeros_like(x)
        return jnp.where(even, x, zero), jnp.where(even, zero, x)

    def suffix_sums(spm):
        hi = spm.astype(bf16)
        lo = (spm - hi.astype(f32)).astype(bf16)
        return jnp.dot(jnp.concatenate([hi, lo], axis=1), suffix2, preferred_element_type=f32)

    pairs = [slice(p * LANES, (p + 1) * LANES) for p in range(N_PAIRS)]
    k_refs = (kd_ref, k1_ref)
    v_refs = (vd_ref, v1_ref)

    def fetch(kb):
        start = pl.multiple_of(kb * blk, blk)
        return (pltpu.make_async_copy(kv_hbm.at[pl.ds(start, blk), pl.ds(SB_WIDTH, SB_WIDTH)], kbuf, sem.at[0]),
                pltpu.make_async_copy(kv_hbm.at[pl.ds(start, blk), pl.ds(2 * SB_WIDTH, SB_WIDTH)], vbuf,
                                      sem.at[1]))

    def live(cr):
        m = cr[0]
        for c in cr[1:]:
            m = jnp.minimum(m, c)
        return jnp.min(m) <= -SB_LOG_ZERO

    pen1 = jnp.where(qb >= 1, 0.0, -SB_MASK_PENALTY).astype(f32)
    q_heads = [split_heads(q_ref[:, ps]) for ps in pairs]
    units = [(p, b, hh) for b in range(SB_STATIC_BLOCKS) for p in range(N_PAIRS) for hh in range(2)]
    z = {u: _nt_dot(q_heads[u[0]][u[2]], k_refs[u[1]][:, pairs[u[0]]]) for u in units}
    sp = {u: _softplus(z[u]) for u in units}
    spm = {u: (jnp.where(diag_mask, sp[u], 0.0) if u[1] == 0 else sp[u]) for u in units}
    logw = {u: z[u] - sp[u] - suffix_sums(spm[u]) for u in units}
    keep = {u: jnp.sum(spm[u], axis=1, keepdims=True) for u in units}
    carries = [keep[p, 0, hh] + keep[p, 1, hh] for p in range(N_PAIRS) for hh in range(2)]
    att = {}
    for p, b, hh in units:
        if b == 0:
            att[p, b, hh] = jnp.where(diag_mask, jnp.exp(logw[p, b, hh]), 0.0).astype(bf16)
        else:
            att[p, b, hh] = jnp.exp(logw[p, b, hh] - (keep[p, 0, hh] + pen1)).astype(bf16)
    accs = []
    for p in range(N_PAIRS):
        vals = jnp.concatenate([h for b in range(SB_STATIC_BLOCKS)
                                for h in split_heads(v_refs[b][:, pairs[p]])], axis=0)
        lhs = jnp.concatenate([att[p, b, hh] for b in range(SB_STATIC_BLOCKS) for hh in range(2)], axis=1)
        accs.append(jnp.dot(lhs, vals, preferred_element_type=f32))

    def cond(st):
        kb, go = st[0], st[1]
        return jnp.logical_and(kb >= 0, go)

    def body(st):
        kb = st[0]
        acc_l = list(st[2:2 + N_PAIRS])
        car_l = list(st[2 + N_PAIRS:])
        for cp in fetch(kb):
            cp.start()
        for cp in fetch(kb):
            cp.wait()
        new_car = []
        for p, ps in enumerate(pairs):
            atts = []
            for hh, qh in enumerate(split_heads(q_ref[:, ps])):
                zz = _nt_dot(qh, kbuf[:, ps])
                spp = _softplus(zz)
                carry = car_l[2 * p + hh]
                atts.append(jnp.exp(zz - spp - suffix_sums(spp) - carry).astype(bf16))
                new_car.append(carry + jnp.sum(spp, axis=1, keepdims=True))
            vals = jnp.concatenate(split_heads(vbuf[:, ps]), axis=0)
            acc_l[p] = acc_l[p] + jnp.dot(jnp.concatenate(atts, axis=1), vals, preferred_element_type=f32)
        return (kb - 1, live(new_car), *acc_l, *new_car)

    st = lax.while_loop(cond, body, (qb - SB_STATIC_BLOCKS, live(carries), *accs, *carries))
    for p, ps in enumerate(pairs):
        o_ref[:, ps] = st[2 + p]


def _sb_attention(p_sb):
    s = p_sb.shape[0]
    blk = SB_BLOCK

    step_rows = SB_QBLOCKS * blk

    def current(col):
        return pl.BlockSpec((step_rows, SB_WIDTH), lambda i: (i, col))

    def previous(col):
        return pl.BlockSpec((blk, SB_WIDTH), lambda i: (jnp.maximum(i * SB_QBLOCKS - 1, 0), col))

    return pl.pallas_call(
        _sb_kernel,
        grid=(s // step_rows,),
        in_specs=[current(0), current(1), previous(1), current(2), previous(2),
                  pl.BlockSpec(memory_space=pl.ANY)],
        out_specs=pl.BlockSpec((step_rows, SB_WIDTH), lambda i: (i, 0)),
        out_shape=jax.ShapeDtypeStruct((s, SB_WIDTH), f32),
        scratch_shapes=[
            pltpu.VMEM((blk, SB_WIDTH), bf16),
            pltpu.VMEM((blk, SB_WIDTH), bf16),
            pltpu.SemaphoreType.DMA((2,)),
        ],
        compiler_params=pltpu.CompilerParams(
            dimension_semantics=("arbitrary",), vmem_limit_bytes=VMEM_LIMIT),
        name="sb_attention",
    )(p_sb, p_sb, p_sb, p_sb, p_sb, p_sb)


MOE_TILE = 256
PAIRS_PER_GROUP = 6
N_CLASSES = N_GROUPS * PAIRS_PER_GROUP
X1E_W = D_MODEL + LANES
MERGE_ROWS = 512
MERGE_PARTS = 2


def _route(aff, sel):
    neg = -jnp.inf
    best = None
    for gidx in range(N_GROUPS):
        rows = slice(gidx * EXPERTS_PER_GROUP, (gidx + 1) * EXPERTS_PER_GROUP)
        sg = sel[rows]
        idx = lax.broadcasted_iota(jnp.int32, sg.shape, 0) + gidx * EXPERTS_PER_GROUP
        m1 = jnp.max(sg, axis=0, keepdims=True)
        i1 = jnp.min(jnp.where(sg == m1, idx, N_EXPERTS), axis=0, keepdims=True)
        sg2 = jnp.where(idx == i1, neg, sg)
        m2 = jnp.max(sg2, axis=0, keepdims=True)
        i2 = jnp.min(jnp.where(sg2 == m2, idx, N_EXPERTS), axis=0, keepdims=True)
        score = m1 + m2
        if best is None:
            best = (score, i1, i2)
        else:
            better = score > best[0]
            best = (jnp.where(better, score, best[0]),
                    jnp.where(better, i1, best[1]),
                    jnp.where(better, i2, best[2]))
    _, i1, i2 = best
    expert = lax.broadcasted_iota(jnp.int32, aff.shape, 0)
    w1 = jnp.sum(jnp.where(expert == i1, aff, 0.0), axis=0, keepdims=True)
    w2 = jnp.sum(jnp.where(expert == i2, aff, 0.0), axis=0, keepdims=True)
    denom = w1 + w2
    w1, w2 = w1 / denom, w2 / denom
    first_low = i1 < i2
    e_lo = jnp.minimum(i1, i2)
    e_hi = jnp.maximum(i1, i2)
    a = jnp.bitwise_and(e_lo, EXPERTS_PER_GROUP - 1)
    b = jnp.bitwise_and(e_hi, EXPERTS_PER_GROUP - 1)
    pair = jnp.where(a == 0, 0, jnp.where(a == 1, 3, 5)) + (b - a - 1)
    order = jnp.where(pair == 3, 4, jnp.where(pair == 4, 3, pair))
    cls = lax.shift_right_logical(e_lo, 2) * PAIRS_PER_GROUP + order
    w_lo, w_hi = jnp.where(first_low, w1, w2), jnp.where(first_low, w2, w1)
    swapped = pair == PAIRS_PER_GROUP - 1
    return cls, jnp.where(swapped, w_hi, w_lo), jnp.where(swapped, w_lo, w_hi)


CLASS_ROWS = 32


def _merge_kernel(x_ref, oa_ref, ob_ref, g_ref, pa_ref, pb_ref, wo_ref, lg_ref, lb_ref, wr_ref, rb_ref,
                  x1e_ref, route_ref, cnt_ref, run_ref):
    step = pl.program_id(0)

    @pl.when(step == 0)
    def _init():
        run_ref[...] = jnp.zeros_like(run_ref)

    rows = x_ref.shape[0] // MERGE_PARTS
    parts = [slice(i * rows, (i + 1) * rows) for i in range(MERGE_PARTS)]
    a = [jnp.dot(oa_ref[p, :].astype(bf16), pa_ref[...], preferred_element_type=f32) for p in parts]
    b = [jnp.dot(ob_ref[p, :].astype(bf16), pb_ref[...], preferred_element_type=f32) for p in parts]
    merged = [(_sigmoid(g_ref[p, :D_MODEL].astype(f32)) * ai
               + _sigmoid(g_ref[p, D_MODEL:].astype(f32)) * bi).astype(bf16)
              for p, ai, bi in zip(parts, a, b)]
    mix = [jnp.dot(m, wo_ref[...], preferred_element_type=f32) for m in merged]
    x1 = [_layer_norm(DEEPNORM_ALPHA * x_ref[p, :] + mi, lg_ref[...], lb_ref[...]) for p, mi in zip(parts, mix)]
    x1b = [xi.astype(bf16) for xi in x1]
    for p, xi in zip(parts, x1):
        x1e_ref[p, :D_MODEL] = xi

    x_lo = [(xi - xbi.astype(f32)).astype(bf16) for xi, xbi in zip(x1, x1b)]
    t = [_nt_dot(wr_ref[...], xbi) for xbi in x1b]
    logits = [ti[:N_EXPERTS] + ti[N_EXPERTS:] + _nt_dot(wr_ref[:N_EXPERTS, :], xl) for ti, xl in zip(t, x_lo)]
    aff = [_sigmoid(lg) for lg in logits]
    routed = [_route(af, af + rb_ref[...]) for af in aff]

    rr = lax.broadcasted_iota(jnp.int32, (rows, rows), 0)
    cc = lax.broadcasted_iota(jnp.int32, (rows, rows), 1)
    triu = jnp.where(rr <= cc, 1.0, 0.0).astype(bf16)
    class_id = lax.broadcasted_iota(jnp.int32, (CLASS_ROWS, rows), 0)
    for p, (cls, w_lo, w_hi) in zip(parts, routed):
        onehot = class_id == cls
        prefix = jnp.dot(jnp.where(onehot, 1.0, 0.0).astype(bf16), triu, preferred_element_type=f32)
        run = run_ref[...]
        rank = (jnp.sum(jnp.where(onehot, prefix + run, 0.0), axis=0, keepdims=True) - 1.0).astype(jnp.int32)
        run_ref[...] = run + prefix[:, rows - 1:rows]
        pad = jnp.zeros((SUBLANES - 3, rows), jnp.int32)
        route_ref[0, :, p] = jnp.concatenate(
            [cls, lax.shift_right_logical(rank, 7), jnp.bitwise_and(rank, LANES - 1), pad], axis=0)
        w_rows = jnp.concatenate([w_lo, w_hi, jnp.zeros((LANES - 2, rows), f32)], axis=0)
        x1e_ref[p, D_MODEL:] = w_rows.T
    cnt_ref[...] = jnp.broadcast_to(run_ref[...], cnt_ref.shape).astype(jnp.int32)


def _merge(x, o_a, o_b, gates, p_a, p_b, w_out, ln_g, ln_b, wr_cat, r_bias):
    s = x.shape[0]
    tm = MERGE_ROWS
    row = lambda i: (i, 0)
    fixed = lambda i: (0, 0)
    return pl.pallas_call(
        _merge_kernel,
        grid=(s // tm,),
        in_specs=[
            pl.BlockSpec((tm, D_MODEL), row),
            pl.BlockSpec((tm, DN_HEADS * DN_DV), row),
            pl.BlockSpec((tm, SB_WIDTH), row),
            pl.BlockSpec((tm, GATE_SLAB), row),
            pl.BlockSpec((DN_HEADS * DN_DV, D_MODEL), fixed),
            pl.BlockSpec((SB_WIDTH, D_MODEL), fixed),
            pl.BlockSpec((D_MODEL, D_MODEL), fixed),
            pl.BlockSpec((1, D_MODEL), fixed),
            pl.BlockSpec((1, D_MODEL), fixed),
            pl.BlockSpec((2 * N_EXPERTS, D_MODEL), fixed),
            pl.BlockSpec((N_EXPERTS, 1), fixed),
        ],
        out_specs=[
            pl.BlockSpec((tm, X1E_W), row),
            pl.BlockSpec((1, SUBLANES, tm), lambda i: (i, 0, 0)),
            pl.BlockSpec((CLASS_ROWS, LANES), fixed),
        ],
        out_shape=[
            jax.ShapeDtypeStruct((s, X1E_W), f32),
            jax.ShapeDtypeStruct((s // tm, SUBLANES, tm), jnp.int32),
            jax.ShapeDtypeStruct((CLASS_ROWS, LANES), jnp.int32),
        ],
        scratch_shapes=[pltpu.VMEM((CLASS_ROWS, 1), f32)],
        compiler_params=pltpu.CompilerParams(
            dimension_semantics=("arbitrary",), vmem_limit_bytes=VMEM_LIMIT),
        name="merge_router",
    )(x, o_a, o_b, gates, p_a, p_b, w_out, ln_g, ln_b, wr_cat, r_bias)


SLOT_PAIRS = ((0, 1), (0, 2), (0, 3), (1, 3), (1, 2), (3, 2))


def _class_experts():
    slot_a = [g * EXPERTS_PER_GROUP + a for g in range(N_GROUPS) for a, _ in SLOT_PAIRS]
    slot_b = [g * EXPERTS_PER_GROUP + b for g in range(N_GROUPS) for _, b in SLOT_PAIRS]
    return jnp.array(slot_a, jnp.int32), jnp.array(slot_b, jnp.int32)


def _n_tiles(s):
    return -(-(s + N_CLASSES * (MOE_TILE - 1)) // MOE_TILE)


def _route_tables(route, cnt, s):
    counts = cnt[:N_CLASSES, 0]
    padded = (counts + (MOE_TILE - 1)) // MOE_TILE * MOE_TILE
    ends = jnp.cumsum(padded)
    offs = ends - padded
    cls, rank_hi, rank_lo = (route[:, r, :].reshape(s) for r in range(3))
    dest = offs[cls] + rank_hi * LANES + rank_lo
    n_active = (ends[-1] // MOE_TILE).astype(jnp.int32)[None]
    tile_row = jnp.minimum(jnp.arange(_n_tiles(s), dtype=jnp.int32) * MOE_TILE, ends[-1] - 1)
    tile_cls = jnp.minimum(jnp.sum(tile_row[:, None] >= ends[None, :], axis=1), N_CLASSES - 1)
    slot_a, slot_b = _class_experts()
    tile_a, tile_b = slot_a[tile_cls], slot_b[tile_cls]
    first = jnp.ones((1,), jnp.int32)
    new_a = jnp.concatenate([first, (tile_a[1:] != tile_a[:-1]).astype(jnp.int32)])
    new_b = jnp.concatenate([first, (tile_b[1:] != tile_b[:-1]).astype(jnp.int32)])
    tail = jnp.arange(s // MOE_TILE, _n_tiles(s), dtype=jnp.int32)
    pad_start = jnp.concatenate([ends - MOE_TILE, tail * MOE_TILE]).astype(jnp.int32)
    pad_valid = jnp.concatenate([padded > 0, tail >= n_active[0]]).astype(jnp.int32)
    return (dest.astype(jnp.int32), (n_active, tile_a, tile_b, new_a, new_b), pad_start, pad_valid)


def _permute_kernel(dest_ref, pstart_ref, pvalid_ref, x_ref, xs_hbm, zero_ref, sem):
    step = pl.program_id(0)
    tp = x_ref.shape[0]

    @pl.when(step == 0)
    def _fill():
        zero_ref[...] = jnp.zeros_like(zero_ref)

        def fill_copy(c):
            start = pl.multiple_of(pstart_ref[c], MOE_TILE)
            return pltpu.make_async_copy(zero_ref, xs_hbm.at[pl.ds(start, MOE_TILE)], sem.at[1])

        for c in range(pstart_ref.shape[0]):
            @pl.when(pvalid_ref[c] != 0)
            def _start():
                fill_copy(c).start()
        for c in range(pstart_ref.shape[0]):
            @pl.when(pvalid_ref[c] != 0)
            def _wait():
                fill_copy(c).wait()

    base = step * tp

    def issue(g, carry):
        r8 = pl.multiple_of(g * SUBLANES, SUBLANES)
        for j in range(SUBLANES):
            d = dest_ref[base + r8 + j]
            pltpu.make_async_copy(x_ref.at[pl.ds(r8 + j, 1)], xs_hbm.at[pl.ds(d, 1)], sem.at[0]).start()
        return carry

    lax.fori_loop(0, tp // SUBLANES, issue, 0)
    pltpu.make_async_copy(x_ref, xs_hbm.at[pl.ds(0, tp)], sem.at[0]).wait()


def _permute(x1e, dest, pad_start, pad_valid, tp=512):
    s = x1e.shape[0]
    return pl.pallas_call(
        _permute_kernel,
        grid_spec=pltpu.PrefetchScalarGridSpec(
            num_scalar_prefetch=3,
            grid=(s // tp,),
            in_specs=[pl.BlockSpec((tp, X1E_W), lambda i, d, ps, pv: (i, 0))],
            out_specs=pl.BlockSpec(memory_space=pl.ANY),
            scratch_shapes=[pltpu.VMEM((MOE_TILE, X1E_W), f32), pltpu.SemaphoreType.DMA((2,))],
        ),
        out_shape=jax.ShapeDtypeStruct((_n_tiles(s) * MOE_TILE, X1E_W), f32),
        compiler_params=pltpu.CompilerParams(
            dimension_semantics=("arbitrary",), vmem_limit_bytes=VMEM_LIMIT),
        name="moe_permute",
    )(dest, pad_start, pad_valid, x1e)


def _moe_kernel(nact_ref, ea_ref, eb_ref, newa_ref, newb_ref, xs_ref,
                wg0_ref, wu0_ref, wd0_ref, wg1_ref, wu1_ref, wd1_ref, ys_ref, wgu_ref, wd_ref):
    j = pl.program_id(0)

    @pl.when(j < nact_ref[0])
    def _tile():
        for slot, (new_ref, wg_ref, wu_ref, wdn_ref) in enumerate(((newa_ref, wg0_ref, wu0_ref, wd0_ref),
                                                                   (newb_ref, wg1_ref, wu1_ref, wd1_ref))):
            @pl.when(new_ref[j] != 0)
            def _recast():
                wgu_ref[2 * slot] = wg_ref[0, 0].astype(bf16)
                wgu_ref[2 * slot + 1] = wu_ref[0, 0].astype(bf16)
                wd_ref[slot] = wdn_ref[0, 0].astype(bf16)

        xb = xs_ref[:, :D_MODEL].astype(bf16)
        acc = None
        for slot in range(2):
            gate = jnp.dot(xb, wgu_ref[2 * slot], preferred_element_type=f32)
            up = jnp.dot(xb, wgu_ref[2 * slot + 1], preferred_element_type=f32)
            hid = (gate * _sigmoid(gate)) * up * xs_ref[:, D_MODEL + slot:D_MODEL + slot + 1]
            part = jnp.dot(hid.astype(bf16), wd_ref[slot], preferred_element_type=f32)
            acc = part if acc is None else acc + part
        ys_ref[...] = acc

    @pl.when(pl.program_id(0) >= nact_ref[0])
    def _unused_tile():
        ys_ref[...] = jnp.zeros_like(ys_ref)


def _moe(xs, tiles, w_gate, w_up, w_down, layer):
    n_tiles = xs.shape[0] // MOE_TILE
    tile = lambda j, na, ea, eb, ca, cb: (jnp.minimum(j, na[0] - 1), 0)
    low = lambda j, na, ea, eb, ca, cb: (layer, ea[j], 0, 0)
    high = lambda j, na, ea, eb, ca, cb: (layer, eb[j], 0, 0)
    up_shape = (1, 1, D_MODEL, D_FF_EXPERT)
    down_shape = (1, 1, D_FF_EXPERT, D_MODEL)
    return pl.pallas_call(
        _moe_kernel,
        grid_spec=pltpu.PrefetchScalarGridSpec(
            num_scalar_prefetch=5,
            grid=(n_tiles,),
            in_specs=[
                pl.BlockSpec((MOE_TILE, X1E_W), tile),
                pl.BlockSpec(up_shape, low), pl.BlockSpec(up_shape, low), pl.BlockSpec(down_shape, low),
                pl.BlockSpec(up_shape, high), pl.BlockSpec(up_shape, high), pl.BlockSpec(down_shape, high),
            ],
            out_specs=pl.BlockSpec((MOE_TILE, D_MODEL), lambda j, na, ea, eb, ca, cb: (j, 0)),
            scratch_shapes=[pltpu.VMEM((4, D_MODEL, D_FF_EXPERT), bf16), pltpu.VMEM((2, D_FF_EXPERT, D_MODEL), bf16)],
        ),
        out_shape=jax.ShapeDtypeStruct((n_tiles * MOE_TILE, D_MODEL), f32),
        compiler_params=pltpu.CompilerParams(
            dimension_semantics=("arbitrary",), vmem_limit_bytes=VMEM_LIMIT),
        name="moe_ffn",
    )(*tiles, xs, w_gate, w_up, w_down, w_gate, w_up, w_down)


def _unpermute_kernel(dest_ref, x1_ref, ys_hbm, lg_ref, lb_ref, out_ref, outb_ref, ybuf, sem):
    i = pl.program_id(0)
    n = pl.num_programs(0)
    tu = x1_ref.shape[0]

    def gather(tile, slot):
        base = tile * tu

        def issue(g, carry):
            r8 = pl.multiple_of(g * SUBLANES, SUBLANES)
            for j in range(SUBLANES):
                d = dest_ref[base + r8 + j]
                pltpu.make_async_copy(ys_hbm.at[pl.ds(d, 1)], ybuf.at[slot, pl.ds(r8 + j, 1)],
                                      sem.at[slot]).start()
            return carry

        lax.fori_loop(0, tu // SUBLANES, issue, 0)

    slot = lax.rem(i, 2)

    @pl.when(i == 0)
    def _first():
        gather(0, 0)

    @pl.when(i + 1 < n)
    def _next():
        gather(i + 1, 1 - slot)

    pltpu.make_async_copy(ys_hbm.at[pl.ds(0, tu)], ybuf.at[slot], sem.at[slot]).wait()
    y = _layer_norm(DEEPNORM_ALPHA * x1_ref[...] + ybuf[slot], lg_ref[...], lb_ref[...])
    out_ref[...] = y
    outb_ref[...] = y.astype(bf16)


def _unpermute(x1e, ys, dest, ln_g, ln_b, tu=256):
    s = x1e.shape[0]
    row = lambda i, d: (i, 0)
    fixed = lambda i, d: (0, 0)
    return pl.pallas_call(
        _unpermute_kernel,
        grid_spec=pltpu.PrefetchScalarGridSpec(
            num_scalar_prefetch=1,
            grid=(s // tu,),
            in_specs=[
                pl.BlockSpec((tu, D_MODEL), row),
                pl.BlockSpec(memory_space=pl.ANY),
                pl.BlockSpec((1, D_MODEL), fixed),
                pl.BlockSpec((1, D_MODEL), fixed),
            ],
            out_specs=[pl.BlockSpec((tu, D_MODEL), row), pl.BlockSpec((tu, D_MODEL), row)],
            scratch_shapes=[pltpu.VMEM((2, tu, D_MODEL), f32), pltpu.SemaphoreType.DMA((2,))],
        ),
        out_shape=[jax.ShapeDtypeStruct((s, D_MODEL), f32), jax.ShapeDtypeStruct((s, D_MODEL), bf16)],
        compiler_params=pltpu.CompilerParams(
            dimension_semantics=("arbitrary",), vmem_limit_bytes=VMEM_LIMIT),
        name="moe_unpermute_ln2",
    )(dest, x1e, ys, ln_g, ln_b)


def _pad_lanes(a, width=LANES):
    return jnp.pad(a, ((0, 0), (0, width - a.shape[1])))


def kernel(x, w_in, conv_w, dn_a_log, dn_dt_bias, dn_norm_w, p_a, p_b, w_out, ln1_g, ln1_b, w_router, router_bias, w_gate, w_up, w_down, ln2_g, ln2_b):
    bsz, s, _ = x.shape
    assert bsz == 1 and s % 512 == 0
    xf = x[0]
    xb = xf

    c_ba = 4 * DN_WIDTH
    c_sb = c_ba + 2 * DN_HEADS
    c_gate = c_sb + SB_SLAB
    wr_t = w_router.T
    wr_hi = wr_t.astype(bf16)
    wr_cat = jnp.concatenate([wr_hi, (wr_t - wr_hi.astype(f32)).astype(bf16)], axis=0)
    rb_col = router_bias[:, None]
    head_pad = ((0, 0), (DN_HEADS, LANES - 2 * DN_HEADS))
    w_in_t = jnp.swapaxes(w_in, 1, 2)

    for l in range(DEPTH):
        w = w_in_t[l]
        w_dn = jnp.concatenate([w[:c_ba], jnp.pad(w[c_ba:c_sb], ((0, LANES - 2 * DN_HEADS), (0, 0)))],
                               axis=0).astype(bf16)
        w_sb = jnp.concatenate([w[c_sb:c_sb + SB_WIDTH] * (SB_DH ** -0.5),
                                w[c_sb + SB_WIDTH:c_gate]], axis=0).astype(bf16)
        w_g = w[c_gate:].astype(bf16)

        p_dn, p_sb, gates = _proj(xb, w_dn, w_sb, w_g)
        o_a = _deltanet(p_dn,
                        jnp.pad(conv_w[l], ((0, 8 - CONV_K), (0, 0))),
                        jnp.pad(dn_a_log[l][None, :], head_pad),
                        jnp.pad(dn_dt_bias[l][None, :], head_pad),
                        dn_norm_w[l][None, :])
        o_b = _sb_attention(p_sb)
        x1e, route, cnt = _merge(xf, o_a, o_b, gates,
                                p_a[l].astype(bf16), p_b[l].astype(bf16), w_out[l].astype(bf16),
                                ln1_g[l][None, :], ln1_b[l][None, :], wr_cat, rb_col)
        dest, tiles, pad_start, pad_valid = _route_tables(route, cnt, s)
        xs = _permute(x1e, dest, pad_start, pad_valid)
        ys = _moe(xs, tiles, w_gate, w_up, w_down, l)
        xf, xb = _unpermute(x1e, ys, dest, ln2_g[l][None, :], ln2_b[l][None, :])
    return xf[None]
```

```python
import jax
import jax.numpy as jnp
from jax import lax
from jax.experimental import pallas as pl
from jax.experimental.pallas import tpu as pltpu

f32 = jnp.float32
bf16 = jnp.bfloat16
HIGHEST = lax.Precision.HIGHEST

D_MODEL = 1024
DEPTH = 2
CHUNK = 64
DN_HEADS = 4
DN_DK = 128
DN_DV = 128
CONV_K = 4
SB_HEADS = 8
SB_DH = 64
SB_BLOCK = 128
N_EXPERTS = 16
N_GROUPS = 4
EXPERTS_PER_GROUP = N_EXPERTS // N_GROUPS
D_FF_EXPERT = 512
LN_EPS = 1e-5
RMS_EPS = 1e-6
DEEPNORM_ALPHA = (2 * DEPTH) ** 0.25

DN_WIDTH = DN_HEADS * DN_DK
SB_WIDTH = SB_HEADS * SB_DH
CONV_CH = 3 * DN_WIDTH
LANES = 128
SUBLANES = 8
DN_SLAB = 4 * DN_WIDTH + LANES
BA_COL = 4 * DN_WIDTH
SB_SLAB = 3 * SB_WIDTH
GATE_SLAB = 2 * D_MODEL

SB_LOG_ZERO = -88.0
SB_STATIC_BLOCKS = 2
SB_MASK_PENALTY = -1e30

VMEM_LIMIT = 48 * 1024 * 1024


def _sigmoid(x):
    return 1.0 / (1.0 + jnp.exp(-x))


def _softplus(x):
    return jnp.maximum(x, 0.0) + jnp.log(1.0 + jnp.exp(-jnp.abs(x)))


def _nt_dot(a, b):
    return lax.dot_general(a, b, (((1,), (1,)), ((), ())), preferred_element_type=f32)


def _tn_dot(a, b):
    return lax.dot_general(a, b, (((0,), (0,)), ((), ())), preferred_element_type=f32)


def _layer_norm(y, g, b):
    mu = jnp.mean(y, axis=-1, keepdims=True)
    d = y - mu
    var = jnp.mean(d * d, axis=-1, keepdims=True)
    return d * lax.rsqrt(var + LN_EPS) * g + b


def _proj_kernel(x_ref, wdn_ref, wsb_ref, wg_ref, odn_ref, osb_ref, og_ref):
    x = x_ref[...].astype(bf16)
    odn_ref[...] = _nt_dot(x, wdn_ref[...])
    osb_ref[...] = _nt_dot(x, wsb_ref[...]).astype(bf16)
    og_ref[...] = _nt_dot(x, wg_ref[...]).astype(bf16)


def _proj(x, w_dn, w_sb, w_g, tm=512):
    s = x.shape[0]
    row = lambda i: (i, 0)
    fixed = lambda i: (0, 0)
    return pl.pallas_call(
        _proj_kernel,
        grid=(s // tm,),
        in_specs=[
            pl.BlockSpec((tm, D_MODEL), row),
            pl.BlockSpec((DN_SLAB, D_MODEL), fixed),
            pl.BlockSpec((SB_SLAB, D_MODEL), fixed),
            pl.BlockSpec((GATE_SLAB, D_MODEL), fixed),
        ],
        out_specs=[
            pl.BlockSpec((tm, DN_SLAB), row),
            pl.BlockSpec((tm, SB_SLAB), row),
            pl.BlockSpec((tm, GATE_SLAB), row),
        ],
        out_shape=[
            jax.ShapeDtypeStruct((s, DN_SLAB), f32),
            jax.ShapeDtypeStruct((s, SB_SLAB), bf16),
            jax.ShapeDtypeStruct((s, GATE_SLAB), bf16),
        ],
        compiler_params=pltpu.CompilerParams(
            dimension_semantics=("arbitrary",), vmem_limit_bytes=VMEM_LIMIT),
        name="proj",
    )(x, w_dn, w_sb, w_g)


DN_ROWS = 1024
PAIR = 2 * CHUNK
TAIL = 8
DN_GROUP = 2
DN_SEQ_START = 22
DN_SEQ_EVERY = 5


def _split2(x):
    hi = x.astype(bf16)
    return hi, (x - hi.astype(f32)).astype(bf16)


def _split3(x):
    hi = x.astype(bf16)
    r = x - hi.astype(f32)
    mid = r.astype(bf16)
    return hi, mid, (r - mid.astype(f32)).astype(bf16)


def _dn_kernel(p_ref, cw_ref, alog_ref, dtb_ref, nw_ref, o_ref, state_ref, tail_ref, xe_ref):
    step = pl.program_id(0)
    rows = p_ref.shape[0]
    n_pb = rows // PAIR

    @pl.when(step == 0)
    def _init():
        state_ref[...] = jnp.zeros_like(state_ref)
        tail_ref[...] = jnp.zeros_like(tail_ref)

    xe_ref[0:TAIL, :] = tail_ref[...]
    xe_ref[TAIL:TAIL + rows, :] = p_ref[:, 0:CONV_CH]
    tail_ref[...] = p_ref[rows - TAIL:rows, 0:CONV_CH]

    row128 = lax.broadcasted_iota(jnp.int32, (PAIR, LANES), 0)
    lane128 = lax.broadcasted_iota(jnp.int32, (PAIR, LANES), 1)
    same_chunk = (row128 >= CHUNK) == (lane128 >= CHUNK)
    tril_bd = jnp.where(jnp.logical_and(row128 >= lane128, same_chunk), 1.0, 0.0).astype(bf16)
    triu_bd = jnp.where(jnp.logical_and(row128 <= lane128, same_chunk), 1.0, 0.0).astype(bf16)
    first_rows = row128 < CHUNK
    row64 = lax.broadcasted_iota(jnp.int32, (CHUNK, LANES), 0)
    lane64 = lax.broadcasted_iota(jnp.int32, (CHUNK, LANES), 1)
    left = lane64 < CHUNK
    col_in_chunk = jnp.bitwise_and(lane64, CHUNK - 1)
    tri_p = row64 >= col_in_chunk
    strict_p = row64 > col_in_chunk
    eye_p = jnp.where(row64 == col_in_chunk, 1.0, 0.0).astype(f32)

    def block_diag(z):
        zero = jnp.zeros_like(z)
        return jnp.concatenate([jnp.where(left, z, zero), jnp.where(left, zero, z)], axis=0)

    def pair_matmul(y_hi, y_lo, zbd_hi, zbd_lo):
        return (jnp.dot(jnp.concatenate([y_hi, y_lo], axis=1), jnp.concatenate([zbd_hi, zbd_hi], axis=0),
                        preferred_element_type=f32)
                + jnp.dot(y_hi, zbd_lo, preferred_element_type=f32))

    nw = nw_ref[...]
    heads = range(DN_HEADS)
    base = TAIL - (CONV_K - 1)

    def pair_block(pb):
        r0 = pb * PAIR
        ba = p_ref[r0:r0 + PAIR, BA_COL:BA_COL + LANES]
        beta_all = _sigmoid(ba)
        parts = _split3(-jnp.exp(alog_ref[...]) * _softplus(ba + dtb_ref[...]))
        gcol = sum(jnp.dot(tril_bd, pt, preferred_element_type=f32) for pt in parts)
        grow = sum(_tn_dot(pt, triu_bd) for pt in parts)
        yield

        qkv = []
        for grp in range(3):
            outs = []
            for h in heads:
                col = grp * DN_WIDTH + h * DN_DK
                acc = xe_ref[base + r0:base + r0 + PAIR, col:col + DN_DK] * cw_ref[0:1, col:col + DN_DK]
                for j in range(1, CONV_K):
                    acc = acc + (xe_ref[base + r0 + j:base + r0 + j + PAIR, col:col + DN_DK]
                                 * cw_ref[j:j + 1, col:col + DN_DK])
                y = acc * _sigmoid(acc)
                if grp < 2:
                    y = y * lax.rsqrt(jnp.sum(y * y, axis=-1, keepdims=True) + RMS_EPS)
                outs.append(y * (DN_DK ** -0.5) if grp == 0 else y)
                yield
            qkv.append(outs)
        qs, ks, vs = qkv

        kbetas, qgs, kdecs, rstacks, decays, egl = [], [], [], [], [], []
        for h in heads:
            gc = jnp.broadcast_to(gcol[:, DN_HEADS + h:DN_HEADS + h + 1], (PAIR, LANES))
            beta = jnp.broadcast_to(beta_all[:, h:h + 1], (PAIR, LANES))
            eg = jnp.exp(gc)
            glast = jnp.where(first_rows, gc[CHUNK - 1:CHUNK, :], gc[PAIR - 1:PAIR, :])
            kbeta = ks[h] * beta
            kbetas.append(kbeta)
            qgs.append(qs[h] * eg)
            kdecs.append((ks[h] * jnp.exp(glast - gc)).astype(bf16))
            rstacks.append(jnp.concatenate([kbeta * eg, vs[h] * beta], axis=1).astype(bf16))
            gdiff = jnp.where(left, gc[:CHUNK], gc[CHUNK:]) - grow[DN_HEADS + h:DN_HEADS + h + 1, :]
            decays.append(jnp.where(tri_p, jnp.exp(jnp.where(tri_p, gdiff, 0.0)), 0.0))
            egl.append((jnp.exp(gc[CHUNK - 1:CHUNK, :]), jnp.exp(gc[PAIR - 1:PAIR, :])))
            yield

        lps, qkms = [], []
        for h in heads:
            kk = _nt_dot(jnp.concatenate([kbetas[h], qs[h]], axis=0).astype(bf16), ks[h].astype(bf16))
            lps.append(jnp.where(strict_p, jnp.where(left, kk[0:CHUNK], kk[CHUNK:PAIR]) * decays[h], 0.0))
            qkms.append(jnp.where(tri_p, jnp.where(left, kk[PAIR:PAIR + CHUNK], kk[PAIR + CHUNK:]) * decays[h], 0.0)
                        .astype(bf16))
            yield

        pw = [_split2(lp) for lp in lps]
        pw_bd = [(block_diag(hi), block_diag(lo)) for hi, lo in pw]
        ts = [eye_p - lp for lp in lps]
        for _ in range(5):
            pw = [_split2(pair_matmul(*pw[h], *pw_bd[h])) for h in heads]
            pw_bd = [(block_diag(hi), block_diag(lo)) for hi, lo in pw]
            yield
            ts = [ts[h] + pair_matmul(*_split2(ts[h]), *pw_bd[h]) for h in heads]
            yield

        zero_p = jnp.zeros((CHUNK, LANES), bf16)
        mq, bo = [], []
        for h in heads:
            t16 = ts[h].astype(bf16)
            halves = (jnp.where(left, t16, zero_p), jnp.where(left, zero_p, t16))
            qk_halves = (jnp.where(left, qkms[h], zero_p), jnp.where(left, zero_p, qkms[h]))
            wus = [jnp.dot(th, rstacks[h], preferred_element_type=f32).astype(bf16) for th in halves]
            wu_stack = jnp.concatenate(wus, axis=0)
            mq_h, bo_h = [], []
            for c in range(2):
                cs = slice(c * CHUNK, (c + 1) * CHUNK)
                kw = _tn_dot(kdecs[h][cs], wus[c])
                qw = jnp.dot(qk_halves[c], wu_stack, preferred_element_type=f32)
                mq_h.append(jnp.concatenate([kw[:, :DN_DK], qgs[h][cs] - qw[:, :DN_DK]], axis=0).astype(bf16))
                bo_h.append((kw[:, DN_DK:], qw[:, DN_DK:]))
            mq.append(mq_h)
            bo.append(bo_h)
            yield

        prepared[pb] = (mq, bo, egl)

    states = [state_ref[h] for h in heads]

    def sequential(pbs):
        for pb in pbs:
            mq, bo, egl = prepared[pb]
            for c in range(2):
                for h in heads:
                    ms = jnp.dot(mq[h][c], states[h].astype(bf16), preferred_element_type=f32)
                    b_c, o_c = bo[h][c]
                    o = ms[DN_DK:] + o_c
                    states[h] = states[h] * egl[h][c] - ms[:DN_DK] + b_c
                    o = o * lax.rsqrt(jnp.mean(o * o, axis=-1, keepdims=True) + RMS_EPS) * nw
                    rc = pb * PAIR + c * CHUNK
                    z = p_ref[rc:rc + CHUNK, 3 * DN_WIDTH + h * DN_DV:3 * DN_WIDTH + (h + 1) * DN_DV]
                    o_ref[rc:rc + CHUNK, h * DN_DV:(h + 1) * DN_DV] = o * (z * _sigmoid(z))
                yield

    prepared = [None] * n_pb
    pending = iter(())
    for g0 in range(0, n_pb, DN_GROUP):
        group = range(g0, min(g0 + DN_GROUP, n_pb))
        pipes = [pair_block(pb) for pb in group]
        slot = 0
        while pipes:
            pipes = [pipe for pipe in pipes if next(pipe, "done") != "done"]
            slot += 1
            if slot >= DN_SEQ_START and (slot - DN_SEQ_START) % DN_SEQ_EVERY == 0:
                next(pending, None)
        for _ in pending:
            pass
        pending = sequential(group)
    for _ in pending:
        pass
    for h in heads:
        state_ref[h] = states[h]


def _deltanet(p_dn, conv_w8, alog_row, dtb_row, nw_row):
    s = p_dn.shape[0]
    fixed = lambda i: (0, 0)
    return pl.pallas_call(
        _dn_kernel,
        grid=(s // DN_ROWS,),
        in_specs=[
            pl.BlockSpec((DN_ROWS, DN_SLAB), lambda i: (i, 0)),
            pl.BlockSpec((8, CONV_CH), fixed),
            pl.BlockSpec((1, LANES), fixed),
            pl.BlockSpec((1, LANES), fixed),
            pl.BlockSpec((1, DN_DV), fixed),
        ],
        out_specs=pl.BlockSpec((DN_ROWS, DN_HEADS * DN_DV), lambda i: (i, 0)),
        out_shape=jax.ShapeDtypeStruct((s, DN_HEADS * DN_DV), f32),
        scratch_shapes=[
            pltpu.VMEM((DN_HEADS, DN_DK, DN_DV), f32),
            pltpu.VMEM((TAIL, CONV_CH), f32),
            pltpu.VMEM((TAIL + DN_ROWS, CONV_CH), f32),
        ],
        compiler_params=pltpu.CompilerParams(
            dimension_semantics=("arbitrary",), vmem_limit_bytes=VMEM_LIMIT),
        name="deltanet",
    )(p_dn, conv_w8, alog_row, dtb_row, nw_row)


N_PAIRS = SB_HEADS // 2
SB_QBLOCKS = 2


def _sb_kernel(q_ref, kc_ref, kp_ref, kpp_ref, vc_ref, vp_ref, vpp_ref, kv_hbm, o_ref, kbuf, vbuf, sem):
    cur = [pl.ds(sub * SB_BLOCK, SB_BLOCK) for sub in range(SB_QBLOCKS)]
    step = pl.program_id(0)
    _sb_query_block(step * SB_QBLOCKS, q_ref.at[cur[0]], (kc_ref.at[cur[0]], kp_ref, kpp_ref),
                    (vc_ref.at[cur[0]], vp_ref, vpp_ref), kv_hbm, o_ref.at[cur[0]], kbuf, vbuf, sem)
    _sb_query_block(step * SB_QBLOCKS + 1, q_ref.at[cur[1]], (kc_ref.at[cur[1]], kc_ref.at[cur[0]], kp_ref),
                    (vc_ref.at[cur[1]], vc_ref.at[cur[0]], vp_ref), kv_hbm, o_ref.at[cur[1]], kbuf, vbuf, sem)


def _sb_query_block(qb, q_ref, k_blocks, v_blocks, kv_hbm, o_ref, kbuf, vbuf, sem):
    blk = SB_BLOCK
    row = lax.broadcasted_iota(jnp.int32, (blk, blk), 0)
    lane = lax.broadcasted_iota(jnp.int32, (blk, blk), 1)
    diag_mask = row > lane
    even = lane < SB_DH
    suffix = jnp.where(row > lane, 1.0, 0.0).astype(bf16)
    suffix2 = jnp.concatenate([suffix, suffix], axis=0)

    def split_heads(x):
        zero = jnp.zeros_like(x)
        return jnp.where(even, x, zero), jnp.where(even, zero, x)

    def suffix_sums(spm):
        hi = spm.astype(bf16)
        lo = (spm - hi.astype(f32)).astype(bf16)
        return jnp.dot(jnp.concatenate([hi, lo], axis=1), suffix2, preferred_element_type=f32)

    pairs = [slice(p * LANES, (p + 1) * LANES) for p in range(N_PAIRS)]
    k_refs, v_refs = k_blocks[:SB_STATIC_BLOCKS], v_blocks[:SB_STATIC_BLOCKS]

    def fetch(kb):
        start = pl.multiple_of(kb * blk, blk)
        return (pltpu.make_async_copy(kv_hbm.at[pl.ds(start, blk), pl.ds(SB_WIDTH, SB_WIDTH)], kbuf, sem.at[0]),
                pltpu.make_async_copy(kv_hbm.at[pl.ds(start, blk), pl.ds(2 * SB_WIDTH, SB_WIDTH)], vbuf,
                                      sem.at[1]))

    def live(cr):
        m = cr[0]
        for c in cr[1:]:
            m = jnp.minimum(m, c)
        return jnp.min(m) <= -SB_LOG_ZERO

    pen1 = jnp.where(qb >= 1, 0.0, -SB_MASK_PENALTY).astype(f32)
    q_heads = [split_heads(q_ref[:, ps]) for ps in pairs]
    units = [(p, b, hh) for b in range(SB_STATIC_BLOCKS) for p in range(N_PAIRS) for hh in range(2)]
    z = {u: _nt_dot(q_heads[u[0]][u[2]], k_refs[u[1]][:, pairs[u[0]]]) for u in units}
    sp = {u: _softplus(z[u]) for u in units}
    spm = {u: (jnp.where(diag_mask, sp[u], 0.0) if u[1] == 0 else sp[u]) for u in units}
    logw = {u: z[u] - sp[u] - suffix_sums(spm[u]) for u in units}
    keep = {u: jnp.sum(spm[u], axis=1, keepdims=True) for u in units}
    carries = [keep[p, 0, hh] + keep[p, 1, hh] for p in range(N_PAIRS) for hh in range(2)]
    att = {}
    for p, b, hh in units:
        if b == 0:
            att[p, b, hh] = jnp.where(diag_mask, jnp.exp(logw[p, b, hh]), 0.0).astype(bf16)
        else:
            att[p, b, hh] = jnp.exp(logw[p, b, hh] - (keep[p, 0, hh] + pen1)).astype(bf16)
    accs = []
    for p in range(N_PAIRS):
        vals = jnp.concatenate([h for b in range(SB_STATIC_BLOCKS)
                                for h in split_heads(v_refs[b][:, pairs[p]])], axis=0)
        lhs = jnp.concatenate([att[p, b, hh] for b in range(SB_STATIC_BLOCKS) for hh in range(2)], axis=1)
        accs.append(jnp.dot(lhs, vals, preferred_element_type=f32))

    def cond(st):
        kb, go = st[0], st[1]
        return jnp.logical_and(kb >= 0, go)

    def body(st):
        kb = st[0]
        acc_l = list(st[2:2 + N_PAIRS])
        car_l = list(st[2 + N_PAIRS:])
        resident = kb == qb - SB_STATIC_BLOCKS

        @pl.when(resident)
        def _copy():
            kbuf[...] = k_blocks[SB_STATIC_BLOCKS][...]
            vbuf[...] = v_blocks[SB_STATIC_BLOCKS][...]

        @pl.when(jnp.logical_not(resident))
        def _fetch():
            for cp in fetch(kb):
                cp.start()
            for cp in fetch(kb):
                cp.wait()

        new_car = []
        for p, ps in enumerate(pairs):
            atts = []
            for hh, qh in enumerate(split_heads(q_ref[:, ps])):
                zz = _nt_dot(qh, kbuf[:, ps])
                spp = _softplus(zz)
                carry = car_l[2 * p + hh]
                atts.append(jnp.exp(zz - spp - suffix_sums(spp) - carry).astype(bf16))
                new_car.append(carry + jnp.sum(spp, axis=1, keepdims=True))
            vals = jnp.concatenate(split_heads(vbuf[:, ps]), axis=0)
            acc_l[p] = acc_l[p] + jnp.dot(jnp.concatenate(atts, axis=1), vals, preferred_element_type=f32)
        return (kb - 1, live(new_car), *acc_l, *new_car)

    st = lax.while_loop(cond, body, (qb - SB_STATIC_BLOCKS, live(carries), *accs, *carries))
    for p, ps in enumerate(pairs):
        o_ref[:, ps] = st[2 + p]


def _sb_attention(p_sb):
    s = p_sb.shape[0]
    blk = SB_BLOCK

    step_rows = SB_QBLOCKS * blk

    def current(col):
        return pl.BlockSpec((step_rows, SB_WIDTH), lambda i: (i, col))

    def previous(col, back):
        return pl.BlockSpec((blk, SB_WIDTH), lambda i: (jnp.maximum(i * SB_QBLOCKS - back, 0), col))

    return pl.pallas_call(
        _sb_kernel,
        grid=(s // step_rows,),
        in_specs=[current(0), current(1), previous(1, 1), previous(1, 2), current(2), previous(2, 1), previous(2, 2),
                  pl.BlockSpec(memory_space=pl.ANY)],
        out_specs=pl.BlockSpec((step_rows, SB_WIDTH), lambda i: (i, 0)),
        out_shape=jax.ShapeDtypeStruct((s, SB_WIDTH), f32),
        scratch_shapes=[
            pltpu.VMEM((blk, SB_WIDTH), bf16),
            pltpu.VMEM((blk, SB_WIDTH), bf16),
            pltpu.SemaphoreType.DMA((2,)),
        ],
        compiler_params=pltpu.CompilerParams(
            dimension_semantics=("arbitrary",), vmem_limit_bytes=VMEM_LIMIT),
        name="sb_attention",
    )(p_sb, p_sb, p_sb, p_sb, p_sb, p_sb, p_sb, p_sb)


MOE_TILE = 256
PAIRS_PER_GROUP = 6
N_CLASSES = N_GROUPS * PAIRS_PER_GROUP
X1E_W = D_MODEL + LANES
MERGE_ROWS = 512
MERGE_PARTS = 2


def _route(aff, sel):
    neg = -jnp.inf
    best = None
    for gidx in range(N_GROUPS):
        rows = slice(gidx * EXPERTS_PER_GROUP, (gidx + 1) * EXPERTS_PER_GROUP)
        sg = sel[rows]
        idx = lax.broadcasted_iota(jnp.int32, sg.shape, 0) + gidx * EXPERTS_PER_GROUP
        m1 = jnp.max(sg, axis=0, keepdims=True)
        i1 = jnp.min(jnp.where(sg == m1, idx, N_EXPERTS), axis=0, keepdims=True)
        sg2 = jnp.where(idx == i1, neg, sg)
        m2 = jnp.max(sg2, axis=0, keepdims=True)
        i2 = jnp.min(jnp.where(sg2 == m2, idx, N_EXPERTS), axis=0, keepdims=True)
        score = m1 + m2
        if best is None:
            best = (score, i1, i2)
        else:
            better = score > best[0]
            best = (jnp.where(better, score, best[0]),
                    jnp.where(better, i1, best[1]),
                    jnp.where(better, i2, best[2]))
    _, i1, i2 = best
    expert = lax.broadcasted_iota(jnp.int32, aff.shape, 0)
    w1 = jnp.sum(jnp.where(expert == i1, aff, 0.0), axis=0, keepdims=True)
    w2 = jnp.sum(jnp.where(expert == i2, aff, 0.0), axis=0, keepdims=True)
    denom = w1 + w2
    w1, w2 = w1 / denom, w2 / denom
    first_low = i1 < i2
    e_lo = jnp.minimum(i1, i2)
    e_hi = jnp.maximum(i1, i2)
    a = jnp.bitwise_and(e_lo, EXPERTS_PER_GROUP - 1)
    b = jnp.bitwise_and(e_hi, EXPERTS_PER_GROUP - 1)
    pair = jnp.where(a == 0, 0, jnp.where(a == 1, 3, 5)) + (b - a - 1)
    order = jnp.where(pair == 3, 4, jnp.where(pair == 4, 3, pair))
    cls = lax.shift_right_logical(e_lo, 2) * PAIRS_PER_GROUP + order
    w_lo, w_hi = jnp.where(first_low, w1, w2), jnp.where(first_low, w2, w1)
    swapped = pair == PAIRS_PER_GROUP - 1
    return cls, jnp.where(swapped, w_hi, w_lo), jnp.where(swapped, w_lo, w_hi)


CLASS_ROWS = 32


def _merge_kernel(x_ref, oa_ref, ob_ref, g_ref, pa_ref, pb_ref, wo_ref, lg_ref, lb_ref, wr_ref, rb_ref,
                  x1e_ref, route_ref, cnt_ref, run_ref):
    step = pl.program_id(0)

    @pl.when(step == 0)
    def _init():
        run_ref[...] = jnp.zeros_like(run_ref)

    rows = x_ref.shape[0] // MERGE_PARTS
    parts = [slice(i * rows, (i + 1) * rows) for i in range(MERGE_PARTS)]
    a = [jnp.dot(oa_ref[p, :].astype(bf16), pa_ref[...], preferred_element_type=f32) for p in parts]
    b = [jnp.dot(ob_ref[p, :].astype(bf16), pb_ref[...], preferred_element_type=f32) for p in parts]
    merged = [(_sigmoid(g_ref[p, :D_MODEL].astype(f32)) * ai
               + _sigmoid(g_ref[p, D_MODEL:].astype(f32)) * bi).astype(bf16)
              for p, ai, bi in zip(parts, a, b)]
    mix = [jnp.dot(m, wo_ref[...], preferred_element_type=f32) for m in merged]
    x1 = [_layer_norm(DEEPNORM_ALPHA * x_ref[p, :] + mi, lg_ref[...], lb_ref[...]) for p, mi in zip(parts, mix)]
    x1b = [xi.astype(bf16) for xi in x1]
    for p, xi in zip(parts, x1):
        x1e_ref[p, :D_MODEL] = xi

    x_lo = [(xi - xbi.astype(f32)).astype(bf16) for xi, xbi in zip(x1, x1b)]
    t = [_nt_dot(wr_ref[...], xbi) for xbi in x1b]
    logits = [ti[:N_EXPERTS] + ti[N_EXPERTS:] + _nt_dot(wr_ref[:N_EXPERTS, :], xl) for ti, xl in zip(t, x_lo)]
    aff = [_sigmoid(lg) for lg in logits]
    routed = [_route(af, af + rb_ref[...]) for af in aff]

    rr = lax.broadcasted_iota(jnp.int32, (rows, rows), 0)
    cc = lax.broadcasted_iota(jnp.int32, (rows, rows), 1)
    triu = jnp.where(rr <= cc, 1.0, 0.0).astype(bf16)
    class_id = lax.broadcasted_iota(jnp.int32, (CLASS_ROWS, rows), 0)
    for p, (cls, w_lo, w_hi) in zip(parts, routed):
        onehot = class_id == cls
        prefix = jnp.dot(jnp.where(onehot, 1.0, 0.0).astype(bf16), triu, preferred_element_type=f32)
        run = run_ref[...]
        rank = (jnp.sum(jnp.where(onehot, prefix + run, 0.0), axis=0, keepdims=True) - 1.0).astype(jnp.int32)
        run_ref[...] = run + prefix[:, rows - 1:rows]
        pad = jnp.zeros((SUBLANES - 3, rows), jnp.int32)
        route_ref[0, :, p] = jnp.concatenate(
            [cls, lax.shift_right_logical(rank, 7), jnp.bitwise_and(rank, LANES - 1), pad], axis=0)
        w_rows = jnp.concatenate([w_lo, w_hi, jnp.zeros((LANES - 2, rows), f32)], axis=0)
        x1e_ref[p, D_MODEL:] = w_rows.T
    cnt_ref[...] = jnp.broadcast_to(run_ref[...], cnt_ref.shape).astype(jnp.int32)


def _merge(x, o_a, o_b, gates, p_a, p_b, w_out, ln_g, ln_b, wr_cat, r_bias):
    s = x.shape[0]
    tm = MERGE_ROWS
    row = lambda i: (i, 0)
    fixed = lambda i: (0, 0)
    return pl.pallas_call(
        _merge_kernel,
        grid=(s // tm,),
        in_specs=[
            pl.BlockSpec((tm, D_MODEL), row),
            pl.BlockSpec((tm, DN_HEADS * DN_DV), row),
            pl.BlockSpec((tm, SB_WIDTH), row),
            pl.BlockSpec((tm, GATE_SLAB), row),
            pl.BlockSpec((DN_HEADS * DN_DV, D_MODEL), fixed),
            pl.BlockSpec((SB_WIDTH, D_MODEL), fixed),
            pl.BlockSpec((D_MODEL, D_MODEL), fixed),
            pl.BlockSpec((1, D_MODEL), fixed),
            pl.BlockSpec((1, D_MODEL), fixed),
            pl.BlockSpec((2 * N_EXPERTS, D_MODEL), fixed),
            pl.BlockSpec((N_EXPERTS, 1), fixed),
        ],
        out_specs=[
            pl.BlockSpec((tm, X1E_W), row),
            pl.BlockSpec((1, SUBLANES, tm), lambda i: (i, 0, 0)),
            pl.BlockSpec((CLASS_ROWS, LANES), fixed),
        ],
        out_shape=[
            jax.ShapeDtypeStruct((s, X1E_W), f32),
            jax.ShapeDtypeStruct((s // tm, SUBLANES, tm), jnp.int32),
            jax.ShapeDtypeStruct((CLASS_ROWS, LANES), jnp.int32),
        ],
        scratch_shapes=[pltpu.VMEM((CLASS_ROWS, 1), f32)],
        compiler_params=pltpu.CompilerParams(
            dimension_semantics=("arbitrary",), vmem_limit_bytes=VMEM_LIMIT),
        name="merge_router",
    )(x, o_a, o_b, gates, p_a, p_b, w_out, ln_g, ln_b, wr_cat, r_bias)


SLOT_PAIRS = ((0, 1), (0, 2), (0, 3), (1, 3), (1, 2), (3, 2))


def _class_experts():
    slot_a = [g * EXPERTS_PER_GROUP + a for g in range(N_GROUPS) for a, _ in SLOT_PAIRS]
    slot_b = [g * EXPERTS_PER_GROUP + b for g in range(N_GROUPS) for _, b in SLOT_PAIRS]
    return jnp.array(slot_a, jnp.int32), jnp.array(slot_b, jnp.int32)


def _n_tiles(s):
    return -(-(s + N_CLASSES * (MOE_TILE - 1)) // MOE_TILE)


def _route_tables(route, cnt, s):
    counts = cnt[:N_CLASSES, 0]
    padded = (counts + (MOE_TILE - 1)) // MOE_TILE * MOE_TILE
    ends = jnp.cumsum(padded)
    offs = ends - padded
    cls, rank_hi, rank_lo = (route[:, r, :].reshape(s) for r in range(3))
    dest = offs[cls] + rank_hi * LANES + rank_lo
    n_active = (ends[-1] // MOE_TILE).astype(jnp.int32)[None]
    tile_row = jnp.minimum(jnp.arange(_n_tiles(s), dtype=jnp.int32) * MOE_TILE, ends[-1] - 1)
    tile_cls = jnp.minimum(jnp.sum(tile_row[:, None] >= ends[None, :], axis=1), N_CLASSES - 1)
    slot_a, slot_b = _class_experts()
    tile_a, tile_b = slot_a[tile_cls], slot_b[tile_cls]
    first = jnp.ones((1,), jnp.int32)
    new_a = jnp.concatenate([first, (tile_a[1:] != tile_a[:-1]).astype(jnp.int32)])
    new_b = jnp.concatenate([first, (tile_b[1:] != tile_b[:-1]).astype(jnp.int32)])
    tail = jnp.arange(s // MOE_TILE, _n_tiles(s), dtype=jnp.int32)
    pad_start = jnp.concatenate([ends - MOE_TILE, tail * MOE_TILE]).astype(jnp.int32)
    pad_valid = jnp.concatenate([padded > 0, tail >= n_active[0]]).astype(jnp.int32)
    return (dest.astype(jnp.int32), (n_active, tile_a, tile_b, new_a, new_b), pad_start, pad_valid)


def _permute_kernel(dest_ref, pstart_ref, pvalid_ref, x_ref, xs_hbm, zero_ref, sem):
    step = pl.program_id(0)
    tp = x_ref.shape[0]

    @pl.when(step == 0)
    def _fill():
        zero_ref[...] = jnp.zeros_like(zero_ref)

        def fill_copy(c):
            start = pl.multiple_of(pstart_ref[c], MOE_TILE)
            return pltpu.make_async_copy(zero_ref, xs_hbm.at[pl.ds(start, MOE_TILE)], sem.at[1])

        for c in range(pstart_ref.shape[0]):
            @pl.when(pvalid_ref[c] != 0)
            def _start():
                fill_copy(c).start()
        for c in range(pstart_ref.shape[0]):
            @pl.when(pvalid_ref[c] != 0)
            def _wait():
                fill_copy(c).wait()

    base = step * tp

    def issue(g, carry):
        r8 = pl.multiple_of(g * SUBLANES, SUBLANES)
        for j in range(SUBLANES):
            d = dest_ref[base + r8 + j]
            pltpu.make_async_copy(x_ref.at[pl.ds(r8 + j, 1)], xs_hbm.at[pl.ds(d, 1)], sem.at[0]).start()
        return carry

    lax.fori_loop(0, tp // SUBLANES, issue, 0)
    pltpu.make_async_copy(x_ref, xs_hbm.at[pl.ds(0, tp)], sem.at[0]).wait()


def _permute(x1e, dest, pad_start, pad_valid, tp=512):
    s = x1e.shape[0]
    return pl.pallas_call(
        _permute_kernel,
        grid_spec=pltpu.PrefetchScalarGridSpec(
            num_scalar_prefetch=3,
            grid=(s // tp,),
            in_specs=[pl.BlockSpec((tp, X1E_W), lambda i, d, ps, pv: (i, 0))],
            out_specs=pl.BlockSpec(memory_space=pl.ANY),
            scratch_shapes=[pltpu.VMEM((MOE_TILE, X1E_W), f32), pltpu.SemaphoreType.DMA((2,))],
        ),
        out_shape=jax.ShapeDtypeStruct((_n_tiles(s) * MOE_TILE, X1E_W), f32),
        compiler_params=pltpu.CompilerParams(
            dimension_semantics=("arbitrary",), vmem_limit_bytes=VMEM_LIMIT),
        name="moe_permute",
    )(dest, pad_start, pad_valid, x1e)


def _moe_kernel(nact_ref, ea_ref, eb_ref, newa_ref, newb_ref, xs_ref,
                wg0_ref, wu0_ref, wd0_ref, wg1_ref, wu1_ref, wd1_ref, ys_ref, wgu_ref, wd_ref):
    j = pl.program_id(0)

    @pl.when(j < nact_ref[0])
    def _tile():
        for slot, (new_ref, wg_ref, wu_ref, wdn_ref) in enumerate(((newa_ref, wg0_ref, wu0_ref, wd0_ref),
                                                                   (newb_ref, wg1_ref, wu1_ref, wd1_ref))):
            @pl.when(new_ref[j] != 0)
            def _recast():
                wgu_ref[2 * slot] = wg_ref[0, 0].astype(bf16)
                wgu_ref[2 * slot + 1] = wu_ref[0, 0].astype(bf16)
                wd_ref[slot] = wdn_ref[0, 0].astype(bf16)

        xb = xs_ref[:, :D_MODEL].astype(bf16)
        acc = None
        for slot in range(2):
            gate = jnp.dot(xb, wgu_ref[2 * slot], preferred_element_type=f32)
            up = jnp.dot(xb, wgu_ref[2 * slot + 1], preferred_element_type=f32)
            hid = (gate * _sigmoid(gate)) * up * xs_ref[:, D_MODEL + slot:D_MODEL + slot + 1]
            part = jnp.dot(hid.astype(bf16), wd_ref[slot], preferred_element_type=f32)
            acc = part if acc is None else acc + part
        ys_ref[...] = acc

    @pl.when(pl.program_id(0) >= nact_ref[0])
    def _unused_tile():
        ys_ref[...] = jnp.zeros_like(ys_ref)


def _moe(xs, tiles, w_gate, w_up, w_down, layer):
    n_tiles = xs.shape[0] // MOE_TILE
    tile = lambda j, na, ea, eb, ca, cb: (jnp.minimum(j, na[0] - 1), 0)
    low = lambda j, na, ea, eb, ca, cb: (layer, ea[j], 0, 0)
    high = lambda j, na, ea, eb, ca, cb: (layer, eb[j], 0, 0)
    up_shape = (1, 1, D_MODEL, D_FF_EXPERT)
    down_shape = (1, 1, D_FF_EXPERT, D_MODEL)
    return pl.pallas_call(
        _moe_kernel,
        grid_spec=pltpu.PrefetchScalarGridSpec(
            num_scalar_prefetch=5,
            grid=(n_tiles,),
            in_specs=[
                pl.BlockSpec((MOE_TILE, X1E_W), tile),
                pl.BlockSpec(up_shape, low), pl.BlockSpec(up_shape, low), pl.BlockSpec(down_shape, low),
                pl.BlockSpec(up_shape, high), pl.BlockSpec(up_shape, high), pl.BlockSpec(down_shape, high),
            ],
            out_specs=pl.BlockSpec((MOE_TILE, D_MODEL), lambda j, na, ea, eb, ca, cb: (j, 0)),
            scratch_shapes=[pltpu.VMEM((4, D_MODEL, D_FF_EXPERT), bf16), pltpu.VMEM((2, D_FF_EXPERT, D_MODEL), bf16)],
        ),
        out_shape=jax.ShapeDtypeStruct((n_tiles * MOE_TILE, D_MODEL), f32),
        compiler_params=pltpu.CompilerParams(
            dimension_semantics=("arbitrary",), vmem_limit_bytes=VMEM_LIMIT),
        name="moe_ffn",
    )(*tiles, xs, w_gate, w_up, w_down, w_gate, w_up, w_down)


def _unpermute_kernel(dest_ref, x1_ref, ys_hbm, lg_ref, lb_ref, out_ref, outb_ref, ybuf, sem):
    i = pl.program_id(0)
    n = pl.num_programs(0)
    tu = x1_ref.shape[0]

    def gather(tile, slot):
        base = tile * tu

        def issue(g, carry):
            r8 = pl.multiple_of(g * SUBLANES, SUBLANES)
            for j in range(SUBLANES):
                d = dest_ref[base + r8 + j]
                pltpu.make_async_copy(ys_hbm.at[pl.ds(d, 1)], ybuf.at[slot, pl.ds(r8 + j, 1)],
                                      sem.at[slot]).start()
            return carry

        lax.fori_loop(0, tu // SUBLANES, issue, 0)

    slot = lax.rem(i, 2)

    @pl.when(i == 0)
    def _first():
        gather(0, 0)

    @pl.when(i + 1 < n)
    def _next():
        gather(i + 1, 1 - slot)

    pltpu.make_async_copy(ys_hbm.at[pl.ds(0, tu)], ybuf.at[slot], sem.at[slot]).wait()
    y = _layer_norm(DEEPNORM_ALPHA * x1_ref[...] + ybuf[slot], lg_ref[...], lb_ref[...])
    out_ref[...] = y
    outb_ref[...] = y.astype(bf16)


def _unpermute(x1e, ys, dest, ln_g, ln_b, tu=256):
    s = x1e.shape[0]
    row = lambda i, d: (i, 0)
    fixed = lambda i, d: (0, 0)
    return pl.pallas_call(
        _unpermute_kernel,
        grid_spec=pltpu.PrefetchScalarGridSpec(
            num_scalar_prefetch=1,
            grid=(s // tu,),
            in_specs=[
                pl.BlockSpec((tu, D_MODEL), row),
                pl.BlockSpec(memory_space=pl.ANY),
                pl.BlockSpec((1, D_MODEL), fixed),
                pl.BlockSpec((1, D_MODEL), fixed),
            ],
            out_specs=[pl.BlockSpec((tu, D_MODEL), row), pl.BlockSpec((tu, D_MODEL), row)],
            scratch_shapes=[pltpu.VMEM((2, tu, D_MODEL), f32), pltpu.SemaphoreType.DMA((2,))],
        ),
        out_shape=[jax.ShapeDtypeStruct((s, D_MODEL), f32), jax.ShapeDtypeStruct((s, D_MODEL), bf16)],
        compiler_params=pltpu.CompilerParams(
            dimension_semantics=("arbitrary",), vmem_limit_bytes=VMEM_LIMIT),
        name="moe_unpermute_ln2",
    )(dest, x1e, ys, ln_g, ln_b)


def _pad_lanes(a, width=LANES):
    return jnp.pad(a, ((0, 0), (0, width - a.shape[1])))


def kernel(x, w_in, conv_w, dn_a_log, dn_dt_bias, dn_norm_w, p_a, p_b, w_out, ln1_g, ln1_b, w_router, router_bias, w_gate, w_up, w_down, ln2_g, ln2_b):
    bsz, s, _ = x.shape
    assert bsz == 1 and s % 512 == 0
    xf = x[0]
    xb = xf

    c_ba = 4 * DN_WIDTH
    c_sb = c_ba + 2 * DN_HEADS
    c_gate = c_sb + SB_SLAB
    wr_t = w_router.T
    wr_hi = wr_t.astype(bf16)
    wr_cat = jnp.concatenate([wr_hi, (wr_t - wr_hi.astype(f32)).astype(bf16)], axis=0)
    rb_col = router_bias[:, None]
    head_pad = ((0, 0), (DN_HEADS, LANES - 2 * DN_HEADS))
    w_in_t = jnp.swapaxes(w_in, 1, 2)

    for l in range(DEPTH):
        w = w_in_t[l]
        w_dn = jnp.concatenate([w[:c_ba], jnp.pad(w[c_ba:c_sb], ((0, LANES - 2 * DN_HEADS), (0, 0)))],
                               axis=0).astype(bf16)
        w_sb = jnp.concatenate([w[c_sb:c_sb + SB_WIDTH] * (SB_DH ** -0.5),
                                w[c_sb + SB_WIDTH:c_gate]], axis=0).astype(bf16)
        w_g = w[c_gate:].astype(bf16)

        p_dn, p_sb, gates = _proj(xb, w_dn, w_sb, w_g)
        o_a = _deltanet(p_dn,
                        jnp.pad(conv_w[l], ((0, 8 - CONV_K), (0, 0))),
                        jnp.pad(dn_a_log[l][None, :], head_pad),
                        jnp.pad(dn_dt_bias[l][None, :], head_pad),
                        dn_norm_w[l][None, :])
        o_b = _sb_attention(p_sb)
        x1e, route, cnt = _merge(xf, o_a, o_b, gates,
                                p_a[l].astype(bf16), p_b[l].astype(bf16), w_out[l].astype(bf16),
                                ln1_g[l][None, :], ln1_b[l][None, :], wr_cat, rb_col)
        dest, tiles, pad_start, pad_valid = _route_tables(route, cnt, s)
        xs = _permute(x1e, dest, pad_start, pad_valid)
        ys = _moe(xs, tiles, w_gate, w_up, w_down, l)
        xf, xb = _unpermute(x1e, ys, dest, ln2_g[l][None, :], ln2_b[l][None, :])
    return xf[None]
```

```python
import jax
import jax.numpy as jnp
from jax import lax
from jax.experimental import pallas as pl
from jax.experimental.pallas import tpu as pltpu

f32 = jnp.float32
bf16 = jnp.bfloat16
HIGHEST = lax.Precision.HIGHEST

D_MODEL = 1024
DEPTH = 2
CHUNK = 64
DN_HEADS = 4
DN_DK = 128
DN_DV = 128
CONV_K = 4
SB_HEADS = 8
SB_DH = 64
SB_BLOCK = 128
N_EXPERTS = 16
N_GROUPS = 4
EXPERTS_PER_GROUP = N_EXPERTS // N_GROUPS
D_FF_EXPERT = 512
LN_EPS = 1e-5
RMS_EPS = 1e-6
DEEPNORM_ALPHA = (2 * DEPTH) ** 0.25

DN_WIDTH = DN_HEADS * DN_DK
SB_WIDTH = SB_HEADS * SB_DH
CONV_CH = 3 * DN_WIDTH
LANES = 128
SUBLANES = 8
DN_SLAB = 4 * DN_WIDTH + LANES
BA_COL = 4 * DN_WIDTH
SB_SLAB = 3 * SB_WIDTH
GATE_SLAB = 2 * D_MODEL

SB_LOG_ZERO = -88.0
SB_STATIC_BLOCKS = 2
SB_MASK_PENALTY = -1e30

VMEM_LIMIT = 48 * 1024 * 1024


def _sigmoid(x):
    return 1.0 / (1.0 + jnp.exp(-x))


def _softplus(x):
    return jnp.maximum(x, 0.0) + jnp.log(1.0 + jnp.exp(-jnp.abs(x)))


def _nt_dot(a, b):
    return lax.dot_general(a, b, (((1,), (1,)), ((), ())), preferred_element_type=f32)


def _tn_dot(a, b):
    return lax.dot_general(a, b, (((0,), (0,)), ((), ())), preferred_element_type=f32)


def _layer_norm(y, g, b):
    mu = jnp.mean(y, axis=-1, keepdims=True)
    d = y - mu
    var = jnp.mean(d * d, axis=-1, keepdims=True)
    return d * lax.rsqrt(var + LN_EPS) * g + b


def _proj_kernel(x_ref, wdn_ref, wsb_ref, wg_ref, odn_ref, osb_ref, og_ref):
    x = x_ref[...].astype(bf16)
    odn_ref[...] = _nt_dot(x, wdn_ref[...])
    osb_ref[...] = _nt_dot(x, wsb_ref[...]).astype(bf16)
    og_ref[...] = _nt_dot(x, wg_ref[...]).astype(bf16)


def _proj(x, w_dn, w_sb, w_g, tm=512):
    s = x.shape[0]
    row = lambda i: (i, 0)
    fixed = lambda i: (0, 0)
    return pl.pallas_call(
        _proj_kernel,
        grid=(s // tm,),
        in_specs=[
            pl.BlockSpec((tm, D_MODEL), row),
            pl.BlockSpec((DN_SLAB, D_MODEL), fixed),
            pl.BlockSpec((SB_SLAB, D_MODEL), fixed),
            pl.BlockSpec((GATE_SLAB, D_MODEL), fixed),
        ],
        out_specs=[
            pl.BlockSpec((tm, DN_SLAB), row),
            pl.BlockSpec((tm, SB_SLAB), row),
            pl.BlockSpec((tm, GATE_SLAB), row),
        ],
        out_shape=[
            jax.ShapeDtypeStruct((s, DN_SLAB), f32),
            jax.ShapeDtypeStruct((s, SB_SLAB), bf16),
            jax.ShapeDtypeStruct((s, GATE_SLAB), bf16),
        ],
        compiler_params=pltpu.CompilerParams(
            dimension_semantics=("arbitrary",), vmem_limit_bytes=VMEM_LIMIT),
        name="proj",
    )(x, w_dn, w_sb, w_g)


DN_ROWS = 1024
PAIR = 2 * CHUNK
TAIL = 8
DN_GROUP = 2
DN_SEQ_START = 22
DN_SEQ_EVERY = 5


def _split2(x):
    hi = x.astype(bf16)
    return hi, (x - hi.astype(f32)).astype(bf16)


def _split3(x):
    hi = x.astype(bf16)
    r = x - hi.astype(f32)
    mid = r.astype(bf16)
    return hi, mid, (r - mid.astype(f32)).astype(bf16)


def _dn_kernel(p_ref, cw_ref, alog_ref, dtb_ref, nw_ref, o_ref, state_ref, tail_ref, xe_ref):
    step = pl.program_id(0)
    rows = p_ref.shape[0]
    n_pb = rows // PAIR

    @pl.when(step == 0)
    def _init():
        state_ref[...] = jnp.zeros_like(state_ref)
        tail_ref[...] = jnp.zeros_like(tail_ref)

    xe_ref[0:TAIL, :] = tail_ref[...]
    xe_ref[TAIL:TAIL + rows, :] = p_ref[:, 0:CONV_CH]
    tail_ref[...] = p_ref[rows - TAIL:rows, 0:CONV_CH]

    row128 = lax.broadcasted_iota(jnp.int32, (PAIR, LANES), 0)
    lane128 = lax.broadcasted_iota(jnp.int32, (PAIR, LANES), 1)
    same_chunk = (row128 >= CHUNK) == (lane128 >= CHUNK)
    tril_bd = jnp.where(jnp.logical_and(row128 >= lane128, same_chunk), 1.0, 0.0).astype(bf16)
    triu_bd = jnp.where(jnp.logical_and(row128 <= lane128, same_chunk), 1.0, 0.0).astype(bf16)
    first_rows = row128 < CHUNK
    row64 = lax.broadcasted_iota(jnp.int32, (CHUNK, LANES), 0)
    lane64 = lax.broadcasted_iota(jnp.int32, (CHUNK, LANES), 1)
    left = lane64 < CHUNK
    col_in_chunk = jnp.bitwise_and(lane64, CHUNK - 1)
    tri_p = row64 >= col_in_chunk
    strict_p = row64 > col_in_chunk
    eye_p = jnp.where(row64 == col_in_chunk, 1.0, 0.0).astype(f32)

    def block_diag(z):
        zero = jnp.zeros_like(z)
        return jnp.concatenate([jnp.where(left, z, zero), jnp.where(left, zero, z)], axis=0)

    def pair_matmul(y_hi, y_lo, zbd_hi, zbd_lo):
        return (jnp.dot(jnp.concatenate([y_hi, y_lo], axis=1), jnp.concatenate([zbd_hi, zbd_hi], axis=0),
                        preferred_element_type=f32)
                + jnp.dot(y_hi, zbd_lo, preferred_element_type=f32))

    nw = nw_ref[...]
    heads = range(DN_HEADS)
    base = TAIL - (CONV_K - 1)

    def pair_block(pb):
        r0 = pb * PAIR
        ba = p_ref[r0:r0 + PAIR, BA_COL:BA_COL + LANES]
        beta_all = _sigmoid(ba)
        parts = _split3(-jnp.exp(alog_ref[...]) * _softplus(ba + dtb_ref[...]))
        gcol = sum(jnp.dot(tril_bd, pt, preferred_element_type=f32) for pt in parts)
        grow = sum(_tn_dot(pt, triu_bd) for pt in parts)
        yield

        qkv = []
        for grp in range(3):
            outs = []
            for h in heads:
                col = grp * DN_WIDTH + h * DN_DK
                acc = xe_ref[base + r0:base + r0 + PAIR, col:col + DN_DK] * cw_ref[0:1, col:col + DN_DK]
                for j in range(1, CONV_K):
                    acc = acc + (xe_ref[base + r0 + j:base + r0 + j + PAIR, col:col + DN_DK]
                                 * cw_ref[j:j + 1, col:col + DN_DK])
                y = acc * _sigmoid(acc)
                if grp < 2:
                    y = y * lax.rsqrt(jnp.sum(y * y, axis=-1, keepdims=True) + RMS_EPS)
                outs.append(y * (DN_DK ** -0.5) if grp == 0 else y)
                yield
            qkv.append(outs)
        qs, ks, vs = qkv

        kbetas, qgs, kdecs, rstacks, decays, egl = [], [], [], [], [], []
        for h in heads:
            gc = jnp.broadcast_to(gcol[:, DN_HEADS + h:DN_HEADS + h + 1], (PAIR, LANES))
            beta = jnp.broadcast_to(beta_all[:, h:h + 1], (PAIR, LANES))
            eg = jnp.exp(gc)
            glast = jnp.where(first_rows, gc[CHUNK - 1:CHUNK, :], gc[PAIR - 1:PAIR, :])
            kbeta = ks[h] * beta
            kbetas.append(kbeta)
            qgs.append(qs[h] * eg)
            kdecs.append((ks[h] * jnp.exp(glast - gc)).astype(bf16))
            rstacks.append(jnp.concatenate([kbeta * eg, vs[h] * beta], axis=1).astype(bf16))
            gdiff = jnp.where(left, gc[:CHUNK], gc[CHUNK:]) - grow[DN_HEADS + h:DN_HEADS + h + 1, :]
            decays.append(jnp.where(tri_p, jnp.exp(jnp.where(tri_p, gdiff, 0.0)), 0.0))
            egl.append((jnp.exp(gc[CHUNK - 1:CHUNK, :]), jnp.exp(gc[PAIR - 1:PAIR, :])))
            yield

        lps, qkms = [], []
        for h in heads:
            kk = _nt_dot(jnp.concatenate([kbetas[h], qs[h]], axis=0).astype(bf16), ks[h].astype(bf16))
            lps.append(jnp.where(strict_p, jnp.where(left, kk[0:CHUNK], kk[CHUNK:PAIR]) * decays[h], 0.0))
            qkms.append(jnp.where(tri_p, jnp.where(left, kk[PAIR:PAIR + CHUNK], kk[PAIR + CHUNK:]) * decays[h], 0.0)
                        .astype(bf16))
            yield

        pw = [_split2(lp) for lp in lps]
        pw_bd = [(block_diag(hi), block_diag(lo)) for hi, lo in pw]
        ts = [eye_p - lp for lp in lps]
        for _ in range(5):
            pw = [_split2(pair_matmul(*pw[h], *pw_bd[h])) for h in heads]
            pw_bd = [(block_diag(hi), block_diag(lo)) for hi, lo in pw]
            yield
            ts = [ts[h] + pair_matmul(*_split2(ts[h]), *pw_bd[h]) for h in heads]
            yield

        zero_p = jnp.zeros((CHUNK, LANES), bf16)
        mq, bo = [], []
        for h in heads:
            t16 = ts[h].astype(bf16)
            halves = (jnp.where(left, t16, zero_p), jnp.where(left, zero_p, t16))
            qk_halves = (jnp.where(left, qkms[h], zero_p), jnp.where(left, zero_p, qkms[h]))
            wus = [jnp.dot(th, rstacks[h], preferred_element_type=f32).astype(bf16) for th in halves]
            wu_stack = jnp.concatenate(wus, axis=0)
            mq_h, bo_h = [], []
            for c in range(2):
                cs = slice(c * CHUNK, (c + 1) * CHUNK)
                kw = _tn_dot(kdecs[h][cs], wus[c])
                qw = jnp.dot(qk_halves[c], wu_stack, preferred_element_type=f32)
                mq_h.append(jnp.concatenate([kw[:, :DN_DK], qgs[h][cs] - qw[:, :DN_DK]], axis=0).astype(bf16))
                bo_h.append((kw[:, DN_DK:], qw[:, DN_DK:]))
            mq.append(mq_h)
            bo.append(bo_h)
            yield

        prepared[pb] = (mq, bo, egl)

    states = [state_ref[h] for h in heads]

    def sequential(pbs):
        for pb in pbs:
            mq, bo, egl = prepared[pb]
            for c in range(2):
                for h in heads:
                    ms = jnp.dot(mq[h][c], states[h].astype(bf16), preferred_element_type=f32)
                    b_c, o_c = bo[h][c]
                    o = ms[DN_DK:] + o_c
                    states[h] = states[h] * egl[h][c] - ms[:DN_DK] + b_c
                    o = o * lax.rsqrt(jnp.mean(o * o, axis=-1, keepdims=True) + RMS_EPS) * nw
                    rc = pb * PAIR + c * CHUNK
                    z = p_ref[rc:rc + CHUNK, 3 * DN_WIDTH + h * DN_DV:3 * DN_WIDTH + (h + 1) * DN_DV]
                    o_ref[rc:rc + CHUNK, h * DN_DV:(h + 1) * DN_DV] = o * (z * _sigmoid(z))
                yield

    prepared = [None] * n_pb
    pending = iter(())
    for g0 in range(0, n_pb, DN_GROUP):
        group = range(g0, min(g0 + DN_GROUP, n_pb))
        pipes = [pair_block(pb) for pb in group]
        slot = 0
        while pipes:
            pipes = [pipe for pipe in pipes if next(pipe, "done") != "done"]
            slot += 1
            if slot >= DN_SEQ_START and (slot - DN_SEQ_START) % DN_SEQ_EVERY == 0:
                next(pending, None)
        for _ in pending:
            pass
        pending = sequential(group)
    for _ in pending:
        pass
    for h in heads:
        state_ref[h] = states[h]


def _deltanet(p_dn, conv_w8, alog_row, dtb_row, nw_row):
    s = p_dn.shape[0]
    fixed = lambda i: (0, 0)
    return pl.pallas_call(
        _dn_kernel,
        grid=(s // DN_ROWS,),
        in_specs=[
            pl.BlockSpec((DN_ROWS, DN_SLAB), lambda i: (i, 0)),
            pl.BlockSpec((8, CONV_CH), fixed),
            pl.BlockSpec((1, LANES), fixed),
            pl.BlockSpec((1, LANES), fixed),
            pl.BlockSpec((1, DN_DV), fixed),
        ],
        out_specs=pl.BlockSpec((DN_ROWS, DN_HEADS * DN_DV), lambda i: (i, 0)),
        out_shape=jax.ShapeDtypeStruct((s, DN_HEADS * DN_DV), f32),
        scratch_shapes=[
            pltpu.VMEM((DN_HEADS, DN_DK, DN_DV), f32),
            pltpu.VMEM((TAIL, CONV_CH), f32),
            pltpu.VMEM((TAIL + DN_ROWS, CONV_CH), f32),
        ],
        compiler_params=pltpu.CompilerParams(
            dimension_semantics=("arbitrary",), vmem_limit_bytes=VMEM_LIMIT),
        name="deltanet",
    )(p_dn, conv_w8, alog_row, dtb_row, nw_row)


N_PAIRS = SB_HEADS // 2
SB_QBLOCKS = 2


def _sb_kernel(q_ref, kc_ref, kp_ref, kpp_ref, vc_ref, vp_ref, vpp_ref, kv_hbm, o_ref, kbuf, vbuf, sem):
    cur = [pl.ds(sub * SB_BLOCK, SB_BLOCK) for sub in range(SB_QBLOCKS)]
    step = pl.program_id(0)
    _sb_query_block(step * SB_QBLOCKS, q_ref.at[cur[0]], (kc_ref.at[cur[0]], kp_ref, kpp_ref),
                    (vc_ref.at[cur[0]], vp_ref, vpp_ref), kv_hbm, o_ref.at[cur[0]], kbuf, vbuf, sem)
    _sb_query_block(step * SB_QBLOCKS + 1, q_ref.at[cur[1]], (kc_ref.at[cur[1]], kc_ref.at[cur[0]], kp_ref),
                    (vc_ref.at[cur[1]], vc_ref.at[cur[0]], vp_ref), kv_hbm, o_ref.at[cur[1]], kbuf, vbuf, sem)


def _sb_query_block(qb, q_ref, k_blocks, v_blocks, kv_hbm, o_ref, kbuf, vbuf, sem):
    blk = SB_BLOCK
    row = lax.broadcasted_iota(jnp.int32, (blk, blk), 0)
    lane = lax.broadcasted_iota(jnp.int32, (blk, blk), 1)
    diag_mask = row > lane
    even = lane < SB_DH
    suffix = jnp.where(row > lane, 1.0, 0.0).astype(bf16)
    suffix2 = jnp.concatenate([suffix, suffix], axis=0)

    def split_heads(x):
        zero = jnp.zeros_like(x)
        return jnp.where(even, x, zero), jnp.where(even, zero, x)

    def suffix_sums(spm):
        hi = spm.astype(bf16)
        lo = (spm - hi.astype(f32)).astype(bf16)
        return jnp.dot(jnp.concatenate([hi, lo], axis=1), suffix2, preferred_element_type=f32)

    pairs = [slice(p * LANES, (p + 1) * LANES) for p in range(N_PAIRS)]
    k_refs, v_refs = k_blocks[:SB_STATIC_BLOCKS], v_blocks[:SB_STATIC_BLOCKS]

    def fetch(kb):
        start = pl.multiple_of(kb * blk, blk)
        return (pltpu.make_async_copy(kv_hbm.at[pl.ds(start, blk), pl.ds(SB_WIDTH, SB_WIDTH)], kbuf, sem.at[0]),
                pltpu.make_async_copy(kv_hbm.at[pl.ds(start, blk), pl.ds(2 * SB_WIDTH, SB_WIDTH)], vbuf,
                                      sem.at[1]))

    def live(cr):
        m = cr[0]
        for c in cr[1:]:
            m = jnp.minimum(m, c)
        return jnp.min(m) <= -SB_LOG_ZERO

    pen1 = jnp.where(qb >= 1, 0.0, -SB_MASK_PENALTY).astype(f32)
    q_heads = [split_heads(q_ref[:, ps]) for ps in pairs]
    units = [(p, b, hh) for b in range(SB_STATIC_BLOCKS) for p in range(N_PAIRS) for hh in range(2)]
    z = {u: _nt_dot(q_heads[u[0]][u[2]], k_refs[u[1]][:, pairs[u[0]]]) for u in units}
    sp = {u: _softplus(z[u]) for u in units}
    spm = {u: (jnp.where(diag_mask, sp[u], 0.0) if u[1] == 0 else sp[u]) for u in units}
    logw = {u: z[u] - sp[u] - suffix_sums(spm[u]) for u in units}
    keep = {u: jnp.sum(spm[u], axis=1, keepdims=True) for u in units}
    carries = [keep[p, 0, hh] + keep[p, 1, hh] for p in range(N_PAIRS) for hh in range(2)]
    att = {}
    for p, b, hh in units:
        if b == 0:
            att[p, b, hh] = jnp.where(diag_mask, jnp.exp(logw[p, b, hh]), 0.0).astype(bf16)
        else:
            att[p, b, hh] = jnp.exp(logw[p, b, hh] - (keep[p, 0, hh] + pen1)).astype(bf16)
    accs = []
    for p in range(N_PAIRS):
        vals = jnp.concatenate([h for b in range(SB_STATIC_BLOCKS)
                                for h in split_heads(v_refs[b][:, pairs[p]])], axis=0)
        lhs = jnp.concatenate([att[p, b, hh] for b in range(SB_STATIC_BLOCKS) for hh in range(2)], axis=1)
        accs.append(jnp.dot(lhs, vals, preferred_element_type=f32))

    def cond(st):
        kb, go = st[0], st[1]
        return jnp.logical_and(kb >= 0, go)

    def body(st):
        kb = st[0]
        acc_l = list(st[2:2 + N_PAIRS])
        car_l = list(st[2 + N_PAIRS:])
        resident = kb == qb - SB_STATIC_BLOCKS

        @pl.when(resident)
        def _copy():
            kbuf[...] = k_blocks[SB_STATIC_BLOCKS][...]
            vbuf[...] = v_blocks[SB_STATIC_BLOCKS][...]

        @pl.when(jnp.logical_not(resident))
        def _fetch():
            for cp in fetch(kb):
                cp.start()
            for cp in fetch(kb):
                cp.wait()

        new_car = []
        for p, ps in enumerate(pairs):
            atts = []
            for hh, qh in enumerate(split_heads(q_ref[:, ps])):
                zz = _nt_dot(qh, kbuf[:, ps])
                spp = _softplus(zz)
                carry = car_l[2 * p + hh]
                atts.append(jnp.exp(zz - spp - suffix_sums(spp) - carry).astype(bf16))
                new_car.append(carry + jnp.sum(spp, axis=1, keepdims=True))
            vals = jnp.concatenate(split_heads(vbuf[:, ps]), axis=0)
            acc_l[p] = acc_l[p] + jnp.dot(jnp.concatenate(atts, axis=1), vals, preferred_element_type=f32)
        return (kb - 1, live(new_car), *acc_l, *new_car)

    st = lax.while_loop(cond, body, (qb - SB_STATIC_BLOCKS, live(carries), *accs, *carries))
    for p, ps in enumerate(pairs):
        o_ref[:, ps] = st[2 + p]


def _sb_attention(p_sb):
    s = p_sb.shape[0]
    blk = SB_BLOCK

    step_rows = SB_QBLOCKS * blk

    def current(col):
        return pl.BlockSpec((step_rows, SB_WIDTH), lambda i: (i, col))

    def previous(col, back):
        return pl.BlockSpec((blk, SB_WIDTH), lambda i: (jnp.maximum(i * SB_QBLOCKS - back, 0), col))

    return pl.pallas_call(
        _sb_kernel,
        grid=(s // step_rows,),
        in_specs=[current(0), current(1), previous(1, 1), previous(1, 2), current(2), previous(2, 1), previous(2, 2),
                  pl.BlockSpec(memory_space=pl.ANY)],
        out_specs=pl.BlockSpec((step_rows, SB_WIDTH), lambda i: (i, 0)),
        out_shape=jax.ShapeDtypeStruct((s, SB_WIDTH), f32),
        scratch_shapes=[
            pltpu.VMEM((blk, SB_WIDTH), bf16),
            pltpu.VMEM((blk, SB_WIDTH), bf16),
            pltpu.SemaphoreType.DMA((2,)),
        ],
        compiler_params=pltpu.CompilerParams(
            dimension_semantics=("arbitrary",), vmem_limit_bytes=VMEM_LIMIT),
        name="sb_attention",
    )(p_sb, p_sb, p_sb, p_sb, p_sb, p_sb, p_sb, p_sb)


MOE_TILE = 512
PAIRS_PER_GROUP = 6
N_CLASSES = N_GROUPS * PAIRS_PER_GROUP
X1E_W = D_MODEL + LANES
MERGE_ROWS = 512
MERGE_PARTS = 2


def _route(aff, sel):
    neg = -jnp.inf
    best = None
    for gidx in range(N_GROUPS):
        rows = slice(gidx * EXPERTS_PER_GROUP, (gidx + 1) * EXPERTS_PER_GROUP)
        sg = sel[rows]
        idx = lax.broadcasted_iota(jnp.int32, sg.shape, 0) + gidx * EXPERTS_PER_GROUP
        m1 = jnp.max(sg, axis=0, keepdims=True)
        i1 = jnp.min(jnp.where(sg == m1, idx, N_EXPERTS), axis=0, keepdims=True)
        sg2 = jnp.where(idx == i1, neg, sg)
        m2 = jnp.max(sg2, axis=0, keepdims=True)
        i2 = jnp.min(jnp.where(sg2 == m2, idx, N_EXPERTS), axis=0, keepdims=True)
        score = m1 + m2
        if best is None:
            best = (score, i1, i2)
        else:
            better = score > best[0]
            best = (jnp.where(better, score, best[0]),
                    jnp.where(better, i1, best[1]),
                    jnp.where(better, i2, best[2]))
    _, i1, i2 = best
    expert = lax.broadcasted_iota(jnp.int32, aff.shape, 0)
    w1 = jnp.sum(jnp.where(expert == i1, aff, 0.0), axis=0, keepdims=True)
    w2 = jnp.sum(jnp.where(expert == i2, aff, 0.0), axis=0, keepdims=True)
    denom = w1 + w2
    w1, w2 = w1 / denom, w2 / denom
    first_low = i1 < i2
    e_lo = jnp.minimum(i1, i2)
    e_hi = jnp.maximum(i1, i2)
    a = jnp.bitwise_and(e_lo, EXPERTS_PER_GROUP - 1)
    b = jnp.bitwise_and(e_hi, EXPERTS_PER_GROUP - 1)
    pair = jnp.where(a == 0, 0, jnp.where(a == 1, 3, 5)) + (b - a - 1)
    order = jnp.where(pair == 3, 4, jnp.where(pair == 4, 3, pair))
    cls = lax.shift_right_logical(e_lo, 2) * PAIRS_PER_GROUP + order
    w_lo, w_hi = jnp.where(first_low, w1, w2), jnp.where(first_low, w2, w1)
    swapped = pair == PAIRS_PER_GROUP - 1
    return cls, jnp.where(swapped, w_hi, w_lo), jnp.where(swapped, w_lo, w_hi)


CLASS_ROWS = 32


def _merge_kernel(x_ref, oa_ref, ob_ref, g_ref, pa_ref, pb_ref, wo_ref, lg_ref, lb_ref, wr_ref, rb_ref,
                  x1e_ref, route_ref, cnt_ref, run_ref):
    step = pl.program_id(0)

    @pl.when(step == 0)
    def _init():
        run_ref[...] = jnp.zeros_like(run_ref)

    rows = x_ref.shape[0] // MERGE_PARTS
    parts = [slice(i * rows, (i + 1) * rows) for i in range(MERGE_PARTS)]
    a = [jnp.dot(oa_ref[p, :].astype(bf16), pa_ref[...], preferred_element_type=f32) for p in parts]
    b = [jnp.dot(ob_ref[p, :].astype(bf16), pb_ref[...], preferred_element_type=f32) for p in parts]
    merged = [(_sigmoid(g_ref[p, :D_MODEL].astype(f32)) * ai
               + _sigmoid(g_ref[p, D_MODEL:].astype(f32)) * bi).astype(bf16)
              for p, ai, bi in zip(parts, a, b)]
    mix = [jnp.dot(m, wo_ref[...], preferred_element_type=f32) for m in merged]
    x1 = [_layer_norm(DEEPNORM_ALPHA * x_ref[p, :] + mi, lg_ref[...], lb_ref[...]) for p, mi in zip(parts, mix)]
    x1b = [xi.astype(bf16) for xi in x1]
    for p, xi in zip(parts, x1):
        x1e_ref[p, :D_MODEL] = xi

    x_lo = [(xi - xbi.astype(f32)).astype(bf16) for xi, xbi in zip(x1, x1b)]
    t = [_nt_dot(wr_ref[...], xbi) for xbi in x1b]
    logits = [ti[:N_EXPERTS] + ti[N_EXPERTS:] + _nt_dot(wr_ref[:N_EXPERTS, :], xl) for ti, xl in zip(t, x_lo)]
    aff = [_sigmoid(lg) for lg in logits]
    routed = [_route(af, af + rb_ref[...]) for af in aff]

    rr = lax.broadcasted_iota(jnp.int32, (rows, rows), 0)
    cc = lax.broadcasted_iota(jnp.int32, (rows, rows), 1)
    triu = jnp.where(rr <= cc, 1.0, 0.0).astype(bf16)
    class_id = lax.broadcasted_iota(jnp.int32, (CLASS_ROWS, rows), 0)
    for p, (cls, w_lo, w_hi) in zip(parts, routed):
        onehot = class_id == cls
        prefix = jnp.dot(jnp.where(onehot, 1.0, 0.0).astype(bf16), triu, preferred_element_type=f32)
        run = run_ref[...]
        rank = (jnp.sum(jnp.where(onehot, prefix + run, 0.0), axis=0, keepdims=True) - 1.0).astype(jnp.int32)
        run_ref[...] = run + prefix[:, rows - 1:rows]
        pad = jnp.zeros((SUBLANES - 3, rows), jnp.int32)
        route_ref[0, :, p] = jnp.concatenate(
            [cls, lax.shift_right_logical(rank, 7), jnp.bitwise_and(rank, LANES - 1), pad], axis=0)
        w_rows = jnp.concatenate([w_lo, w_hi, jnp.zeros((LANES - 2, rows), f32)], axis=0)
        x1e_ref[p, D_MODEL:] = w_rows.T
    cnt_ref[...] = jnp.broadcast_to(run_ref[...], cnt_ref.shape).astype(jnp.int32)


def _merge(x, o_a, o_b, gates, p_a, p_b, w_out, ln_g, ln_b, wr_cat, r_bias):
    s = x.shape[0]
    tm = MERGE_ROWS
    row = lambda i: (i, 0)
    fixed = lambda i: (0, 0)
    return pl.pallas_call(
        _merge_kernel,
        grid=(s // tm,),
        in_specs=[
            pl.BlockSpec((tm, D_MODEL), row),
            pl.BlockSpec((tm, DN_HEADS * DN_DV), row),
            pl.BlockSpec((tm, SB_WIDTH), row),
            pl.BlockSpec((tm, GATE_SLAB), row),
            pl.BlockSpec((DN_HEADS * DN_DV, D_MODEL), fixed),
            pl.BlockSpec((SB_WIDTH, D_MODEL), fixed),
            pl.BlockSpec((D_MODEL, D_MODEL), fixed),
            pl.BlockSpec((1, D_MODEL), fixed),
            pl.BlockSpec((1, D_MODEL), fixed),
            pl.BlockSpec((2 * N_EXPERTS, D_MODEL), fixed),
            pl.BlockSpec((N_EXPERTS, 1), fixed),
        ],
        out_specs=[
            pl.BlockSpec((tm, X1E_W), row),
            pl.BlockSpec((1, SUBLANES, tm), lambda i: (i, 0, 0)),
            pl.BlockSpec((CLASS_ROWS, LANES), fixed),
        ],
        out_shape=[
            jax.ShapeDtypeStruct((s, X1E_W), f32),
            jax.ShapeDtypeStruct((s // tm, SUBLANES, tm), jnp.int32),
            jax.ShapeDtypeStruct((CLASS_ROWS, LANES), jnp.int32),
        ],
        scratch_shapes=[pltpu.VMEM((CLASS_ROWS, 1), f32)],
        compiler_params=pltpu.CompilerParams(
            dimension_semantics=("arbitrary",), vmem_limit_bytes=VMEM_LIMIT),
        name="merge_router",
    )(x, o_a, o_b, gates, p_a, p_b, w_out, ln_g, ln_b, wr_cat, r_bias)


SLOT_PAIRS = ((0, 1), (0, 2), (0, 3), (1, 3), (1, 2), (3, 2))


def _class_experts():
    slot_a = [g * EXPERTS_PER_GROUP + a for g in range(N_GROUPS) for a, _ in SLOT_PAIRS]
    slot_b = [g * EXPERTS_PER_GROUP + b for g in range(N_GROUPS) for _, b in SLOT_PAIRS]
    return jnp.array(slot_a, jnp.int32), jnp.array(slot_b, jnp.int32)


def _n_tiles(s):
    return -(-(s + N_CLASSES * (MOE_TILE - 1)) // MOE_TILE)


def _route_tables(route, cnt, s):
    counts = cnt[:N_CLASSES, 0]
    padded = (counts + (MOE_TILE - 1)) // MOE_TILE * MOE_TILE
    ends = jnp.cumsum(padded)
    offs = ends - padded
    cls, rank_hi, rank_lo = (route[:, r, :].reshape(s) for r in range(3))
    dest = offs[cls] + rank_hi * LANES + rank_lo
    n_active = (ends[-1] // MOE_TILE).astype(jnp.int32)[None]
    tile_row = jnp.minimum(jnp.arange(_n_tiles(s), dtype=jnp.int32) * MOE_TILE, ends[-1] - 1)
    tile_cls = jnp.minimum(jnp.sum(tile_row[:, None] >= ends[None, :], axis=1), N_CLASSES - 1)
    slot_a, slot_b = _class_experts()
    tile_a, tile_b = slot_a[tile_cls], slot_b[tile_cls]
    first = jnp.ones((1,), jnp.int32)
    new_a = jnp.concatenate([first, (tile_a[1:] != tile_a[:-1]).astype(jnp.int32)])
    new_b = jnp.concatenate([first, (tile_b[1:] != tile_b[:-1]).astype(jnp.int32)])
    tail = jnp.arange(s // MOE_TILE, _n_tiles(s), dtype=jnp.int32)
    pad_start = jnp.concatenate([ends - MOE_TILE, tail * MOE_TILE]).astype(jnp.int32)
    pad_valid = jnp.concatenate([padded > 0, tail >= n_active[0]]).astype(jnp.int32)
    return (dest.astype(jnp.int32), (n_active, tile_a, tile_b, new_a, new_b), pad_start, pad_valid)


def _permute_kernel(dest_ref, pstart_ref, pvalid_ref, x_ref, xs_hbm, zero_ref, sem):
    step = pl.program_id(0)
    tp = x_ref.shape[0]

    @pl.when(step == 0)
    def _fill():
        zero_ref[...] = jnp.zeros_like(zero_ref)

        def fill_copy(c):
            start = pl.multiple_of(pstart_ref[c], MOE_TILE)
            return pltpu.make_async_copy(zero_ref, xs_hbm.at[pl.ds(start, MOE_TILE)], sem.at[1])

        for c in range(pstart_ref.shape[0]):
            @pl.when(pvalid_ref[c] != 0)
            def _start():
                fill_copy(c).start()
        for c in range(pstart_ref.shape[0]):
            @pl.when(pvalid_ref[c] != 0)
            def _wait():
                fill_copy(c).wait()

    base = step * tp

    def issue(g, carry):
        r8 = pl.multiple_of(g * SUBLANES, SUBLANES)
        for j in range(SUBLANES):
            d = dest_ref[base + r8 + j]
            pltpu.make_async_copy(x_ref.at[pl.ds(r8 + j, 1)], xs_hbm.at[pl.ds(d, 1)], sem.at[0]).start()
        return carry

    lax.fori_loop(0, tp // SUBLANES, issue, 0)
    pltpu.make_async_copy(x_ref, xs_hbm.at[pl.ds(0, tp)], sem.at[0]).wait()


def _permute(x1e, dest, pad_start, pad_valid, tp=512):
    s = x1e.shape[0]
    return pl.pallas_call(
        _permute_kernel,
        grid_spec=pltpu.PrefetchScalarGridSpec(
            num_scalar_prefetch=3,
            grid=(s // tp,),
            in_specs=[pl.BlockSpec((tp, X1E_W), lambda i, d, ps, pv: (i, 0))],
            out_specs=pl.BlockSpec(memory_space=pl.ANY),
            scratch_shapes=[pltpu.VMEM((MOE_TILE, X1E_W), f32), pltpu.SemaphoreType.DMA((2,))],
        ),
        out_shape=jax.ShapeDtypeStruct((_n_tiles(s) * MOE_TILE, X1E_W), f32),
        compiler_params=pltpu.CompilerParams(
            dimension_semantics=("arbitrary",), vmem_limit_bytes=VMEM_LIMIT),
        name="moe_permute",
    )(dest, pad_start, pad_valid, x1e)


def _moe_kernel(nact_ref, ea_ref, eb_ref, newa_ref, newb_ref, xs_ref,
                wg0_ref, wu0_ref, wd0_ref, wg1_ref, wu1_ref, wd1_ref, ys_ref, wgu_ref, wd_ref):
    j = pl.program_id(0)

    @pl.when(j < nact_ref[0])
    def _tile():
        for slot, (new_ref, wg_ref, wu_ref, wdn_ref) in enumerate(((newa_ref, wg0_ref, wu0_ref, wd0_ref),
                                                                   (newb_ref, wg1_ref, wu1_ref, wd1_ref))):
            @pl.when(new_ref[j] != 0)
            def _recast():
                wgu_ref[2 * slot] = wg_ref[0, 0].astype(bf16)
                wgu_ref[2 * slot + 1] = wu_ref[0, 0].astype(bf16)
                wd_ref[slot] = wdn_ref[0, 0].astype(bf16)

        xb = xs_ref[:, :D_MODEL].astype(bf16)
        acc = None
        for slot in range(2):
            gate = jnp.dot(xb, wgu_ref[2 * slot], preferred_element_type=f32)
            up = jnp.dot(xb, wgu_ref[2 * slot + 1], preferred_element_type=f32)
            hid = (gate * _sigmoid(gate)) * up * xs_ref[:, D_MODEL + slot:D_MODEL + slot + 1]
            part = jnp.dot(hid.astype(bf16), wd_ref[slot], preferred_element_type=f32)
            acc = part if acc is None else acc + part
        ys_ref[...] = acc

    @pl.when(pl.program_id(0) >= nact_ref[0])
    def _unused_tile():
        ys_ref[...] = jnp.zeros_like(ys_ref)


def _moe(xs, tiles, w_gate, w_up, w_down, layer):
    n_tiles = xs.shape[0] // MOE_TILE
    tile = lambda j, na, ea, eb, ca, cb: (jnp.minimum(j, na[0] - 1), 0)
    low = lambda j, na, ea, eb, ca, cb: (layer, ea[j], 0, 0)
    high = lambda j, na, ea, eb, ca, cb: (layer, eb[j], 0, 0)
    up_shape = (1, 1, D_MODEL, D_FF_EXPERT)
    down_shape = (1, 1, D_FF_EXPERT, D_MODEL)
    return pl.pallas_call(
        _moe_kernel,
        grid_spec=pltpu.PrefetchScalarGridSpec(
            num_scalar_prefetch=5,
            grid=(n_tiles,),
            in_specs=[
                pl.BlockSpec((MOE_TILE, X1E_W), tile),
                pl.BlockSpec(up_shape, low), pl.BlockSpec(up_shape, low), pl.BlockSpec(down_shape, low),
                pl.BlockSpec(up_shape, high), pl.BlockSpec(up_shape, high), pl.BlockSpec(down_shape, high),
            ],
            out_specs=pl.BlockSpec((MOE_TILE, D_MODEL), lambda j, na, ea, eb, ca, cb: (j, 0)),
            scratch_shapes=[pltpu.VMEM((4, D_MODEL, D_FF_EXPERT), bf16), pltpu.VMEM((2, D_FF_EXPERT, D_MODEL), bf16)],
        ),
        out_shape=jax.ShapeDtypeStruct((n_tiles * MOE_TILE, D_MODEL), f32),
        compiler_params=pltpu.CompilerParams(
            dimension_semantics=("arbitrary",), vmem_limit_bytes=VMEM_LIMIT),
        name="moe_ffn",
    )(*tiles, xs, w_gate, w_up, w_down, w_gate, w_up, w_down)


def _unpermute_kernel(dest_ref, x1_ref, ys_hbm, lg_ref, lb_ref, out_ref, outb_ref, ybuf, sem):
    i = pl.program_id(0)
    n = pl.num_programs(0)
    tu = x1_ref.shape[0]

    def gather(tile, slot):
        base = tile * tu

        def issue(g, carry):
            r8 = pl.multiple_of(g * SUBLANES, SUBLANES)
            for j in range(SUBLANES):
                d = dest_ref[base + r8 + j]
                pltpu.make_async_copy(ys_hbm.at[pl.ds(d, 1)], ybuf.at[slot, pl.ds(r8 + j, 1)],
                                      sem.at[slot]).start()
            return carry

        lax.fori_loop(0, tu // SUBLANES, issue, 0)

    slot = lax.rem(i, 2)

    @pl.when(i == 0)
    def _first():
        gather(0, 0)

    @pl.when(i + 1 < n)
    def _next():
        gather(i + 1, 1 - slot)

    pltpu.make_async_copy(ys_hbm.at[pl.ds(0, tu)], ybuf.at[slot], sem.at[slot]).wait()
    y = _layer_norm(DEEPNORM_ALPHA * x1_ref[...] + ybuf[slot], lg_ref[...], lb_ref[...])
    out_ref[...] = y
    outb_ref[...] = y.astype(bf16)


def _unpermute(x1e, ys, dest, ln_g, ln_b, tu=256):
    s = x1e.shape[0]
    row = lambda i, d: (i, 0)
    fixed = lambda i, d: (0, 0)
    return pl.pallas_call(
        _unpermute_kernel,
        grid_spec=pltpu.PrefetchScalarGridSpec(
            num_scalar_prefetch=1,
            grid=(s // tu,),
            in_specs=[
                pl.BlockSpec((tu, D_MODEL), row),
                pl.BlockSpec(memory_space=pl.ANY),
                pl.BlockSpec((1, D_MODEL), fixed),
                pl.BlockSpec((1, D_MODEL), fixed),
            ],
            out_specs=[pl.BlockSpec((tu, D_MODEL), row), pl.BlockSpec((tu, D_MODEL), row)],
            scratch_shapes=[pltpu.VMEM((2, tu, D_MODEL), f32), pltpu.SemaphoreType.DMA((2,))],
        ),
        out_shape=[jax.ShapeDtypeStruct((s, D_MODEL), f32), jax.ShapeDtypeStruct((s, D_MODEL), bf16)],
        compiler_params=pltpu.CompilerParams(
            dimension_semantics=("arbitrary",), vmem_limit_bytes=VMEM_LIMIT),
        name="moe_unpermute_ln2",
    )(dest, x1e, ys, ln_g, ln_b)


def _pad_lanes(a, width=LANES):
    return jnp.pad(a, ((0, 0), (0, width - a.shape[1])))


def kernel(x, w_in, conv_w, dn_a_log, dn_dt_bias, dn_norm_w, p_a, p_b, w_out, ln1_g, ln1_b, w_router, router_bias, w_gate, w_up, w_down, ln2_g, ln2_b):
    bsz, s, _ = x.shape
    assert bsz == 1 and s % 512 == 0
    xf = x[0]
    xb = xf

    c_ba = 4 * DN_WIDTH
    c_sb = c_ba + 2 * DN_HEADS
    c_gate = c_sb + SB_SLAB
    wr_t = w_router.T
    wr_hi = wr_t.astype(bf16)
    wr_cat = jnp.concatenate([wr_hi, (wr_t - wr_hi.astype(f32)).astype(bf16)], axis=0)
    rb_col = router_bias[:, None]
    head_pad = ((0, 0), (DN_HEADS, LANES - 2 * DN_HEADS))
    w_in_t = jnp.swapaxes(w_in, 1, 2)

    for l in range(DEPTH):
        w = w_in_t[l]
        w_dn = jnp.concatenate([w[:c_ba], jnp.pad(w[c_ba:c_sb], ((0, LANES - 2 * DN_HEADS), (0, 0)))],
                               axis=0).astype(bf16)
        w_sb = jnp.concatenate([w[c_sb:c_sb + SB_WIDTH] * (SB_DH ** -0.5),
                                w[c_sb + SB_WIDTH:c_gate]], axis=0).astype(bf16)
        w_g = w[c_gate:].astype(bf16)

        p_dn, p_sb, gates = _proj(xb, w_dn, w_sb, w_g)
        o_a = _deltanet(p_dn,
                        jnp.pad(conv_w[l], ((0, 8 - CONV_K), (0, 0))),
                        jnp.pad(dn_a_log[l][None, :], head_pad),
                        jnp.pad(dn_dt_bias[l][None, :], head_pad),
                        dn_norm_w[l][None, :])
        o_b = _sb_attention(p_sb)
        x1e, route, cnt = _merge(xf, o_a, o_b, gates,
                                p_a[l].astype(bf16), p_b[l].astype(bf16), w_out[l].astype(bf16),
                                ln1_g[l][None, :], ln1_b[l][None, :], wr_cat, rb_col)
        dest, tiles, pad_start, pad_valid = _route_tables(route, cnt, s)
        xs = _permute(x1e, dest, pad_start, pad_valid)
        ys = _moe(xs, tiles, w_gate, w_up, w_down, l)
        xf, xb = _unpermute(x1e, ys, dest, ln2_g[l][None, :], ln2_b[l][None, :])
    return xf[None]
```

```python
import jax
import jax.numpy as jnp
from jax import lax
from jax.experimental import pallas as pl
from jax.experimental.pallas import tpu as pltpu

f32 = jnp.float32
bf16 = jnp.bfloat16

D_MODEL = 1024
DEPTH = 2
CHUNK = 64
DN_HEADS = 4
DN_DK = 128
DN_DV = 128
CONV_K = 4
SB_HEADS = 8
SB_DH = 64
SB_BLOCK = 128
N_EXPERTS = 16
N_GROUPS = 4
EXPERTS_PER_GROUP = N_EXPERTS // N_GROUPS
D_FF_EXPERT = 512
LN_EPS = 1e-5
RMS_EPS = 1e-6
DEEPNORM_ALPHA = (2 * DEPTH) ** 0.25

DN_WIDTH = DN_HEADS * DN_DK
SB_WIDTH = SB_HEADS * SB_DH
CONV_CH = 3 * DN_WIDTH
LANES = 128
SUBLANES = 8
DN_SLAB = 4 * DN_WIDTH + LANES
BA_COL = 4 * DN_WIDTH
SB_SLAB = 3 * SB_WIDTH
GATE_SLAB = 2 * D_MODEL

SB_LOG_ZERO = -88.0
SB_STATIC_BLOCKS = 2
SB_MASK_PENALTY = -1e30

VMEM_LIMIT = 48 * 1024 * 1024


def _sigmoid(x):
    return 1.0 / (1.0 + jnp.exp(-x))


def _softplus(x):
    return jnp.maximum(x, 0.0) + jnp.log(1.0 + jnp.exp(-jnp.abs(x)))


def _nt_dot(a, b):
    return lax.dot_general(a, b, (((1,), (1,)), ((), ())), preferred_element_type=f32)


def _tn_dot(a, b):
    return lax.dot_general(a, b, (((0,), (0,)), ((), ())), preferred_element_type=f32)


def _layer_norm(y, g, b):
    mu = jnp.mean(y, axis=-1, keepdims=True)
    d = y - mu
    var = jnp.mean(d * d, axis=-1, keepdims=True)
    return d * lax.rsqrt(var + LN_EPS) * g + b


def _proj_kernel(x_ref, wdn_ref, wsb_ref, wg_ref, odn_ref, osb_ref, og_ref):
    x = x_ref[...].astype(bf16)
    odn_ref[...] = _nt_dot(x, wdn_ref[...])
    osb_ref[...] = _nt_dot(x, wsb_ref[...]).astype(bf16)
    og_ref[...] = _nt_dot(x, wg_ref[...]).astype(bf16)


def _proj(x, w_dn, w_sb, w_g, tm=512):
    s = x.shape[0]
    row = lambda i: (i, 0)
    fixed = lambda i: (0, 0)
    return pl.pallas_call(
        _proj_kernel,
        grid=(s // tm,),
        in_specs=[
            pl.BlockSpec((tm, D_MODEL), row),
            pl.BlockSpec((DN_SLAB, D_MODEL), fixed),
            pl.BlockSpec((SB_SLAB, D_MODEL), fixed),
            pl.BlockSpec((GATE_SLAB, D_MODEL), fixed),
        ],
        out_specs=[
            pl.BlockSpec((tm, DN_SLAB), row),
            pl.BlockSpec((tm, SB_SLAB), row),
            pl.BlockSpec((tm, GATE_SLAB), row),
        ],
        out_shape=[
            jax.ShapeDtypeStruct((s, DN_SLAB), f32),
            jax.ShapeDtypeStruct((s, SB_SLAB), bf16),
            jax.ShapeDtypeStruct((s, GATE_SLAB), bf16),
        ],
        compiler_params=pltpu.CompilerParams(
            dimension_semantics=("arbitrary",), vmem_limit_bytes=VMEM_LIMIT),
        name="proj",
    )(x, w_dn, w_sb, w_g)


DN_ROWS = 1024
PAIR = 2 * CHUNK
TAIL = 8
DN_GROUP = 2
DN_SEQ_START = 22
DN_SEQ_EVERY = 5


def _split2(x):
    hi = x.astype(bf16)
    return hi, (x - hi.astype(f32)).astype(bf16)


def _split3(x):
    hi = x.astype(bf16)
    r = x - hi.astype(f32)
    mid = r.astype(bf16)
    return hi, mid, (r - mid.astype(f32)).astype(bf16)


def _dn_kernel(p_ref, cw_ref, alog_ref, dtb_ref, nw_ref, o_ref, state_ref, tail_ref, xe_ref):
    step = pl.program_id(0)
    rows = p_ref.shape[0]
    n_pb = rows // PAIR

    @pl.when(step == 0)
    def _init():
        state_ref[...] = jnp.zeros_like(state_ref)
        tail_ref[...] = jnp.zeros_like(tail_ref)

    xe_ref[0:TAIL, :] = tail_ref[...]
    xe_ref[TAIL:TAIL + rows, :] = p_ref[:, 0:CONV_CH]
    tail_ref[...] = p_ref[rows - TAIL:rows, 0:CONV_CH]

    row128 = lax.broadcasted_iota(jnp.int32, (PAIR, LANES), 0)
    lane128 = lax.broadcasted_iota(jnp.int32, (PAIR, LANES), 1)
    same_chunk = (row128 >= CHUNK) == (lane128 >= CHUNK)
    tril_bd = jnp.where(jnp.logical_and(row128 >= lane128, same_chunk), 1.0, 0.0).astype(bf16)
    triu_bd = jnp.where(jnp.logical_and(row128 <= lane128, same_chunk), 1.0, 0.0).astype(bf16)
    first_rows = row128 < CHUNK
    row64 = lax.broadcasted_iota(jnp.int32, (CHUNK, LANES), 0)
    lane64 = lax.broadcasted_iota(jnp.int32, (CHUNK, LANES), 1)
    left = lane64 < CHUNK
    col_in_chunk = jnp.bitwise_and(lane64, CHUNK - 1)
    tri_p = row64 >= col_in_chunk
    strict_p = row64 > col_in_chunk
    eye_p = jnp.where(row64 == col_in_chunk, 1.0, 0.0).astype(f32)

    def block_diag(z):
        zero = jnp.zeros_like(z)
        return jnp.concatenate([jnp.where(left, z, zero), jnp.where(left, zero, z)], axis=0)

    def pair_matmul(y_hi, y_lo, zbd_hi, zbd_lo):
        return (jnp.dot(jnp.concatenate([y_hi, y_lo], axis=1), jnp.concatenate([zbd_hi, zbd_hi], axis=0),
                        preferred_element_type=f32)
                + jnp.dot(y_hi, zbd_lo, preferred_element_type=f32))

    nw = nw_ref[...]
    heads = range(DN_HEADS)
    base = TAIL - (CONV_K - 1)

    def pair_block(pb):
        r0 = pb * PAIR
        ba = p_ref[r0:r0 + PAIR, BA_COL:BA_COL + LANES]
        beta_all = _sigmoid(ba)
        parts = _split3(-jnp.exp(alog_ref[...]) * _softplus(ba + dtb_ref[...]))
        gcol = sum(jnp.dot(tril_bd, pt, preferred_element_type=f32) for pt in parts)
        grow = sum(_tn_dot(pt, triu_bd) for pt in parts)
        yield

        qkv = []
        for grp in range(3):
            outs = []
            for h in heads:
                col = grp * DN_WIDTH + h * DN_DK
                acc = xe_ref[base + r0:base + r0 + PAIR, col:col + DN_DK] * cw_ref[0:1, col:col + DN_DK]
                for j in range(1, CONV_K):
                    acc = acc + (xe_ref[base + r0 + j:base + r0 + j + PAIR, col:col + DN_DK]
                                 * cw_ref[j:j + 1, col:col + DN_DK])
                y = acc * _sigmoid(acc)
                if grp < 2:
                    y = y * lax.rsqrt(jnp.sum(y * y, axis=-1, keepdims=True) + RMS_EPS)
                outs.append(y * (DN_DK ** -0.5) if grp == 0 else y)
                yield
            qkv.append(outs)
        qs, ks, vs = qkv

        kbetas, qgs, kdecs, rstacks, decays, egl = [], [], [], [], [], []
        for h in heads:
            gc = jnp.broadcast_to(gcol[:, DN_HEADS + h:DN_HEADS + h + 1], (PAIR, LANES))
            beta = jnp.broadcast_to(beta_all[:, h:h + 1], (PAIR, LANES))
            eg = jnp.exp(gc)
            glast = jnp.where(first_rows, gc[CHUNK - 1:CHUNK, :], gc[PAIR - 1:PAIR, :])
            kbeta = ks[h] * beta
            kbetas.append(kbeta)
            qgs.append(qs[h] * eg)
            kdecs.append((ks[h] * jnp.exp(glast - gc)).astype(bf16))
            rstacks.append(jnp.concatenate([kbeta * eg, vs[h] * beta], axis=1).astype(bf16))
            gdiff = jnp.where(left, gc[:CHUNK], gc[CHUNK:]) - grow[DN_HEADS + h:DN_HEADS + h + 1, :]
            decays.append(jnp.where(tri_p, jnp.exp(jnp.where(tri_p, gdiff, 0.0)), 0.0))
            egl.append((jnp.exp(gc[CHUNK - 1:CHUNK, :]), jnp.exp(gc[PAIR - 1:PAIR, :])))
            yield

        lps, qkms = [], []
        for h in heads:
            kk = _nt_dot(jnp.concatenate([kbetas[h], qs[h]], axis=0).astype(bf16), ks[h].astype(bf16))
            lps.append(jnp.where(strict_p, jnp.where(left, kk[0:CHUNK], kk[CHUNK:PAIR]) * decays[h], 0.0))
            qkms.append(jnp.where(tri_p, jnp.where(left, kk[PAIR:PAIR + CHUNK], kk[PAIR + CHUNK:]) * decays[h], 0.0)
                        .astype(bf16))
            yield

        pw = [_split2(lp) for lp in lps]
        pw_bd = [(block_diag(hi), block_diag(lo)) for hi, lo in pw]
        ts = [eye_p - lp for lp in lps]
        for _ in range(5):
            pw = [_split2(pair_matmul(*pw[h], *pw_bd[h])) for h in heads]
            pw_bd = [(block_diag(hi), block_diag(lo)) for hi, lo in pw]
            yield
            ts = [ts[h] + pair_matmul(*_split2(ts[h]), *pw_bd[h]) for h in heads]
            yield

        zero_p = jnp.zeros((CHUNK, LANES), bf16)
        mq, bo = [], []
        for h in heads:
            t16 = ts[h].astype(bf16)
            halves = (jnp.where(left, t16, zero_p), jnp.where(left, zero_p, t16))
            qk_halves = (jnp.where(left, qkms[h], zero_p), jnp.where(left, zero_p, qkms[h]))
            wus = [jnp.dot(th, rstacks[h], preferred_element_type=f32).astype(bf16) for th in halves]
            wu_stack = jnp.concatenate(wus, axis=0)
            mq_h, bo_h = [], []
            for c in range(2):
                cs = slice(c * CHUNK, (c + 1) * CHUNK)
                kw = _tn_dot(kdecs[h][cs], wus[c])
                qw = jnp.dot(qk_halves[c], wu_stack, preferred_element_type=f32)
                mq_h.append(jnp.concatenate([kw[:, :DN_DK], qgs[h][cs] - qw[:, :DN_DK]], axis=0).astype(bf16))
                bo_h.append((kw[:, DN_DK:], qw[:, DN_DK:]))
            mq.append(mq_h)
            bo.append(bo_h)
            yield

        prepared[pb] = (mq, bo, egl)

    states = [state_ref[h] for h in heads]

    def sequential(pbs):
        for pb in pbs:
            mq, bo, egl = prepared[pb]
            for c in range(2):
                for h in heads:
                    ms = jnp.dot(mq[h][c], states[h].astype(bf16), preferred_element_type=f32)
                    b_c, o_c = bo[h][c]
                    o = ms[DN_DK:] + o_c
                    states[h] = states[h] * egl[h][c] - ms[:DN_DK] + b_c
                    o = o * lax.rsqrt(jnp.mean(o * o, axis=-1, keepdims=True) + RMS_EPS) * nw
                    rc = pb * PAIR + c * CHUNK
                    z = p_ref[rc:rc + CHUNK, 3 * DN_WIDTH + h * DN_DV:3 * DN_WIDTH + (h + 1) * DN_DV]
                    o_ref[rc:rc + CHUNK, h * DN_DV:(h + 1) * DN_DV] = o * (z * _sigmoid(z))
                yield

    prepared = [None] * n_pb
    pending = iter(())
    for g0 in range(0, n_pb, DN_GROUP):
        group = range(g0, min(g0 + DN_GROUP, n_pb))
        pipes = [pair_block(pb) for pb in group]
        slot = 0
        while pipes:
            pipes = [pipe for pipe in pipes if next(pipe, "done") != "done"]
            slot += 1
            if slot >= DN_SEQ_START and (slot - DN_SEQ_START) % DN_SEQ_EVERY == 0:
                next(pending, None)
        for _ in pending:
            pass
        pending = sequential(group)
    for _ in pending:
        pass
    for h in heads:
        state_ref[h] = states[h]


def _deltanet(p_dn, conv_w8, alog_row, dtb_row, nw_row):
    s = p_dn.shape[0]
    fixed = lambda i: (0, 0)
    return pl.pallas_call(
        _dn_kernel,
        grid=(s // DN_ROWS,),
        in_specs=[
            pl.BlockSpec((DN_ROWS, DN_SLAB), lambda i: (i, 0)),
            pl.BlockSpec((8, CONV_CH), fixed),
            pl.BlockSpec((1, LANES), fixed),
            pl.BlockSpec((1, LANES), fixed),
            pl.BlockSpec((1, DN_DV), fixed),
        ],
        out_specs=pl.BlockSpec((DN_ROWS, DN_HEADS * DN_DV), lambda i: (i, 0)),
        out_shape=jax.ShapeDtypeStruct((s, DN_HEADS * DN_DV), f32),
        scratch_shapes=[
            pltpu.VMEM((DN_HEADS, DN_DK, DN_DV), f32),
            pltpu.VMEM((TAIL, CONV_CH), f32),
            pltpu.VMEM((TAIL + DN_ROWS, CONV_CH), f32),
        ],
        compiler_params=pltpu.CompilerParams(
            dimension_semantics=("arbitrary",), vmem_limit_bytes=VMEM_LIMIT),
        name="deltanet",
    )(p_dn, conv_w8, alog_row, dtb_row, nw_row)


N_PAIRS = SB_HEADS // 2
SB_QBLOCKS = 2


def _sb_kernel(q_ref, kc_ref, kp_ref, kpp_ref, vc_ref, vp_ref, vpp_ref, kv_hbm, o_ref, kbuf, vbuf, sem):
    cur = [pl.ds(sub * SB_BLOCK, SB_BLOCK) for sub in range(SB_QBLOCKS)]
    step = pl.program_id(0)
    _sb_query_block(step * SB_QBLOCKS, q_ref.at[cur[0]], (kc_ref.at[cur[0]], kp_ref, kpp_ref),
                    (vc_ref.at[cur[0]], vp_ref, vpp_ref), kv_hbm, o_ref.at[cur[0]], kbuf, vbuf, sem)
    _sb_query_block(step * SB_QBLOCKS + 1, q_ref.at[cur[1]], (kc_ref.at[cur[1]], kc_ref.at[cur[0]], kp_ref),
                    (vc_ref.at[cur[1]], vc_ref.at[cur[0]], vp_ref), kv_hbm, o_ref.at[cur[1]], kbuf, vbuf, sem)


def _sb_query_block(qb, q_ref, k_blocks, v_blocks, kv_hbm, o_ref, kbuf, vbuf, sem):
    blk = SB_BLOCK
    row = lax.broadcasted_iota(jnp.int32, (blk, blk), 0)
    lane = lax.broadcasted_iota(jnp.int32, (blk, blk), 1)
    diag_mask = row > lane
    even = lane < SB_DH
    suffix = jnp.where(row > lane, 1.0, 0.0).astype(bf16)
    suffix2 = jnp.concatenate([suffix, suffix], axis=0)

    def split_heads(x):
        zero = jnp.zeros_like(x)
        return jnp.where(even, x, zero), jnp.where(even, zero, x)

    def suffix_sums(spm):
        hi = spm.astype(bf16)
        lo = (spm - hi.astype(f32)).astype(bf16)
        return jnp.dot(jnp.concatenate([hi, lo], axis=1), suffix2, preferred_element_type=f32)

    pairs = [slice(p * LANES, (p + 1) * LANES) for p in range(N_PAIRS)]
    k_refs, v_refs = k_blocks[:SB_STATIC_BLOCKS], v_blocks[:SB_STATIC_BLOCKS]

    def fetch(kb):
        start = pl.multiple_of(kb * blk, blk)
        return (pltpu.make_async_copy(kv_hbm.at[pl.ds(start, blk), pl.ds(SB_WIDTH, SB_WIDTH)], kbuf, sem.at[0]),
                pltpu.make_async_copy(kv_hbm.at[pl.ds(start, blk), pl.ds(2 * SB_WIDTH, SB_WIDTH)], vbuf,
                                      sem.at[1]))

    def live(cr):
        m = cr[0]
        for c in cr[1:]:
            m = jnp.minimum(m, c)
        return jnp.min(m) <= -SB_LOG_ZERO

    pen1 = jnp.where(qb >= 1, 0.0, -SB_MASK_PENALTY).astype(f32)
    q_heads = [split_heads(q_ref[:, ps]) for ps in pairs]
    units = [(p, b, hh) for b in range(SB_STATIC_BLOCKS) for p in range(N_PAIRS) for hh in range(2)]
    z = {u: _nt_dot(q_heads[u[0]][u[2]], k_refs[u[1]][:, pairs[u[0]]]) for u in units}
    sp = {u: _softplus(z[u]) for u in units}
    spm = {u: (jnp.where(diag_mask, sp[u], 0.0) if u[1] == 0 else sp[u]) for u in units}
    logw = {u: z[u] - sp[u] - suffix_sums(spm[u]) for u in units}
    keep = {u: jnp.sum(spm[u], axis=1, keepdims=True) for u in units}
    carries = [keep[p, 0, hh] + keep[p, 1, hh] for p in range(N_PAIRS) for hh in range(2)]
    att = {}
    for p, b, hh in units:
        if b == 0:
            att[p, b, hh] = jnp.where(diag_mask, jnp.exp(logw[p, b, hh]), 0.0).astype(bf16)
        else:
            att[p, b, hh] = jnp.exp(logw[p, b, hh] - (keep[p, 0, hh] + pen1)).astype(bf16)
    accs = []
    for p in range(N_PAIRS):
        vals = jnp.concatenate([h for b in range(SB_STATIC_BLOCKS)
                                for h in split_heads(v_refs[b][:, pairs[p]])], axis=0)
        lhs = jnp.concatenate([att[p, b, hh] for b in range(SB_STATIC_BLOCKS) for hh in range(2)], axis=1)
        accs.append(jnp.dot(lhs, vals, preferred_element_type=f32))

    def cond(st):
        kb, go = st[0], st[1]
        return jnp.logical_and(kb >= 0, go)

    def body(st):
        kb = st[0]
        acc_l = list(st[2:2 + N_PAIRS])
        car_l = list(st[2 + N_PAIRS:])
        resident = kb == qb - SB_STATIC_BLOCKS

        @pl.when(resident)
        def _copy():
            kbuf[...] = k_blocks[SB_STATIC_BLOCKS][...]
            vbuf[...] = v_blocks[SB_STATIC_BLOCKS][...]

        @pl.when(jnp.logical_not(resident))
        def _fetch():
            for cp in fetch(kb):
                cp.start()
            for cp in fetch(kb):
                cp.wait()

        new_car = []
        for p, ps in enumerate(pairs):
            atts = []
            for hh, qh in enumerate(split_heads(q_ref[:, ps])):
                zz = _nt_dot(qh, kbuf[:, ps])
                spp = _softplus(zz)
                carry = car_l[2 * p + hh]
                atts.append(jnp.exp(zz - spp - suffix_sums(spp) - carry).astype(bf16))
                new_car.append(carry + jnp.sum(spp, axis=1, keepdims=True))
            vals = jnp.concatenate(split_heads(vbuf[:, ps]), axis=0)
            acc_l[p] = acc_l[p] + jnp.dot(jnp.concatenate(atts, axis=1), vals, preferred_element_type=f32)
        return (kb - 1, live(new_car), *acc_l, *new_car)

    st = lax.while_loop(cond, body, (qb - SB_STATIC_BLOCKS, live(carries), *accs, *carries))
    for p, ps in enumerate(pairs):
        o_ref[:, ps] = st[2 + p]


def _sb_attention(p_sb):
    s = p_sb.shape[0]
    blk = SB_BLOCK

    step_rows = SB_QBLOCKS * blk

    def current(col):
        return pl.BlockSpec((step_rows, SB_WIDTH), lambda i: (i, col))

    def previous(col, back):
        return pl.BlockSpec((blk, SB_WIDTH), lambda i: (jnp.maximum(i * SB_QBLOCKS - back, 0), col))

    return pl.pallas_call(
        _sb_kernel,
        grid=(s // step_rows,),
        in_specs=[current(0), current(1), previous(1, 1), previous(1, 2), current(2), previous(2, 1), previous(2, 2),
                  pl.BlockSpec(memory_space=pl.ANY)],
        out_specs=pl.BlockSpec((step_rows, SB_WIDTH), lambda i: (i, 0)),
        out_shape=jax.ShapeDtypeStruct((s, SB_WIDTH), f32),
        scratch_shapes=[
            pltpu.VMEM((blk, SB_WIDTH), bf16),
            pltpu.VMEM((blk, SB_WIDTH), bf16),
            pltpu.SemaphoreType.DMA((2,)),
        ],
        compiler_params=pltpu.CompilerParams(
            dimension_semantics=("arbitrary",), vmem_limit_bytes=VMEM_LIMIT),
        name="sb_attention",
    )(p_sb, p_sb, p_sb, p_sb, p_sb, p_sb, p_sb, p_sb)


MOE_TILE = 256
PAIRS_PER_GROUP = 6
N_CLASSES = N_GROUPS * PAIRS_PER_GROUP
X1E_W = D_MODEL + LANES
MERGE_ROWS = 512
MERGE_PARTS = 2


def _route(aff, sel):
    neg = -jnp.inf
    best = None
    for gidx in range(N_GROUPS):
        rows = slice(gidx * EXPERTS_PER_GROUP, (gidx + 1) * EXPERTS_PER_GROUP)
        sg = sel[rows]
        idx = lax.broadcasted_iota(jnp.int32, sg.shape, 0) + gidx * EXPERTS_PER_GROUP
        m1 = jnp.max(sg, axis=0, keepdims=True)
        i1 = jnp.min(jnp.where(sg == m1, idx, N_EXPERTS), axis=0, keepdims=True)
        sg2 = jnp.where(idx == i1, neg, sg)
        m2 = jnp.max(sg2, axis=0, keepdims=True)
        i2 = jnp.min(jnp.where(sg2 == m2, idx, N_EXPERTS), axis=0, keepdims=True)
        score = m1 + m2
        if best is None:
            best = (score, i1, i2)
        else:
            better = score > best[0]
            best = (jnp.where(better, score, best[0]),
                    jnp.where(better, i1, best[1]),
                    jnp.where(better, i2, best[2]))
    _, i1, i2 = best
    expert = lax.broadcasted_iota(jnp.int32, aff.shape, 0)
    w1 = jnp.sum(jnp.where(expert == i1, aff, 0.0), axis=0, keepdims=True)
    w2 = jnp.sum(jnp.where(expert == i2, aff, 0.0), axis=0, keepdims=True)
    denom = w1 + w2
    w1, w2 = w1 / denom, w2 / denom
    first_low = i1 < i2
    e_lo = jnp.minimum(i1, i2)
    e_hi = jnp.maximum(i1, i2)
    a = jnp.bitwise_and(e_lo, EXPERTS_PER_GROUP - 1)
    b = jnp.bitwise_and(e_hi, EXPERTS_PER_GROUP - 1)
    pair = jnp.where(a == 0, 0, jnp.where(a == 1, 3, 5)) + (b - a - 1)
    order = jnp.where(pair == 3, 4, jnp.where(pair == 4, 3, pair))
    cls = lax.shift_right_logical(e_lo, 2) * PAIRS_PER_GROUP + order
    w_lo, w_hi = jnp.where(first_low, w1, w2), jnp.where(first_low, w2, w1)
    swapped = pair == PAIRS_PER_GROUP - 1
    return cls, jnp.where(swapped, w_hi, w_lo), jnp.where(swapped, w_lo, w_hi)


CLASS_ROWS = 32


def _merge_kernel(x_ref, oa_ref, ob_ref, g_ref, pa_ref, pb_ref, wo_ref, lg_ref, lb_ref, wr_ref, rb_ref,
                  x1e_ref, route_ref, cnt_ref, run_ref):
    step = pl.program_id(0)

    @pl.when(step == 0)
    def _init():
        run_ref[...] = jnp.zeros_like(run_ref)

    rows = x_ref.shape[0] // MERGE_PARTS
    parts = [slice(i * rows, (i + 1) * rows) for i in range(MERGE_PARTS)]
    a = [jnp.dot(oa_ref[p, :].astype(bf16), pa_ref[...], preferred_element_type=f32) for p in parts]
    b = [jnp.dot(ob_ref[p, :].astype(bf16), pb_ref[...], preferred_element_type=f32) for p in parts]
    merged = [(_sigmoid(g_ref[p, :D_MODEL].astype(f32)) * ai
               + _sigmoid(g_ref[p, D_MODEL:].astype(f32)) * bi).astype(bf16)
              for p, ai, bi in zip(parts, a, b)]
    mix = [jnp.dot(m, wo_ref[...], preferred_element_type=f32) for m in merged]
    x1 = [_layer_norm(DEEPNORM_ALPHA * x_ref[p, :] + mi, lg_ref[...], lb_ref[...]) for p, mi in zip(parts, mix)]
    x1b = [xi.astype(bf16) for xi in x1]
    for p, xi in zip(parts, x1):
        x1e_ref[p, :D_MODEL] = xi

    x_lo = [(xi - xbi.astype(f32)).astype(bf16) for xi, xbi in zip(x1, x1b)]
    t = [_nt_dot(wr_ref[...], xbi) for xbi in x1b]
    logits = [ti[:N_EXPERTS] + ti[N_EXPERTS:] + _nt_dot(wr_ref[:N_EXPERTS, :], xl) for ti, xl in zip(t, x_lo)]
    aff = [_sigmoid(lg) for lg in logits]
    routed = [_route(af, af + rb_ref[...]) for af in aff]

    rr = lax.broadcasted_iota(jnp.int32, (rows, rows), 0)
    cc = lax.broadcasted_iota(jnp.int32, (rows, rows), 1)
    triu = jnp.where(rr <= cc, 1.0, 0.0).astype(bf16)
    class_id = lax.broadcasted_iota(jnp.int32, (CLASS_ROWS, rows), 0)
    for p, (cls, w_lo, w_hi) in zip(parts, routed):
        onehot = class_id == cls
        prefix = jnp.dot(jnp.where(onehot, 1.0, 0.0).astype(bf16), triu, preferred_element_type=f32)
        run = run_ref[...]
        rank = (jnp.sum(jnp.where(onehot, prefix + run, 0.0), axis=0, keepdims=True) - 1.0).astype(jnp.int32)
        run_ref[...] = run + prefix[:, rows - 1:rows]
        pad = jnp.zeros((SUBLANES - 3, rows), jnp.int32)
        route_ref[0, :, p] = jnp.concatenate(
            [cls, lax.shift_right_logical(rank, 7), jnp.bitwise_and(rank, LANES - 1), pad], axis=0)
        w_rows = jnp.concatenate([w_lo, w_hi, jnp.zeros((LANES - 2, rows), f32)], axis=0)
        x1e_ref[p, D_MODEL:] = w_rows.T
    cnt_ref[...] = jnp.broadcast_to(run_ref[...], cnt_ref.shape).astype(jnp.int32)


def _merge(x, o_a, o_b, gates, p_a, p_b, w_out, ln_g, ln_b, wr_cat, r_bias):
    s = x.shape[0]
    tm = MERGE_ROWS
    row = lambda i: (i, 0)
    fixed = lambda i: (0, 0)
    return pl.pallas_call(
        _merge_kernel,
        grid=(s // tm,),
        in_specs=[
            pl.BlockSpec((tm, D_MODEL), row),
            pl.BlockSpec((tm, DN_HEADS * DN_DV), row),
            pl.BlockSpec((tm, SB_WIDTH), row),
            pl.BlockSpec((tm, GATE_SLAB), row),
            pl.BlockSpec((DN_HEADS * DN_DV, D_MODEL), fixed),
            pl.BlockSpec((SB_WIDTH, D_MODEL), fixed),
            pl.BlockSpec((D_MODEL, D_MODEL), fixed),
            pl.BlockSpec((1, D_MODEL), fixed),
            pl.BlockSpec((1, D_MODEL), fixed),
            pl.BlockSpec((2 * N_EXPERTS, D_MODEL), fixed),
            pl.BlockSpec((N_EXPERTS, 1), fixed),
        ],
        out_specs=[
            pl.BlockSpec((tm, X1E_W), row),
            pl.BlockSpec((1, SUBLANES, tm), lambda i: (i, 0, 0)),
            pl.BlockSpec((CLASS_ROWS, LANES), fixed),
        ],
        out_shape=[
            jax.ShapeDtypeStruct((s, X1E_W), f32),
            jax.ShapeDtypeStruct((s // tm, SUBLANES, tm), jnp.int32),
            jax.ShapeDtypeStruct((CLASS_ROWS, LANES), jnp.int32),
        ],
        scratch_shapes=[pltpu.VMEM((CLASS_ROWS, 1), f32)],
        compiler_params=pltpu.CompilerParams(
            dimension_semantics=("arbitrary",), vmem_limit_bytes=VMEM_LIMIT),
        name="merge_router",
    )(x, o_a, o_b, gates, p_a, p_b, w_out, ln_g, ln_b, wr_cat, r_bias)


SLOT_PAIRS = ((0, 1), (0, 2), (0, 3), (1, 3), (1, 2), (3, 2))


def _class_experts():
    slot_a = [g * EXPERTS_PER_GROUP + a for g in range(N_GROUPS) for a, _ in SLOT_PAIRS]
    slot_b = [g * EXPERTS_PER_GROUP + b for g in range(N_GROUPS) for _, b in SLOT_PAIRS]
    return jnp.array(slot_a, jnp.int32), jnp.array(slot_b, jnp.int32)


def _n_tiles(s):
    return -(-(s + N_CLASSES * (MOE_TILE - 1)) // MOE_TILE)


def _route_tables(route, cnt, s):
    counts = cnt[:N_CLASSES, 0]
    padded = (counts + (MOE_TILE - 1)) // MOE_TILE * MOE_TILE
    ends = jnp.cumsum(padded)
    offs = ends - padded
    cls, rank_hi, rank_lo = (route[:, r, :].reshape(s) for r in range(3))
    dest = offs[cls] + rank_hi * LANES + rank_lo
    n_active = (ends[-1] // MOE_TILE).astype(jnp.int32)[None]
    tile_row = jnp.minimum(jnp.arange(_n_tiles(s), dtype=jnp.int32) * MOE_TILE, ends[-1] - 1)
    tile_cls = jnp.minimum(jnp.sum(tile_row[:, None] >= ends[None, :], axis=1), N_CLASSES - 1)
    slot_a, slot_b = _class_experts()
    tile_a, tile_b = slot_a[tile_cls], slot_b[tile_cls]
    first = jnp.ones((1,), jnp.int32)
    new_a = jnp.concatenate([first, (tile_a[1:] != tile_a[:-1]).astype(jnp.int32)])
    new_b = jnp.concatenate([first, (tile_b[1:] != tile_b[:-1]).astype(jnp.int32)])
    tail = jnp.arange(s // MOE_TILE, _n_tiles(s), dtype=jnp.int32)
    pad_start = jnp.concatenate([ends - MOE_TILE, tail * MOE_TILE]).astype(jnp.int32)
    pad_valid = jnp.concatenate([padded > 0, tail >= n_active[0]]).astype(jnp.int32)
    return (dest.astype(jnp.int32), (n_active, tile_a, tile_b, new_a, new_b), pad_start, pad_valid)


def _permute_kernel(dest_ref, pstart_ref, pvalid_ref, x_ref, xs_hbm, zero_ref, sem):
    step = pl.program_id(0)
    tp = x_ref.shape[0]

    @pl.when(step == 0)
    def _fill():
        zero_ref[...] = jnp.zeros_like(zero_ref)

        def fill_copy(c):
            start = pl.multiple_of(pstart_ref[c], MOE_TILE)
            return pltpu.make_async_copy(zero_ref, xs_hbm.at[pl.ds(start, MOE_TILE)], sem.at[1])

        for c in range(pstart_ref.shape[0]):
            @pl.when(pvalid_ref[c] != 0)
            def _start():
                fill_copy(c).start()
        for c in range(pstart_ref.shape[0]):
            @pl.when(pvalid_ref[c] != 0)
            def _wait():
                fill_copy(c).wait()

    base = step * tp

    def issue(g, carry):
        r8 = pl.multiple_of(g * SUBLANES, SUBLANES)
        for j in range(SUBLANES):
            d = dest_ref[base + r8 + j]
            pltpu.make_async_copy(x_ref.at[pl.ds(r8 + j, 1)], xs_hbm.at[pl.ds(d, 1)], sem.at[0]).start()
        return carry

    lax.fori_loop(0, tp // SUBLANES, issue, 0)
    pltpu.make_async_copy(x_ref, xs_hbm.at[pl.ds(0, tp)], sem.at[0]).wait()


def _permute(x1e, dest, pad_start, pad_valid, tp=1024):
    s = x1e.shape[0]
    return pl.pallas_call(
        _permute_kernel,
        grid_spec=pltpu.PrefetchScalarGridSpec(
            num_scalar_prefetch=3,
            grid=(s // tp,),
            in_specs=[pl.BlockSpec((tp, X1E_W), lambda i, d, ps, pv: (i, 0))],
            out_specs=pl.BlockSpec(memory_space=pl.ANY),
            scratch_shapes=[pltpu.VMEM((MOE_TILE, X1E_W), f32), pltpu.SemaphoreType.DMA((2,))],
        ),
        out_shape=jax.ShapeDtypeStruct((_n_tiles(s) * MOE_TILE, X1E_W), f32),
        compiler_params=pltpu.CompilerParams(
            dimension_semantics=("arbitrary",), vmem_limit_bytes=VMEM_LIMIT),
        name="moe_permute",
    )(dest, pad_start, pad_valid, x1e)


def _moe_kernel(nact_ref, ea_ref, eb_ref, newa_ref, newb_ref, xs_ref,
                wg0_ref, wu0_ref, wd0_ref, wg1_ref, wu1_ref, wd1_ref, ys_ref, wgu_ref, wd_ref):
    j = pl.program_id(0)

    @pl.when(j < nact_ref[0])
    def _tile():
        for slot, (new_ref, wg_ref, wu_ref, wdn_ref) in enumerate(((newa_ref, wg0_ref, wu0_ref, wd0_ref),
                                                                   (newb_ref, wg1_ref, wu1_ref, wd1_ref))):
            @pl.when(new_ref[j] != 0)
            def _recast():
                wgu_ref[2 * slot] = wg_ref[0, 0].astype(bf16)
                wgu_ref[2 * slot + 1] = wu_ref[0, 0].astype(bf16)
                wd_ref[slot] = wdn_ref[0, 0].astype(bf16)

        xb = xs_ref[:, :D_MODEL].astype(bf16)
        acc = None
        for slot in range(2):
            gate = jnp.dot(xb, wgu_ref[2 * slot], preferred_element_type=f32)
            up = jnp.dot(xb, wgu_ref[2 * slot + 1], preferred_element_type=f32)
            hid = (gate * _sigmoid(gate)) * up * xs_ref[:, D_MODEL + slot:D_MODEL + slot + 1]
            part = jnp.dot(hid.astype(bf16), wd_ref[slot], preferred_element_type=f32)
            acc = part if acc is None else acc + part
        ys_ref[...] = acc

    @pl.when(pl.program_id(0) >= nact_ref[0])
    def _unused_tile():
        ys_ref[...] = jnp.zeros_like(ys_ref)


def _moe(xs, tiles, w_gate, w_up, w_down, layer):
    n_tiles = xs.shape[0] // MOE_TILE
    tile = lambda j, na, ea, eb, ca, cb: (jnp.minimum(j, na[0] - 1), 0)
    low = lambda j, na, ea, eb, ca, cb: (layer, ea[j], 0, 0)
    high = lambda j, na, ea, eb, ca, cb: (layer, eb[j], 0, 0)
    up_shape = (1, 1, D_MODEL, D_FF_EXPERT)
    down_shape = (1, 1, D_FF_EXPERT, D_MODEL)
    return pl.pallas_call(
        _moe_kernel,
        grid_spec=pltpu.PrefetchScalarGridSpec(
            num_scalar_prefetch=5,
            grid=(n_tiles,),
            in_specs=[
                pl.BlockSpec((MOE_TILE, X1E_W), tile),
                pl.BlockSpec(up_shape, low), pl.BlockSpec(up_shape, low), pl.BlockSpec(down_shape, low),
                pl.BlockSpec(up_shape, high), pl.BlockSpec(up_shape, high), pl.BlockSpec(down_shape, high),
            ],
            out_specs=pl.BlockSpec((MOE_TILE, D_MODEL), lambda j, na, ea, eb, ca, cb: (j, 0)),
            scratch_shapes=[pltpu.VMEM((4, D_MODEL, D_FF_EXPERT), bf16), pltpu.VMEM((2, D_FF_EXPERT, D_MODEL), bf16)],
        ),
        out_shape=jax.ShapeDtypeStruct((n_tiles * MOE_TILE, D_MODEL), f32),
        compiler_params=pltpu.CompilerParams(
            dimension_semantics=("arbitrary",), vmem_limit_bytes=VMEM_LIMIT),
        name="moe_ffn",
    )(*tiles, xs, w_gate, w_up, w_down, w_gate, w_up, w_down)


def _unpermute_kernel(dest_ref, x1_ref, ys_hbm, lg_ref, lb_ref, out_ref, outb_ref, ybuf, sem):
    i = pl.program_id(0)
    n = pl.num_programs(0)
    tu = x1_ref.shape[0]

    def gather(tile, slot):
        base = tile * tu

        def issue(g, carry):
            r8 = pl.multiple_of(g * SUBLANES, SUBLANES)
            for j in range(SUBLANES):
                d = dest_ref[base + r8 + j]
                pltpu.make_async_copy(ys_hbm.at[pl.ds(d, 1)], ybuf.at[slot, pl.ds(r8 + j, 1)],
                                      sem.at[slot]).start()
            return carry

        lax.fori_loop(0, tu // SUBLANES, issue, 0)

    slot = lax.rem(i, 2)

    @pl.when(i == 0)
    def _first():
        gather(0, 0)

    @pl.when(i + 1 < n)
    def _next():
        gather(i + 1, 1 - slot)

    pltpu.make_async_copy(ys_hbm.at[pl.ds(0, tu)], ybuf.at[slot], sem.at[slot]).wait()
    y = _layer_norm(DEEPNORM_ALPHA * x1_ref[...] + ybuf[slot], lg_ref[...], lb_ref[...])
    out_ref[...] = y
    outb_ref[...] = y.astype(bf16)


def _unpermute(x1e, ys, dest, ln_g, ln_b, tu=512):
    s = x1e.shape[0]
    row = lambda i, d: (i, 0)
    fixed = lambda i, d: (0, 0)
    return pl.pallas_call(
        _unpermute_kernel,
        grid_spec=pltpu.PrefetchScalarGridSpec(
            num_scalar_prefetch=1,
            grid=(s // tu,),
            in_specs=[
                pl.BlockSpec((tu, D_MODEL), row),
                pl.BlockSpec(memory_space=pl.ANY),
                pl.BlockSpec((1, D_MODEL), fixed),
                pl.BlockSpec((1, D_MODEL), fixed),
            ],
            out_specs=[pl.BlockSpec((tu, D_MODEL), row), pl.BlockSpec((tu, D_MODEL), row)],
            scratch_shapes=[pltpu.VMEM((2, tu, D_MODEL), f32), pltpu.SemaphoreType.DMA((2,))],
        ),
        out_shape=[jax.ShapeDtypeStruct((s, D_MODEL), f32), jax.ShapeDtypeStruct((s, D_MODEL), bf16)],
        compiler_params=pltpu.CompilerParams(
            dimension_semantics=("arbitrary",), vmem_limit_bytes=VMEM_LIMIT),
        name="moe_unpermute_ln2",
    )(dest, x1e, ys, ln_g, ln_b)


def kernel(x, w_in, conv_w, dn_a_log, dn_dt_bias, dn_norm_w, p_a, p_b, w_out, ln1_g, ln1_b, w_router, router_bias, w_gate, w_up, w_down, ln2_g, ln2_b):
    bsz, s, _ = x.shape
    assert bsz == 1 and s % DN_ROWS == 0 and SB_QBLOCKS == 2
    xf = x[0]
    xb = xf

    c_ba = 4 * DN_WIDTH
    c_sb = c_ba + 2 * DN_HEADS
    c_gate = c_sb + SB_SLAB
    wr_t = w_router.T
    wr_hi = wr_t.astype(bf16)
    wr_cat = jnp.concatenate([wr_hi, (wr_t - wr_hi.astype(f32)).astype(bf16)], axis=0)
    rb_col = router_bias[:, None]
    head_pad = ((0, 0), (DN_HEADS, LANES - 2 * DN_HEADS))
    w_in_t = jnp.swapaxes(w_in, 1, 2)

    for l in range(DEPTH):
        w = w_in_t[l]
        w_dn = jnp.concatenate([w[:c_ba], jnp.pad(w[c_ba:c_sb], ((0, LANES - 2 * DN_HEADS), (0, 0)))],
                               axis=0).astype(bf16)
        w_sb = jnp.concatenate([w[c_sb:c_sb + SB_WIDTH] * (SB_DH ** -0.5),
                                w[c_sb + SB_WIDTH:c_gate]], axis=0).astype(bf16)
        w_g = w[c_gate:].astype(bf16)

        p_dn, p_sb, gates = _proj(xb, w_dn, w_sb, w_g)
        o_a = _deltanet(p_dn,
                        jnp.pad(conv_w[l], ((0, 8 - CONV_K), (0, 0))),
                        jnp.pad(dn_a_log[l][None, :], head_pad),
                        jnp.pad(dn_dt_bias[l][None, :], head_pad),
                        dn_norm_w[l][None, :])
        o_b = _sb_attention(p_sb)
        x1e, route, cnt = _merge(xf, o_a, o_b, gates,
                                p_a[l].astype(bf16), p_b[l].astype(bf16), w_out[l].astype(bf16),
                                ln1_g[l][None, :], ln1_b[l][None, :], wr_cat, rb_col)
        dest, tiles, pad_start, pad_valid = _route_tables(route, cnt, s)
        xs = _permute(x1e, dest, pad_start, pad_valid)
        ys = _moe(xs, tiles, w_gate, w_up, w_down, l)
        xf, xb = _unpermute(x1e, ys, dest, ln2_g[l][None, :], ln2_b[l][None, :])
    return xf[None]
```

```python
import jax
import jax.numpy as jnp
from jax import lax
from jax.experimental import pallas as pl
from jax.experimental.pallas import tpu as pltpu

f32 = jnp.float32
bf16 = jnp.bfloat16

D_MODEL = 1024
DEPTH = 2
CHUNK = 64
DN_HEADS = 4
DN_DK = 128
DN_DV = 128
CONV_K = 4
SB_HEADS = 8
SB_DH = 64
SB_BLOCK = 128
N_EXPERTS = 16
N_GROUPS = 4
EXPERTS_PER_GROUP = N_EXPERTS // N_GROUPS
D_FF_EXPERT = 512
LN_EPS = 1e-5
RMS_EPS = 1e-6
DEEPNORM_ALPHA = (2 * DEPTH) ** 0.25

DN_WIDTH = DN_HEADS * DN_DK
SB_WIDTH = SB_HEADS * SB_DH
CONV_CH = 3 * DN_WIDTH
LANES = 128
SUBLANES = 8
DN_SLAB = 4 * DN_WIDTH + LANES
BA_COL = 4 * DN_WIDTH
SB_SLAB = 3 * SB_WIDTH
GATE_SLAB = 2 * D_MODEL

SB_LOG_ZERO = -88.0
SB_STATIC_BLOCKS = 2
SB_MASK_PENALTY = -1e30

VMEM_LIMIT = 48 * 1024 * 1024


def _sigmoid(x):
    return 1.0 / (1.0 + jnp.exp(-x))


def _softplus(x):
    return jnp.maximum(x, 0.0) + jnp.log(1.0 + jnp.exp(-jnp.abs(x)))


def _nt_dot(a, b):
    return lax.dot_general(a, b, (((1,), (1,)), ((), ())), preferred_element_type=f32)


def _tn_dot(a, b):
    return lax.dot_general(a, b, (((0,), (0,)), ((), ())), preferred_element_type=f32)


def _layer_norm(y, g, b):
    mu = jnp.mean(y, axis=-1, keepdims=True)
    d = y - mu
    var = jnp.mean(d * d, axis=-1, keepdims=True)
    return d * lax.rsqrt(var + LN_EPS) * g + b


def _proj_kernel(x_ref, wdn_ref, wsb_ref, wg_ref, odn_ref, osb_ref, og_ref):
    x = x_ref[...].astype(bf16)
    odn_ref[...] = _nt_dot(x, wdn_ref[...])
    osb_ref[...] = _nt_dot(x, wsb_ref[...]).astype(bf16)
    og_ref[...] = _nt_dot(x, wg_ref[...]).astype(bf16)


def _proj(x, w_dn, w_sb, w_g, tm=512):
    s = x.shape[0]
    row = lambda i: (i, 0)
    fixed = lambda i: (0, 0)
    return pl.pallas_call(
        _proj_kernel,
        grid=(s // tm,),
        in_specs=[
            pl.BlockSpec((tm, D_MODEL), row),
            pl.BlockSpec((DN_SLAB, D_MODEL), fixed),
            pl.BlockSpec((SB_SLAB, D_MODEL), fixed),
            pl.BlockSpec((GATE_SLAB, D_MODEL), fixed),
        ],
        out_specs=[
            pl.BlockSpec((tm, DN_SLAB), row),
            pl.BlockSpec((tm, SB_SLAB), row),
            pl.BlockSpec((tm, GATE_SLAB), row),
        ],
        out_shape=[
            jax.ShapeDtypeStruct((s, DN_SLAB), f32),
            jax.ShapeDtypeStruct((s, SB_SLAB), bf16),
            jax.ShapeDtypeStruct((s, GATE_SLAB), bf16),
        ],
        compiler_params=pltpu.CompilerParams(
            dimension_semantics=("arbitrary",), vmem_limit_bytes=VMEM_LIMIT),
        name="proj",
    )(x, w_dn, w_sb, w_g)


DN_ROWS = 1024
PAIR = 2 * CHUNK
TAIL = 8
DN_GROUP = 2
DN_SEQ_START = 22
DN_SEQ_EVERY = 5


def _split2(x):
    hi = x.astype(bf16)
    return hi, (x - hi.astype(f32)).astype(bf16)


def _split3(x):
    hi = x.astype(bf16)
    r = x - hi.astype(f32)
    mid = r.astype(bf16)
    return hi, mid, (r - mid.astype(f32)).astype(bf16)


def _dn_kernel(p_ref, cw_ref, alog_ref, dtb_ref, nw_ref, o_ref, state_ref, tail_ref, xe_ref):
    step = pl.program_id(0)
    rows = p_ref.shape[0]
    n_pb = rows // PAIR

    @pl.when(step == 0)
    def _init():
        state_ref[...] = jnp.zeros_like(state_ref)
        tail_ref[...] = jnp.zeros_like(tail_ref)

    xe_ref[0:TAIL, :] = tail_ref[...]
    xe_ref[TAIL:TAIL + rows, :] = p_ref[:, 0:CONV_CH]
    tail_ref[...] = p_ref[rows - TAIL:rows, 0:CONV_CH]

    row128 = lax.broadcasted_iota(jnp.int32, (PAIR, LANES), 0)
    lane128 = lax.broadcasted_iota(jnp.int32, (PAIR, LANES), 1)
    same_chunk = (row128 >= CHUNK) == (lane128 >= CHUNK)
    tril_bd = jnp.where(jnp.logical_and(row128 >= lane128, same_chunk), 1.0, 0.0).astype(bf16)
    triu_bd = jnp.where(jnp.logical_and(row128 <= lane128, same_chunk), 1.0, 0.0).astype(bf16)
    first_rows = row128 < CHUNK
    row64 = lax.broadcasted_iota(jnp.int32, (CHUNK, LANES), 0)
    lane64 = lax.broadcasted_iota(jnp.int32, (CHUNK, LANES), 1)
    left = lane64 < CHUNK
    col_in_chunk = jnp.bitwise_and(lane64, CHUNK - 1)
    tri_p = row64 >= col_in_chunk
    strict_p = row64 > col_in_chunk
    eye_p = jnp.where(row64 == col_in_chunk, 1.0, 0.0).astype(f32)

    def block_diag(z):
        zero = jnp.zeros_like(z)
        return jnp.concatenate([jnp.where(left, z, zero), jnp.where(left, zero, z)], axis=0)

    def pair_matmul(y_hi, y_lo, zbd_hi, zbd_lo):
        return (jnp.dot(jnp.concatenate([y_hi, y_lo], axis=1), jnp.concatenate([zbd_hi, zbd_hi], axis=0),
                        preferred_element_type=f32)
                + jnp.dot(y_hi, zbd_lo, preferred_element_type=f32))

    nw = nw_ref[...]
    heads = range(DN_HEADS)
    base = TAIL - (CONV_K - 1)

    def pair_block(pb):
        r0 = pb * PAIR
        ba = p_ref[r0:r0 + PAIR, BA_COL:BA_COL + LANES]
        beta_all = _sigmoid(ba)
        parts = _split3(-jnp.exp(alog_ref[...]) * _softplus(ba + dtb_ref[...]))
        gcol = sum(jnp.dot(tril_bd, pt, preferred_element_type=f32) for pt in parts)
        grow = sum(_tn_dot(pt, triu_bd) for pt in parts)
        yield

        qkv = []
        for grp in range(3):
            outs = []
            for h in heads:
                col = grp * DN_WIDTH + h * DN_DK
                acc = xe_ref[base + r0:base + r0 + PAIR, col:col + DN_DK] * cw_ref[0:1, col:col + DN_DK]
                for j in range(1, CONV_K):
                    acc = acc + (xe_ref[base + r0 + j:base + r0 + j + PAIR, col:col + DN_DK]
                                 * cw_ref[j:j + 1, col:col + DN_DK])
                y = acc * _sigmoid(acc)
                if grp < 2:
                    y = y * lax.rsqrt(jnp.sum(y * y, axis=-1, keepdims=True) + RMS_EPS)
                outs.append(y * (DN_DK ** -0.5) if grp == 0 else y)
                yield
            qkv.append(outs)
        qs, ks, vs = qkv

        kbetas, qgs, kdecs, rstacks, decays, egl = [], [], [], [], [], []
        for h in heads:
            gc = jnp.broadcast_to(gcol[:, DN_HEADS + h:DN_HEADS + h + 1], (PAIR, LANES))
            beta = jnp.broadcast_to(beta_all[:, h:h + 1], (PAIR, LANES))
            eg = jnp.exp(gc)
            glast = jnp.where(first_rows, gc[CHUNK - 1:CHUNK, :], gc[PAIR - 1:PAIR, :])
            kbeta = ks[h] * beta
            kbetas.append(kbeta)
            qgs.append(qs[h] * eg)
            kdecs.append((ks[h] * jnp.exp(glast - gc)).astype(bf16))
            rstacks.append(jnp.concatenate([kbeta * eg, vs[h] * beta], axis=1).astype(bf16))
            gdiff = jnp.where(left, gc[:CHUNK], gc[CHUNK:]) - grow[DN_HEADS + h:DN_HEADS + h + 1, :]
            decays.append(jnp.where(tri_p, jnp.exp(jnp.where(tri_p, gdiff, 0.0)), 0.0))
            egl.append((jnp.exp(gc[CHUNK - 1:CHUNK, :]), jnp.exp(gc[PAIR - 1:PAIR, :])))
            yield

        lps, qkms = [], []
        for h in heads:
            kk = _nt_dot(jnp.concatenate([kbetas[h], qs[h]], axis=0).astype(bf16), ks[h].astype(bf16))
            lps.append(jnp.where(strict_p, jnp.where(left, kk[0:CHUNK], kk[CHUNK:PAIR]) * decays[h], 0.0))
            qkms.append(jnp.where(tri_p, jnp.where(left, kk[PAIR:PAIR + CHUNK], kk[PAIR + CHUNK:]) * decays[h], 0.0)
                        .astype(bf16))
            yield

        pw = [_split2(lp) for lp in lps]
        pw_bd = [(block_diag(hi), block_diag(lo)) for hi, lo in pw]
        ts = [eye_p - lp for lp in lps]
        for _ in range(5):
            pw = [_split2(pair_matmul(*pw[h], *pw_bd[h])) for h in heads]
            pw_bd = [(block_diag(hi), block_diag(lo)) for hi, lo in pw]
            yield
            ts = [ts[h] + pair_matmul(*_split2(ts[h]), *pw_bd[h]) for h in heads]
            yield

        zero_p = jnp.zeros((CHUNK, LANES), bf16)
        mq, bo = [], []
        for h in heads:
            t16 = ts[h].astype(bf16)
            halves = (jnp.where(left, t16, zero_p), jnp.where(left, zero_p, t16))
            qk_halves = (jnp.where(left, qkms[h], zero_p), jnp.where(left, zero_p, qkms[h]))
            wus = [jnp.dot(th, rstacks[h], preferred_element_type=f32).astype(bf16) for th in halves]
            wu_stack = jnp.concatenate(wus, axis=0)
            mq_h, bo_h = [], []
            for c in range(2):
                cs = slice(c * CHUNK, (c + 1) * CHUNK)
                kw = _tn_dot(kdecs[h][cs], wus[c])
                qw = jnp.dot(qk_halves[c], wu_stack, preferred_element_type=f32)
                mq_h.append(jnp.concatenate([kw[:, :DN_DK], qgs[h][cs] - qw[:, :DN_DK]], axis=0).astype(bf16))
                bo_h.append((kw[:, DN_DK:], qw[:, DN_DK:]))
            mq.append(mq_h)
            bo.append(bo_h)
            yield

        prepared[pb] = (mq, bo, egl)

    states = [state_ref[h] for h in heads]

    def sequential(pbs):
        for pb in pbs:
            mq, bo, egl = prepared[pb]
            for c in range(2):
                for h in heads:
                    ms = jnp.dot(mq[h][c], states[h].astype(bf16), preferred_element_type=f32)
                    b_c, o_c = bo[h][c]
                    o = ms[DN_DK:] + o_c
                    states[h] = states[h] * egl[h][c] - ms[:DN_DK] + b_c
                    o = o * lax.rsqrt(jnp.mean(o * o, axis=-1, keepdims=True) + RMS_EPS) * nw
                    rc = pb * PAIR + c * CHUNK
                    z = p_ref[rc:rc + CHUNK, 3 * DN_WIDTH + h * DN_DV:3 * DN_WIDTH + (h + 1) * DN_DV]
                    o_ref[rc:rc + CHUNK, h * DN_DV:(h + 1) * DN_DV] = o * (z * _sigmoid(z))
                yield

    prepared = [None] * n_pb
    pending = iter(())
    for g0 in range(0, n_pb, DN_GROUP):
        group = range(g0, min(g0 + DN_GROUP, n_pb))
        pipes = [pair_block(pb) for pb in group]
        slot = 0
        while pipes:
            pipes = [pipe for pipe in pipes if next(pipe, "done") != "done"]
            slot += 1
            if slot >= DN_SEQ_START and (slot - DN_SEQ_START) % DN_SEQ_EVERY == 0:
                next(pending, None)
        for _ in pending:
            pass
        pending = sequential(group)
    for _ in pending:
        pass
    for h in heads:
        state_ref[h] = states[h]


def _deltanet(p_dn, conv_w8, alog_row, dtb_row, nw_row):
    s = p_dn.shape[0]
    fixed = lambda i: (0, 0)
    return pl.pallas_call(
        _dn_kernel,
        grid=(s // DN_ROWS,),
        in_specs=[
            pl.BlockSpec((DN_ROWS, DN_SLAB), lambda i: (i, 0)),
            pl.BlockSpec((8, CONV_CH), fixed),
            pl.BlockSpec((1, LANES), fixed),
            pl.BlockSpec((1, LANES), fixed),
            pl.BlockSpec((1, DN_DV), fixed),
        ],
        out_specs=pl.BlockSpec((DN_ROWS, DN_HEADS * DN_DV), lambda i: (i, 0)),
        out_shape=jax.ShapeDtypeStruct((s, DN_HEADS * DN_DV), f32),
        scratch_shapes=[
            pltpu.VMEM((DN_HEADS, DN_DK, DN_DV), f32),
            pltpu.VMEM((TAIL, CONV_CH), f32),
            pltpu.VMEM((TAIL + DN_ROWS, CONV_CH), f32),
        ],
        compiler_params=pltpu.CompilerParams(
            dimension_semantics=("arbitrary",), vmem_limit_bytes=VMEM_LIMIT),
        name="deltanet",
    )(p_dn, conv_w8, alog_row, dtb_row, nw_row)


N_PAIRS = SB_HEADS // 2
SB_QBLOCKS = 2


def _sb_kernel(q_ref, kc_ref, kp_ref, kpp_ref, vc_ref, vp_ref, vpp_ref, kv_hbm, o_ref, kbuf, vbuf, sem):
    cur = [pl.ds(sub * SB_BLOCK, SB_BLOCK) for sub in range(SB_QBLOCKS)]
    step = pl.program_id(0)
    _sb_query_block(step * SB_QBLOCKS, q_ref.at[cur[0]], (kc_ref.at[cur[0]], kp_ref, kpp_ref),
                    (vc_ref.at[cur[0]], vp_ref, vpp_ref), kv_hbm, o_ref.at[cur[0]], kbuf, vbuf, sem)
    _sb_query_block(step * SB_QBLOCKS + 1, q_ref.at[cur[1]], (kc_ref.at[cur[1]], kc_ref.at[cur[0]], kp_ref),
                    (vc_ref.at[cur[1]], vc_ref.at[cur[0]], vp_ref), kv_hbm, o_ref.at[cur[1]], kbuf, vbuf, sem)


def _sb_query_block(qb, q_ref, k_blocks, v_blocks, kv_hbm, o_ref, kbuf, vbuf, sem):
    blk = SB_BLOCK
    row = lax.broadcasted_iota(jnp.int32, (blk, blk), 0)
    lane = lax.broadcasted_iota(jnp.int32, (blk, blk), 1)
    diag_mask = row > lane
    even = lane < SB_DH
    suffix = jnp.where(row > lane, 1.0, 0.0).astype(bf16)
    suffix2 = jnp.concatenate([suffix, suffix], axis=0)

    def split_heads(x):
        zero = jnp.zeros_like(x)
        return jnp.where(even, x, zero), jnp.where(even, zero, x)

    def suffix_sums(spm):
        hi = spm.astype(bf16)
        lo = (spm - hi.astype(f32)).astype(bf16)
        return jnp.dot(jnp.concatenate([hi, lo], axis=1), suffix2, preferred_element_type=f32)

    pairs = [slice(p * LANES, (p + 1) * LANES) for p in range(N_PAIRS)]
    k_refs, v_refs = k_blocks[:SB_STATIC_BLOCKS], v_blocks[:SB_STATIC_BLOCKS]

    def fetch(kb):
        start = pl.multiple_of(kb * blk, blk)
        return (pltpu.make_async_copy(kv_hbm.at[pl.ds(start, blk), pl.ds(SB_WIDTH, SB_WIDTH)], kbuf, sem.at[0]),
                pltpu.make_async_copy(kv_hbm.at[pl.ds(start, blk), pl.ds(2 * SB_WIDTH, SB_WIDTH)], vbuf,
                                      sem.at[1]))

    def live(cr):
        m = cr[0]
        for c in cr[1:]:
            m = jnp.minimum(m, c)
        return jnp.min(m) <= -SB_LOG_ZERO

    pen1 = jnp.where(qb >= 1, 0.0, -SB_MASK_PENALTY).astype(f32)
    q_heads = [split_heads(q_ref[:, ps]) for ps in pairs]
    units = [(p, b, hh) for b in range(SB_STATIC_BLOCKS) for p in range(N_PAIRS) for hh in range(2)]
    z = {u: _nt_dot(q_heads[u[0]][u[2]], k_refs[u[1]][:, pairs[u[0]]]) for u in units}
    sp = {u: _softplus(z[u]) for u in units}
    spm = {u: (jnp.where(diag_mask, sp[u], 0.0) if u[1] == 0 else sp[u]) for u in units}
    logw = {u: z[u] - sp[u] - suffix_sums(spm[u]) for u in units}
    keep = {u: jnp.sum(spm[u], axis=1, keepdims=True) for u in units}
    carries = [keep[p, 0, hh] + keep[p, 1, hh] for p in range(N_PAIRS) for hh in range(2)]
    att = {}
    for p, b, hh in units:
        if b == 0:
            att[p, b, hh] = jnp.where(diag_mask, jnp.exp(logw[p, b, hh]), 0.0).astype(bf16)
        else:
            att[p, b, hh] = jnp.exp(logw[p, b, hh] - (keep[p, 0, hh] + pen1)).astype(bf16)
    accs = []
    for p in range(N_PAIRS):
        vals = jnp.concatenate([h for b in range(SB_STATIC_BLOCKS)
                                for h in split_heads(v_refs[b][:, pairs[p]])], axis=0)
        lhs = jnp.concatenate([att[p, b, hh] for b in range(SB_STATIC_BLOCKS) for hh in range(2)], axis=1)
        accs.append(jnp.dot(lhs, vals, preferred_element_type=f32))

    def cond(st):
        kb, go = st[0], st[1]
        return jnp.logical_and(kb >= 0, go)

    def body(st):
        kb = st[0]
        acc_l = list(st[2:2 + N_PAIRS])
        car_l = list(st[2 + N_PAIRS:])
        resident = kb == qb - SB_STATIC_BLOCKS

        @pl.when(resident)
        def _copy():
            kbuf[...] = k_blocks[SB_STATIC_BLOCKS][...]
            vbuf[...] = v_blocks[SB_STATIC_BLOCKS][...]

        @pl.when(jnp.logical_not(resident))
        def _fetch():
            for cp in fetch(kb):
                cp.start()
            for cp in fetch(kb):
                cp.wait()

        new_car = []
        for p, ps in enumerate(pairs):
            atts = []
            for hh, qh in enumerate(split_heads(q_ref[:, ps])):
                zz = _nt_dot(qh, kbuf[:, ps])
                spp = _softplus(zz)
                carry = car_l[2 * p + hh]
                atts.append(jnp.exp(zz - spp - suffix_sums(spp) - carry).astype(bf16))
                new_car.append(carry + jnp.sum(spp, axis=1, keepdims=True))
            vals = jnp.concatenate(split_heads(vbuf[:, ps]), axis=0)
            acc_l[p] = acc_l[p] + jnp.dot(jnp.concatenate(atts, axis=1), vals, preferred_element_type=f32)
        return (kb - 1, live(new_car), *acc_l, *new_car)

    st = lax.while_loop(cond, body, (qb - SB_STATIC_BLOCKS, live(carries), *accs, *carries))
    for p, ps in enumerate(pairs):
        o_ref[:, ps] = st[2 + p]


def _sb_attention(p_sb):
    s = p_sb.shape[0]
    blk = SB_BLOCK

    step_rows = SB_QBLOCKS * blk

    def current(col):
        return pl.BlockSpec((step_rows, SB_WIDTH), lambda i: (i, col))

    def previous(col, back):
        return pl.BlockSpec((blk, SB_WIDTH), lambda i: (jnp.maximum(i * SB_QBLOCKS - back, 0), col))

    return pl.pallas_call(
        _sb_kernel,
        grid=(s // step_rows,),
        in_specs=[current(0), current(1), previous(1, 1), previous(1, 2), current(2), previous(2, 1), previous(2, 2),
                  pl.BlockSpec(memory_space=pl.ANY)],
        out_specs=pl.BlockSpec((step_rows, SB_WIDTH), lambda i: (i, 0)),
        out_shape=jax.ShapeDtypeStruct((s, SB_WIDTH), f32),
        scratch_shapes=[
            pltpu.VMEM((blk, SB_WIDTH), bf16),
            pltpu.VMEM((blk, SB_WIDTH), bf16),
            pltpu.SemaphoreType.DMA((2,)),
        ],
        compiler_params=pltpu.CompilerParams(
            dimension_semantics=("arbitrary",), vmem_limit_bytes=VMEM_LIMIT),
        name="sb_attention",
    )(p_sb, p_sb, p_sb, p_sb, p_sb, p_sb, p_sb, p_sb)


MOE_TILE = 256
PAIRS_PER_GROUP = 6
N_CLASSES = N_GROUPS * PAIRS_PER_GROUP
X1E_W = D_MODEL + LANES
MERGE_ROWS = 512
MERGE_PARTS = 2


def _route(aff, sel):
    neg = -jnp.inf
    best = None
    for gidx in range(N_GROUPS):
        rows = slice(gidx * EXPERTS_PER_GROUP, (gidx + 1) * EXPERTS_PER_GROUP)
        sg = sel[rows]
        idx = lax.broadcasted_iota(jnp.int32, sg.shape, 0) + gidx * EXPERTS_PER_GROUP
        m1 = jnp.max(sg, axis=0, keepdims=True)
        i1 = jnp.min(jnp.where(sg == m1, idx, N_EXPERTS), axis=0, keepdims=True)
        sg2 = jnp.where(idx == i1, neg, sg)
        m2 = jnp.max(sg2, axis=0, keepdims=True)
        i2 = jnp.min(jnp.where(sg2 == m2, idx, N_EXPERTS), axis=0, keepdims=True)
        score = m1 + m2
        if best is None:
            best = (score, i1, i2)
        else:
            better = score > best[0]
            best = (jnp.where(better, score, best[0]),
                    jnp.where(better, i1, best[1]),
                    jnp.where(better, i2, best[2]))
    _, i1, i2 = best
    expert = lax.broadcasted_iota(jnp.int32, aff.shape, 0)
    w1 = jnp.sum(jnp.where(expert == i1, aff, 0.0), axis=0, keepdims=True)
    w2 = jnp.sum(jnp.where(expert == i2, aff, 0.0), axis=0, keepdims=True)
    denom = w1 + w2
    w1, w2 = w1 / denom, w2 / denom
    first_low = i1 < i2
    e_lo = jnp.minimum(i1, i2)
    e_hi = jnp.maximum(i1, i2)
    a = jnp.bitwise_and(e_lo, EXPERTS_PER_GROUP - 1)
    b = jnp.bitwise_and(e_hi, EXPERTS_PER_GROUP - 1)
    pair = jnp.where(a == 0, 0, jnp.where(a == 1, 3, 5)) + (b - a - 1)
    order = jnp.where(pair == 3, 4, jnp.where(pair == 4, 3, pair))
    cls = lax.shift_right_logical(e_lo, 2) * PAIRS_PER_GROUP + order
    w_lo, w_hi = jnp.where(first_low, w1, w2), jnp.where(first_low, w2, w1)
    swapped = pair == PAIRS_PER_GROUP - 1
    return cls, jnp.where(swapped, w_hi, w_lo), jnp.where(swapped, w_lo, w_hi)


CLASS_ROWS = 32


def _merge_kernel(x_ref, oa_ref, ob_ref, g_ref, pa_ref, pb_ref, wo_ref, lg_ref, lb_ref, wr_ref, rb_ref,
                  x1e_ref, route_ref, cnt_ref, run_ref):
    step = pl.program_id(0)

    @pl.when(step == 0)
    def _init():
        run_ref[...] = jnp.zeros_like(run_ref)

    rows = x_ref.shape[0] // MERGE_PARTS
    parts = [slice(i * rows, (i + 1) * rows) for i in range(MERGE_PARTS)]
    a = [jnp.dot(oa_ref[p, :].astype(bf16), pa_ref[...], preferred_element_type=f32) for p in parts]
    b = [jnp.dot(ob_ref[p, :].astype(bf16), pb_ref[...], preferred_element_type=f32) for p in parts]
    merged = [(_sigmoid(g_ref[p, :D_MODEL].astype(f32)) * ai
               + _sigmoid(g_ref[p, D_MODEL:].astype(f32)) * bi).astype(bf16)
              for p, ai, bi in zip(parts, a, b)]
    mix = [jnp.dot(m, wo_ref[...], preferred_element_type=f32) for m in merged]
    x1 = [_layer_norm(DEEPNORM_ALPHA * x_ref[p, :] + mi, lg_ref[...], lb_ref[...]) for p, mi in zip(parts, mix)]
    x1b = [xi.astype(bf16) for xi in x1]
    for p, xi in zip(parts, x1):
        x1e_ref[p, :D_MODEL] = xi

    x_lo = [(xi - xbi.astype(f32)).astype(bf16) for xi, xbi in zip(x1, x1b)]
    t = [_nt_dot(wr_ref[...], xbi) for xbi in x1b]
    logits = [ti[:N_EXPERTS] + ti[N_EXPERTS:] + _nt_dot(wr_ref[:N_EXPERTS, :], xl) for ti, xl in zip(t, x_lo)]
    aff = [_sigmoid(lg) for lg in logits]
    routed = [_route(af, af + rb_ref[...]) for af in aff]

    rr = lax.broadcasted_iota(jnp.int32, (rows, rows), 0)
    cc = lax.broadcasted_iota(jnp.int32, (rows, rows), 1)
    triu = jnp.where(rr <= cc, 1.0, 0.0).astype(bf16)
    class_id = lax.broadcasted_iota(jnp.int32, (CLASS_ROWS, rows), 0)
    for p, (cls, w_lo, w_hi) in zip(parts, routed):
        onehot = class_id == cls
        prefix = jnp.dot(jnp.where(onehot, 1.0, 0.0).astype(bf16), triu, preferred_element_type=f32)
        run = run_ref[...]
        rank = (jnp.sum(jnp.where(onehot, prefix + run, 0.0), axis=0, keepdims=True) - 1.0).astype(jnp.int32)
        run_ref[...] = run + prefix[:, rows - 1:rows]
        pad = jnp.zeros((SUBLANES - 3, rows), jnp.int32)
        route_ref[0, :, p] = jnp.concatenate(
            [cls, lax.shift_right_logical(rank, 7), jnp.bitwise_and(rank, LANES - 1), pad], axis=0)
        w_rows = jnp.concatenate([w_lo, w_hi, jnp.zeros((LANES - 2, rows), f32)], axis=0)
        x1e_ref[p, D_MODEL:] = w_rows.T
    cnt_ref[...] = jnp.broadcast_to(run_ref[...], cnt_ref.shape).astype(jnp.int32)


def _merge(x, o_a, o_b, gates, p_a, p_b, w_out, ln_g, ln_b, wr_cat, r_bias):
    s = x.shape[0]
    tm = MERGE_ROWS
    row = lambda i: (i, 0)
    fixed = lambda i: (0, 0)
    return pl.pallas_call(
        _merge_kernel,
        grid=(s // tm,),
        in_specs=[
            pl.BlockSpec((tm, D_MODEL), row),
            pl.BlockSpec((tm, DN_HEADS * DN_DV), row),
            pl.BlockSpec((tm, SB_WIDTH), row),
            pl.BlockSpec((tm, GATE_SLAB), row),
            pl.BlockSpec((DN_HEADS * DN_DV, D_MODEL), fixed),
            pl.BlockSpec((SB_WIDTH, D_MODEL), fixed),
            pl.BlockSpec((D_MODEL, D_MODEL), fixed),
            pl.BlockSpec((1, D_MODEL), fixed),
            pl.BlockSpec((1, D_MODEL), fixed),
            pl.BlockSpec((2 * N_EXPERTS, D_MODEL), fixed),
            pl.BlockSpec((N_EXPERTS, 1), fixed),
        ],
        out_specs=[
            pl.BlockSpec((tm, X1E_W), row),
            pl.BlockSpec((1, SUBLANES, tm), lambda i: (i, 0, 0)),
            pl.BlockSpec((CLASS_ROWS, LANES), fixed),
        ],
        out_shape=[
            jax.ShapeDtypeStruct((s, X1E_W), f32),
            jax.ShapeDtypeStruct((s // tm, SUBLANES, tm), jnp.int32),
            jax.ShapeDtypeStruct((CLASS_ROWS, LANES), jnp.int32),
        ],
        scratch_shapes=[pltpu.VMEM((CLASS_ROWS, 1), f32)],
        compiler_params=pltpu.CompilerParams(
            dimension_semantics=("arbitrary",), vmem_limit_bytes=VMEM_LIMIT),
        name="merge_router",
    )(x, o_a, o_b, gates, p_a, p_b, w_out, ln_g, ln_b, wr_cat, r_bias)


SLOT_PAIRS = ((0, 1), (0, 2), (0, 3), (1, 3), (1, 2), (3, 2))


def _class_experts():
    slot_a = [g * EXPERTS_PER_GROUP + a for g in range(N_GROUPS) for a, _ in SLOT_PAIRS]
    slot_b = [g * EXPERTS_PER_GROUP + b for g in range(N_GROUPS) for _, b in SLOT_PAIRS]
    return jnp.array(slot_a, jnp.int32), jnp.array(slot_b, jnp.int32)


def _n_tiles(s):
    return -(-(s + N_CLASSES * (MOE_TILE - 1)) // MOE_TILE)


def _route_tables(route, cnt, s):
    counts = cnt[:N_CLASSES, 0]
    padded = (counts + (MOE_TILE - 1)) // MOE_TILE * MOE_TILE
    ends = jnp.cumsum(padded)
    offs = ends - padded
    cls, rank_hi, rank_lo = (route[:, r, :].reshape(s) for r in range(3))
    dest = offs[cls] + rank_hi * LANES + rank_lo
    n_active = (ends[-1] // MOE_TILE).astype(jnp.int32)[None]
    tile_row = jnp.minimum(jnp.arange(_n_tiles(s), dtype=jnp.int32) * MOE_TILE, ends[-1] - 1)
    tile_cls = jnp.minimum(jnp.sum(tile_row[:, None] >= ends[None, :], axis=1), N_CLASSES - 1)
    slot_a, slot_b = _class_experts()
    tile_a, tile_b = slot_a[tile_cls], slot_b[tile_cls]
    first = jnp.ones((1,), jnp.int32)
    new_a = jnp.concatenate([first, (tile_a[1:] != tile_a[:-1]).astype(jnp.int32)])
    new_b = jnp.concatenate([first, (tile_b[1:] != tile_b[:-1]).astype(jnp.int32)])
    tail = jnp.arange(s // MOE_TILE, _n_tiles(s), dtype=jnp.int32)
    pad_start = jnp.concatenate([ends - MOE_TILE, tail * MOE_TILE]).astype(jnp.int32)
    pad_valid = jnp.concatenate([padded > 0, tail >= n_active[0]]).astype(jnp.int32)
    return (dest.astype(jnp.int32), (n_active, tile_a, tile_b, new_a, new_b), pad_start, pad_valid)


def _permute_kernel(dest_ref, pstart_ref, pvalid_ref, x_ref, xs_hbm, zero_ref, sem):
    step = pl.program_id(0)
    tp = x_ref.shape[0]

    @pl.when(step == 0)
    def _fill():
        zero_ref[...] = jnp.zeros_like(zero_ref)

        def fill_copy(c):
            start = pl.multiple_of(pstart_ref[c], MOE_TILE)
            return pltpu.make_async_copy(zero_ref, xs_hbm.at[pl.ds(start, MOE_TILE)], sem.at[1])

        for c in range(pstart_ref.shape[0]):
            @pl.when(pvalid_ref[c] != 0)
            def _start():
                fill_copy(c).start()
        for c in range(pstart_ref.shape[0]):
            @pl.when(pvalid_ref[c] != 0)
            def _wait():
                fill_copy(c).wait()

    base = step * tp

    def issue(g, carry):
        r8 = pl.multiple_of(g * SUBLANES, SUBLANES)
        for j in range(SUBLANES):
            d = dest_ref[base + r8 + j]
            pltpu.make_async_copy(x_ref.at[pl.ds(r8 + j, 1)], xs_hbm.at[pl.ds(d, 1)], sem.at[0]).start(
                priority=j % 2)
        return carry

    lax.fori_loop(0, tp // SUBLANES, issue, 0)
    pltpu.make_async_copy(x_ref, xs_hbm.at[pl.ds(0, tp)], sem.at[0]).wait()


def _permute(x1e, dest, pad_start, pad_valid, tp=1024):
    s = x1e.shape[0]
    return pl.pallas_call(
        _permute_kernel,
        grid_spec=pltpu.PrefetchScalarGridSpec(
            num_scalar_prefetch=3,
            grid=(s // tp,),
            in_specs=[pl.BlockSpec((tp, X1E_W), lambda i, d, ps, pv: (i, 0))],
            out_specs=pl.BlockSpec(memory_space=pl.ANY),
            scratch_shapes=[pltpu.VMEM((MOE_TILE, X1E_W), f32), pltpu.SemaphoreType.DMA((2,))],
        ),
        out_shape=jax.ShapeDtypeStruct((_n_tiles(s) * MOE_TILE, X1E_W), f32),
        compiler_params=pltpu.CompilerParams(
            dimension_semantics=("arbitrary",), vmem_limit_bytes=VMEM_LIMIT),
        name="moe_permute",
    )(dest, pad_start, pad_valid, x1e)


def _moe_kernel(nact_ref, ea_ref, eb_ref, newa_ref, newb_ref, xs_ref,
                wg0_ref, wu0_ref, wd0_ref, wg1_ref, wu1_ref, wd1_ref, ys_ref, wgu_ref, wd_ref):
    j = pl.program_id(0)

    @pl.when(j < nact_ref[0])
    def _tile():
        for slot, (new_ref, wg_ref, wu_ref, wdn_ref) in enumerate(((newa_ref, wg0_ref, wu0_ref, wd0_ref),
                                                                   (newb_ref, wg1_ref, wu1_ref, wd1_ref))):
            @pl.when(new_ref[j] != 0)
            def _recast():
                wgu_ref[2 * slot] = wg_ref[0, 0].astype(bf16)
                wgu_ref[2 * slot + 1] = wu_ref[0, 0].astype(bf16)
                wd_ref[slot] = wdn_ref[0, 0].astype(bf16)

        xb = xs_ref[:, :D_MODEL].astype(bf16)
        acc = None
        for slot in range(2):
            gate = jnp.dot(xb, wgu_ref[2 * slot], preferred_element_type=f32)
            up = jnp.dot(xb, wgu_ref[2 * slot + 1], preferred_element_type=f32)
            hid = (gate * _sigmoid(gate)) * up * xs_ref[:, D_MODEL + slot:D_MODEL + slot + 1]
            part = jnp.dot(hid.astype(bf16), wd_ref[slot], preferred_element_type=f32)
            acc = part if acc is None else acc + part
        ys_ref[...] = acc

    @pl.when(pl.program_id(0) >= nact_ref[0])
    def _unused_tile():
        ys_ref[...] = jnp.zeros_like(ys_ref)


def _moe(xs, tiles, w_gate, w_up, w_down, layer):
    n_tiles = xs.shape[0] // MOE_TILE
    tile = lambda j, na, ea, eb, ca, cb: (jnp.minimum(j, na[0] - 1), 0)
    low = lambda j, na, ea, eb, ca, cb: (layer, ea[j], 0, 0)
    high = lambda j, na, ea, eb, ca, cb: (layer, eb[j], 0, 0)
    up_shape = (1, 1, D_MODEL, D_FF_EXPERT)
    down_shape = (1, 1, D_FF_EXPERT, D_MODEL)
    return pl.pallas_call(
        _moe_kernel,
        grid_spec=pltpu.PrefetchScalarGridSpec(
            num_scalar_prefetch=5,
            grid=(n_tiles,),
            in_specs=[
                pl.BlockSpec((MOE_TILE, X1E_W), tile),
                pl.BlockSpec(up_shape, low), pl.BlockSpec(up_shape, low), pl.BlockSpec(down_shape, low),
                pl.BlockSpec(up_shape, high), pl.BlockSpec(up_shape, high), pl.BlockSpec(down_shape, high),
            ],
            out_specs=pl.BlockSpec((MOE_TILE, D_MODEL), lambda j, na, ea, eb, ca, cb: (j, 0)),
            scratch_shapes=[pltpu.VMEM((4, D_MODEL, D_FF_EXPERT), bf16), pltpu.VMEM((2, D_FF_EXPERT, D_MODEL), bf16)],
        ),
        out_shape=jax.ShapeDtypeStruct((n_tiles * MOE_TILE, D_MODEL), f32),
        compiler_params=pltpu.CompilerParams(
            dimension_semantics=("arbitrary",), vmem_limit_bytes=VMEM_LIMIT),
        name="moe_ffn",
    )(*tiles, xs, w_gate, w_up, w_down, w_gate, w_up, w_down)


def _unpermute_kernel(dest_ref, x1_ref, ys_hbm, lg_ref, lb_ref, out_ref, outb_ref, ybuf, sem):
    i = pl.program_id(0)
    n = pl.num_programs(0)
    tu = x1_ref.shape[0]

    def gather(tile, slot):
        base = tile * tu

        def issue(g, carry):
            r8 = pl.multiple_of(g * SUBLANES, SUBLANES)
            for j in range(SUBLANES):
                d = dest_ref[base + r8 + j]
                pltpu.make_async_copy(ys_hbm.at[pl.ds(d, 1)], ybuf.at[slot, pl.ds(r8 + j, 1)],
                                      sem.at[slot]).start(priority=j % 2)
            return carry

        lax.fori_loop(0, tu // SUBLANES, issue, 0)

    slot = lax.rem(i, 2)

    @pl.when(i == 0)
    def _first():
        gather(0, 0)

    @pl.when(i + 1 < n)
    def _next():
        gather(i + 1, 1 - slot)

    pltpu.make_async_copy(ys_hbm.at[pl.ds(0, tu)], ybuf.at[slot], sem.at[slot]).wait()
    y = _layer_norm(DEEPNORM_ALPHA * x1_ref[...] + ybuf[slot], lg_ref[...], lb_ref[...])
    out_ref[...] = y
    outb_ref[...] = y.astype(bf16)


def _unpermute(x1e, ys, dest, ln_g, ln_b, tu=512):
    s = x1e.shape[0]
    row = lambda i, d: (i, 0)
    fixed = lambda i, d: (0, 0)
    return pl.pallas_call(
        _unpermute_kernel,
        grid_spec=pltpu.PrefetchScalarGridSpec(
            num_scalar_prefetch=1,
            grid=(s // tu,),
            in_specs=[
                pl.BlockSpec((tu, D_MODEL), row),
                pl.BlockSpec(memory_space=pl.ANY),
                pl.BlockSpec((1, D_MODEL), fixed),
                pl.BlockSpec((1, D_MODEL), fixed),
            ],
            out_specs=[pl.BlockSpec((tu, D_MODEL), row), pl.BlockSpec((tu, D_MODEL), row)],
            scratch_shapes=[pltpu.VMEM((2, tu, D_MODEL), f32), pltpu.SemaphoreType.DMA((2,))],
        ),
        out_shape=[jax.ShapeDtypeStruct((s, D_MODEL), f32), jax.ShapeDtypeStruct((s, D_MODEL), bf16)],
        compiler_params=pltpu.CompilerParams(
            dimension_semantics=("arbitrary",), vmem_limit_bytes=VMEM_LIMIT),
        name="moe_unpermute_ln2",
    )(dest, x1e, ys, ln_g, ln_b)


def kernel(x, w_in, conv_w, dn_a_log, dn_dt_bias, dn_norm_w, p_a, p_b, w_out, ln1_g, ln1_b, w_router, router_bias, w_gate, w_up, w_down, ln2_g, ln2_b):
    bsz, s, _ = x.shape
    assert bsz == 1 and s % DN_ROWS == 0 and SB_QBLOCKS == 2
    xf = x[0]
    xb = xf

    c_ba = 4 * DN_WIDTH
    c_sb = c_ba + 2 * DN_HEADS
    c_gate = c_sb + SB_SLAB
    wr_t = w_router.T
    wr_hi = wr_t.astype(bf16)
    wr_cat = jnp.concatenate([wr_hi, (wr_t - wr_hi.astype(f32)).astype(bf16)], axis=0)
    rb_col = router_bias[:, None]
    head_pad = ((0, 0), (DN_HEADS, LANES - 2 * DN_HEADS))
    w_in_t = jnp.swapaxes(w_in, 1, 2)

    for l in range(DEPTH):
        w = w_in_t[l]
        w_dn = jnp.concatenate([w[:c_ba], jnp.pad(w[c_ba:c_sb], ((0, LANES - 2 * DN_HEADS), (0, 0)))],
                               axis=0).astype(bf16)
        w_sb = jnp.concatenate([w[c_sb:c_sb + SB_WIDTH] * (SB_DH ** -0.5),
                                w[c_sb + SB_WIDTH:c_gate]], axis=0).astype(bf16)
        w_g = w[c_gate:].astype(bf16)

        p_dn, p_sb, gates = _proj(xb, w_dn, w_sb, w_g)
        o_a = _deltanet(p_dn,
                        jnp.pad(conv_w[l], ((0, 8 - CONV_K), (0, 0))),
                        jnp.pad(dn_a_log[l][None, :], head_pad),
                        jnp.pad(dn_dt_bias[l][None, :], head_pad),
                        dn_norm_w[l][None, :])
        o_b = _sb_attention(p_sb)
        x1e, route, cnt = _merge(xf, o_a, o_b, gates,
                                p_a[l].astype(bf16), p_b[l].astype(bf16), w_out[l].astype(bf16),
                                ln1_g[l][None, :], ln1_b[l][None, :], wr_cat, rb_col)
        dest, tiles, pad_start, pad_valid = _route_tables(route, cnt, s)
        xs = _permute(x1e, dest, pad_start, pad_valid)
        ys = _moe(xs, tiles, w_gate, w_up, w_down, l)
        xf, xb = _unpermute(x1e, ys, dest, ln2_g[l][None, :], ln2_b[l][None, :])
    return xf[None]
```
